```python
import math
import jax, jax.numpy as jnp
from jax import lax
import numpy as np

D_MODEL = 2048
BATCH = 8
SEQ = 2048
DEPTH = 1
DEC_BATCH = 2
DEC_SEQ = 16384
PAST_LEN = 128

HEAD_DIM = 64
N_HEADS_A = 16
N_KV_A = 4
GROUP_A = N_HEADS_A // N_KV_A
N_HEADS_B = 16
WINDOW = 128
ATTN_BLOCK = 128
T5_BUCKETS = 32
T5_MAX_DIST = 128
GRID_W = 64
NA_ROWS = 8
NA_COLS = 16
N_EXPERTS = 32
TOP_K = 4
D_FF = D_MODEL
SWIGLU_LIMIT = 7.0
SWIGLU_ALPHA = 1.702
EXPERT_BLOCK = 256
RMS_EPS = 1e-6

Q_A = N_HEADS_A * HEAD_DIM
KV_A = N_KV_A * HEAD_DIM
W_B = N_HEADS_B * HEAD_DIM
IN_COLS = Q_A + 2 * KV_A + 3 * W_B + 2 * D_MODEL

kernel_name = 'hybrid_gated_window_natten_moe_encoder'


def rmsnorm(x, g):
    xf = x.astype(jnp.float32)
    y = xf * lax.rsqrt(jnp.mean(xf * xf, axis=-1, keepdims=True) + RMS_EPS) * g.astype(jnp.float32)
    return y.astype(x.dtype)


def t5_bucket(rel):
    nb = T5_BUCKETS // 2
    max_exact = nb // 2
    ret = jnp.where(rel > 0, nb, 0)
    n = jnp.abs(rel)
    nf = jnp.maximum(n, 1).astype(jnp.float32)
    large = max_exact + (jnp.log(nf / max_exact) / math.log(T5_MAX_DIST / max_exact) * (nb - max_exact)).astype(jnp.int32)
    large = jnp.minimum(large, nb - 1)
    return ret + jnp.where(n < max_exact, n, large)


def window_attention(q, k, v, sink, t5_bias):
    bsz, s_len = q.shape[0], q.shape[1]
    nblk = s_len // ATTN_BLOCK
    kl = ATTN_BLOCK + 2 * WINDOW
    kp = jnp.pad(k, ((0, 0), (WINDOW, WINDOW), (0, 0), (0, 0)))
    vp = jnp.pad(v, ((0, 0), (WINDOW, WINDOW), (0, 0), (0, 0)))
    rel = jnp.arange(kl)[None, :] - WINDOW - jnp.arange(ATTN_BLOCK)[:, None]
    band = jnp.abs(rel) <= WINDOW
    bias = t5_bias[t5_bucket(rel)].astype(jnp.float32)
    bias = bias.transpose(2, 0, 1).reshape(N_KV_A, GROUP_A, ATTN_BLOCK, kl)
    sink_f = sink.astype(jnp.float32).reshape(N_KV_A, GROUP_A, 1, 1)
    scale = HEAD_DIM ** -0.5
    qblocks = q.reshape(bsz, nblk, ATTN_BLOCK, N_KV_A, GROUP_A, HEAD_DIM).transpose(1, 0, 2, 3, 4, 5)

    def block(args):
        qb, b = args
        start = b * ATTN_BLOCK
        kb = lax.dynamic_slice_in_dim(kp, start, kl, axis=1)
        vb = lax.dynamic_slice_in_dim(vp, start, kl, axis=1)
        kpos = start - WINDOW + jnp.arange(kl)
        valid = band & ((kpos >= 0) & (kpos < s_len))[None, :]
        s = jnp.einsum('bqkgd,bskd->bkgqs', qb, kb, preferred_element_type=jnp.float32) * scale + bias
        s = jnp.where(valid, s, -jnp.inf)
        m = jnp.maximum(jnp.max(s, axis=-1, keepdims=True), sink_f)
        p = jnp.exp(s - m)
        denom = jnp.sum(p, axis=-1, keepdims=True) + jnp.exp(sink_f - m)
        return jnp.einsum('bkgqs,bskd->bqkgd', (p / denom).astype(vb.dtype), vb)

    out = lax.map(block, (qblocks, jnp.arange(nblk)))
    return out.transpose(1, 0, 2, 3, 4, 5).reshape(bsz, s_len, Q_A)


def neighbourhood_attention(q, k, v, na_bias):
    bsz, s_len = q.shape[0], q.shape[1]
    rows = s_len // GRID_W
    kr = min(NA_ROWS, rows)
    ncb = GRID_W // NA_COLS
    kc = 2 * NA_COLS
    c0 = np.arange(ncb) * NA_COLS
    band_start = np.minimum(np.clip(c0 - NA_COLS // 2, 0, GRID_W - NA_COLS), GRID_W - kc)
    key_cols = band_start[:, None] + np.arange(kc)[None, :]
    q_cols = c0[:, None] + np.arange(NA_COLS)[None, :]
    win_start = np.clip(q_cols - NA_COLS // 2, 0, GRID_W - NA_COLS)
    kcol = key_cols[:, None, :]
    col_valid = (kcol >= win_start[..., None]) & (kcol < win_start[..., None] + NA_COLS)
    col_idx = np.clip(kcol - q_cols[:, :, None] + NA_COLS - 1, 0, 2 * NA_COLS - 2)
    col_valid4 = jnp.asarray(col_valid)[:, :, None, :]
    scale = HEAD_DIM ** -0.5
    qg = q.reshape(bsz, rows, ncb, NA_COLS, N_HEADS_B, HEAD_DIM).transpose(1, 0, 2, 3, 4, 5)
    kg = k.reshape(bsz, rows, GRID_W, N_HEADS_B, HEAD_DIM)
    vg = v.reshape(bsz, rows, GRID_W, N_HEADS_B, HEAD_DIM)

    def row(args):
        qr, r = args
        rs = jnp.clip(r - kr // 2, 0, rows - kr)
        kb = lax.dynamic_slice_in_dim(kg, rs, kr, axis=1)[:, :, key_cols]
        vb = lax.dynamic_slice_in_dim(vg, rs, kr, axis=1)[:, :, key_cols]
        drow = rs + jnp.arange(kr) - r + NA_ROWS - 1
        bias = na_bias[:, drow[None, None, :, None], col_idx[:, :, None, :]].astype(jnp.float32)
        s = jnp.einsum('bnqhd,brnkhd->bhnqrk', qr, kb, preferred_element_type=jnp.float32) * scale + bias
        s = jnp.where(col_valid4, s, -jnp.inf)
        p = jax.nn.softmax(s.reshape(bsz, N_HEADS_B, ncb, NA_COLS, kr * kc), axis=-1).reshape(s.shape)
        return jnp.einsum('bhnqrk,brnkhd->bnqhd', p.astype(vb.dtype), vb)

    out = lax.map(row, (qg, jnp.arange(rows)))
    return out.transpose(1, 0, 2, 3, 4, 5).reshape(bsz, s_len, W_B)


def mixer_sublayer(h, w_in, g_q_a, g_k_a, g_q_b, g_k_b, sink_a, t5_bias, na_bias, w_o_a, w_o_b, w_out):
    bsz, s_len, _ = h.shape
    proj = h @ w_in
    o1 = Q_A
    o2 = o1 + KV_A
    o3 = o2 + KV_A
    o4 = o3 + W_B
    o5 = o4 + W_B
    o6 = o5 + W_B
    o7 = o6 + D_MODEL
    qa, ka, va, qb, kb, vb, gate_a, gate_b = jnp.split(proj, [o1, o2, o3, o4, o5, o6, o7], axis=-1)
    qa = rmsnorm(qa.reshape(bsz, s_len, N_KV_A, GROUP_A, HEAD_DIM), g_q_a)
    ka = rmsnorm(ka.reshape(bsz, s_len, N_KV_A, HEAD_DIM), g_k_a)
    va = va.reshape(bsz, s_len, N_KV_A, HEAD_DIM)
    qb = rmsnorm(qb.reshape(bsz, s_len, N_HEADS_B, HEAD_DIM), g_q_b)
    kb = rmsnorm(kb.reshape(bsz, s_len, N_HEADS_B, HEAD_DIM), g_k_b)
    vb = vb.reshape(bsz, s_len, N_HEADS_B, HEAD_DIM)
    ya = window_attention(qa, ka, va, sink_a, t5_bias) @ w_o_a
    yb = neighbourhood_attention(qb, kb, vb, na_bias) @ w_o_b
    ga = jax.nn.sigmoid(gate_a.astype(jnp.float32)).astype(h.dtype)
    gb = jax.nn.sigmoid(gate_b.astype(jnp.float32)).astype(h.dtype)
    return (ga * ya + gb * yb) @ w_out


def moe_sublayer(h, w_router, b_router, w_gu, b_gu, w_down, b_down):
    bsz, s_len, d = h.shape
    n_tok = bsz * s_len
    xf = h.reshape(n_tok, d)
    logits = (xf @ w_router + b_router).astype(jnp.float32)
    top_logits, top_idx = lax.top_k(logits, TOP_K)
    top_w = jax.nn.softmax(top_logits, axis=-1)
    n_assign = n_tok * TOP_K
    e_flat = top_idx.reshape(-1)
    tok_flat = jnp.arange(n_assign, dtype=jnp.int32) // TOP_K
    w_flat = top_w.reshape(-1)
    order = jnp.argsort(e_flat)
    e_sorted = e_flat[order]
    counts = jnp.bincount(e_flat, length=N_EXPERTS)
    padded = (counts + EXPERT_BLOCK - 1) // EXPERT_BLOCK * EXPERT_BLOCK
    start = jnp.cumsum(counts) - counts
    pend = jnp.cumsum(padded)
    pstart = pend - padded
    dest = pstart[e_sorted] + jnp.arange(n_assign, dtype=jnp.int32) - start[e_sorted]
    n_blocks = -(-n_assign // EXPERT_BLOCK) + N_EXPERTS
    n_rows = n_blocks * EXPERT_BLOCK
    buf_tok = jnp.full((n_rows,), n_tok, jnp.int32).at[dest].set(tok_flat[order])
    buf_w = jnp.zeros((n_rows,), jnp.float32).at[dest].set(w_flat[order])
    block_e = jnp.minimum(jnp.searchsorted(pend, jnp.arange(n_blocks) * EXPERT_BLOCK, side='right'), N_EXPERTS - 1)
    xpad = jnp.concatenate([xf, jnp.zeros((1, d), xf.dtype)], axis=0)
    xin = xpad[buf_tok].reshape(n_blocks, EXPERT_BLOCK, d)

    def expert_block(args):
        xb, e = args
        gu = xb @ w_gu[e] + b_gu[e]
        gate, lin = jnp.split(gu, 2, axis=-1)
        gate = jnp.minimum(gate, SWIGLU_LIMIT)
        lin = jnp.clip(lin, -SWIGLU_LIMIT, SWIGLU_LIMIT)
        act = gate * jax.nn.sigmoid(SWIGLU_ALPHA * gate) * (lin + 1)
        return act @ w_down[e] + b_down[e]

    yb = lax.map(expert_block, (xin, block_e)).reshape(n_rows, d)
    out = jnp.zeros((n_tok + 1, d), jnp.float32).at[buf_tok].add(yb.astype(jnp.float32) * buf_w[:, None])[:n_tok]
    return out.astype(h.dtype).reshape(bsz, s_len, d)


def encoder_trunk(x, c, w_ada, b_ada, g_norm1, w_in, g_q_a, g_k_a, g_q_b, g_k_b, sink_a, t5_bias, na_bias,
                  w_o_a, w_o_b, w_out, g_norm2, w_router, b_router, w_gu, b_gu, w_down, b_down):
    for l in range(DEPTH):
        mod = jax.nn.silu(c) @ w_ada[l] + b_ada[l]
        sh1, sc1, ga1, sh2, sc2, ga2 = jnp.split(mod[:, None, :], 6, axis=-1)
        h = rmsnorm(x, g_norm1[l]) * (1 + sc1) + sh1
        x = x + ga1 * mixer_sublayer(h, w_in[l], g_q_a[l], g_k_a[l], g_q_b[l], g_k_b[l], sink_a[l], t5_bias,
                                     na_bias[l], w_o_a[l], w_o_b[l], w_out[l])
        h = rmsnorm(x, g_norm2[l]) * (1 + sc2) + sh2
        x = x + ga2 * moe_sublayer(h, w_router[l], b_router[l], w_gu[l], b_gu[l], w_down[l], b_down[l])
    return x


def setup_inputs(seed: int = 0) -> dict:
    key = jax.random.key(seed)
    ks = jax.random.split(key, 25)
    f32 = jnp.float32
    d = D_MODEL

    def nrm(k, shape, scale):
        return jax.random.normal(k, shape, f32) * scale

    return {
        'x_prompt': nrm(ks[0], (BATCH, SEQ, d), 1.0),
        'x_sample': nrm(ks[1], (DEC_BATCH, DEC_SEQ, d), 1.0),
        'c_prompt': nrm(ks[2], (BATCH, d), 1.0),
        'c_sample': nrm(ks[3], (DEC_BATCH, d), 1.0),
        'w_ada': nrm(ks[4], (DEPTH, d, 6 * d), 0.5 * d ** -0.5),
        'b_ada': nrm(ks[5], (DEPTH, 6 * d), 0.02),
        'g_norm1': 1.0 + nrm(ks[6], (DEPTH, d), 0.02),
        'w_in': nrm(ks[7], (DEPTH, d, IN_COLS), d ** -0.5),
        'g_q_a': 1.0 + nrm(ks[8], (DEPTH, HEAD_DIM), 0.02),
        'g_k_a': 1.0 + nrm(ks[9], (DEPTH, HEAD_DIM), 0.02),
        'g_q_b': 1.0 + nrm(ks[10], (DEPTH, HEAD_DIM), 0.02),
        'g_k_b': 1.0 + nrm(ks[11], (DEPTH, HEAD_DIM), 0.02),
        'sink_a': nrm(ks[12], (DEPTH, N_HEADS_A), 0.5),
        't5_bias': nrm(ks[13], (T5_BUCKETS, N_HEADS_A), 0.5),
        'na_bias': nrm(ks[14], (DEPTH, N_HEADS_B, 2 * NA_ROWS - 1, 2 * NA_COLS - 1), 0.5),
        'w_o_a': nrm(ks[15], (DEPTH, Q_A, d), Q_A ** -0.5),
        'w_o_b': nrm(ks[16], (DEPTH, W_B, d), W_B ** -0.5),
        'w_out': nrm(ks[17], (DEPTH, d, d), d ** -0.5),
        'g_norm2': 1.0 + nrm(ks[18], (DEPTH, d), 0.02),
        'w_router': nrm(ks[19], (DEPTH, d, N_EXPERTS), d ** -0.5),
        'b_router': nrm(ks[20], (DEPTH, N_EXPERTS), 0.01),
        'w_gu': nrm(ks[21], (DEPTH, N_EXPERTS, d, 2 * D_FF), d ** -0.5),
        'b_gu': nrm(ks[22], (DEPTH, N_EXPERTS, 2 * D_FF), 0.02),
        'w_down': nrm(ks[23], (DEPTH, N_EXPERTS, D_FF, d), D_FF ** -0.5),
        'b_down': nrm(ks[24], (DEPTH, N_EXPERTS, d), 0.02),
    }


def reference(x_prompt, x_sample, c_prompt, c_sample, w_ada, b_ada, g_norm1, w_in, g_q_a, g_k_a, g_q_b, g_k_b,
              sink_a, t5_bias, na_bias, w_o_a, w_o_b, w_out, g_norm2, w_router, b_router, w_gu, b_gu, w_down, b_down):
    y_prompt = encoder_trunk(x_prompt, c_prompt, w_ada, b_ada, g_norm1, w_in, g_q_a, g_k_a, g_q_b, g_k_b, sink_a,
                             t5_bias, na_bias, w_o_a, w_o_b, w_out, g_norm2, w_router, b_router, w_gu, b_gu,
                             w_down, b_down)
    y_sample = encoder_trunk(x_sample, c_sample, w_ada, b_ada, g_norm1, w_in, g_q_a, g_k_a, g_q_b, g_k_b, sink_a,
                             t5_bias, na_bias, w_o_a, w_o_b, w_out, g_norm2, w_router, b_router, w_gu, b_gu,
                             w_down, b_down)
    return (y_prompt, y_sample)
```

```python
import functools
import math

import numpy as np
import jax
import jax.numpy as jnp
from jax import lax
from jax.experimental import pallas as pl
from jax.experimental.pallas import tpu as pltpu

F32 = jnp.float32
BF16 = jnp.bfloat16

HEAD_DIM = 64
N_HEADS_A = 16
N_KV_A = 4
GROUP_A = N_HEADS_A // N_KV_A
N_HEADS_B = 16
WINDOW = 128
ATTN_BLOCK = 128
T5_BUCKETS = 32
T5_MAX_DIST = 128
GRID_W = 64
NA_ROWS = 8
NA_COLS = 16
TOP_K = 4
SWIGLU_LIMIT = 7.0
SWIGLU_ALPHA = 1.702
RMS_EPS = 1e-6

Q_A = N_HEADS_A * HEAD_DIM
KV_A = N_KV_A * HEAD_DIM
W_B = N_HEADS_B * HEAD_DIM

LANES = 128
MXU_DIM = 256
VMEM_LIMIT_BYTES = 56 * 1024 * 1024
MASKED = -1e30

N_HEAD_PAIRS = Q_A // LANES
COL_QA = 0
COL_QB = COL_QA + Q_A
COL_KB = COL_QB + W_B
COL_KA = COL_KB + W_B
COL_VA = COL_KA + N_KV_A * LANES
COL_VB = COL_VA + N_KV_A * LANES
COL_END = COL_VB + W_B
NA_GROUP_ROWS = 2
NA_KEY_ROWS = NA_GROUP_ROWS + NA_ROWS - 1
NA_HALO_ROWS = NA_ROWS


def _sigmoid(x):
    return 1.0 / (1.0 + jnp.exp(-x))


def _cparams(sem):
    return pltpu.CompilerParams(dimension_semantics=sem, vmem_limit_bytes=VMEM_LIMIT_BYTES)


def _ada_kernel(c_ref, w_ref, b_ref, o_ref):
    c = c_ref[...]
    a = (c * _sigmoid(c)).astype(BF16)
    o_ref[...] = jnp.dot(a, w_ref[...].astype(BF16), preferred_element_type=F32) + b_ref[...]


def _ada_call(c_pad, w_ada, b_ada):
    rows, d = c_pad.shape
    n = w_ada.shape[1]
    tn = min(n, 1024)
    return pl.pallas_call(
        _ada_kernel,
        grid=(n // tn,),
        in_specs=[pl.BlockSpec((rows, d), lambda j: (0, 0)),
                  pl.BlockSpec((d, tn), lambda j: (0, j)),
                  pl.BlockSpec((1, tn), lambda j: (0, j))],
        out_specs=pl.BlockSpec((rows, tn), lambda j: (0, j)),
        out_shape=jax.ShapeDtypeStruct((rows, n), F32),
        compiler_params=_cparams(("arbitrary",)),
        name="ada_ln",
    )(c_pad, w_ada, b_ada.reshape(1, n))


def _two_group_specs(tm, d, n_prompt_tiles, n_sample_tiles):
    xp = pl.BlockSpec((tm, d), lambda i, *_: (jnp.minimum(i, n_prompt_tiles - 1), 0))
    xs = pl.BlockSpec((tm, d), lambda i, *_: (jnp.clip(i - n_prompt_tiles, 0, n_sample_tiles - 1), 0))
    return xp, xs


def _norm_mod_kernel(xp_ref, xs_ref, mod_ref, g_ref, o_ref, *, n_prompt_tiles):
    i = pl.program_id(0)
    x = jnp.where(i < n_prompt_tiles, xp_ref[...], xs_ref[...])
    y = x * lax.rsqrt(jnp.mean(x * x, axis=-1, keepdims=True) + RMS_EPS) * g_ref[...]
    m = mod_ref[0]
    o_ref[...] = (y * (1.0 + m[1:2]) + m[0:1]).astype(BF16)


def _norm_mod_call(xp, xs, mod_seg, g, seg):
    d = xp.shape[1]
    tm = min(512, seg)
    npt, nst = xp.shape[0] // tm, xs.shape[0] // tm
    xp_spec, xs_spec = _two_group_specs(tm, d, npt, nst)
    return pl.pallas_call(
        functools.partial(_norm_mod_kernel, n_prompt_tiles=npt),
        grid=(npt + nst,),
        in_specs=[xp_spec, xs_spec,
                  pl.BlockSpec((1, 6, d), lambda i: (i * tm // seg, 0, 0)),
                  pl.BlockSpec((1, d), lambda i: (0, 0))],
        out_specs=pl.BlockSpec((tm, d), lambda i: (i, 0)),
        out_shape=jax.ShapeDtypeStruct((xp.shape[0] + xs.shape[0], d), BF16),
        compiler_params=_cparams(("arbitrary",)),
        name="norm1_mod",
    )(xp, xs, mod_seg, g.reshape(1, d))


def _group_rms(y, g, ones_ref):
    sq = (y * y).astype(BF16)
    parts = []
    for c in range(y.shape[1] // MXU_DIM):
        parts.append(jnp.dot(sq[:, c * MXU_DIM:(c + 1) * MXU_DIM], ones_ref[...], preferred_element_type=F32))
    ss = jnp.concatenate(parts, axis=1) if len(parts) > 1 else parts[0]
    return y * lax.rsqrt(ss * (1.0 / HEAD_DIM) + RMS_EPS) * g


def _inproj_kernel(h_ref, w_ref, g_ref, ones_ref, o_ref, *, tn, gate_tiles):
    j = pl.program_id(1)
    y = jnp.dot(h_ref[...], w_ref[...], preferred_element_type=F32)
    half = tn // 2
    mixed_tile = gate_tiles + COL_KA // tn
    plain_tile = gate_tiles + COL_VB // tn

    @pl.when(j < gate_tiles)
    def _():
        o_ref[...] = _sigmoid(y).astype(BF16)

    @pl.when((j >= gate_tiles) & (j < mixed_tile))
    def _():
        o_ref[...] = _group_rms(y, g_ref[...], ones_ref).astype(BF16)

    @pl.when(j == mixed_tile)
    def _():
        o_ref[:, :half] = _group_rms(y[:, :half], g_ref[:, :half], ones_ref).astype(BF16)
        o_ref[:, half:] = y[:, half:].astype(BF16)

    @pl.when(j == plain_tile)
    def _():
        o_ref[...] = y.astype(BF16)


def _inproj_call(h, w, gvec, ones_bd, tm):
    t, d = h.shape
    pw = w.shape[1]
    tn = 1024
    assert (2 * d) % tn == 0 and COL_KA % tn == 0 and COL_VA - COL_KA == tn // 2 and COL_VB % tn == 0
    return pl.pallas_call(
        functools.partial(_inproj_kernel, tn=tn, gate_tiles=2 * d // tn),
        grid=(t // tm, pw // tn),
        in_specs=[pl.BlockSpec((tm, d), lambda i, j: (i, 0)),
                  pl.BlockSpec((d, tn), lambda i, j: (0, j)),
                  pl.BlockSpec((1, tn), lambda i, j: (0, j)),
                  pl.BlockSpec((MXU_DIM, MXU_DIM), lambda i, j: (0, 0))],
        out_specs=pl.BlockSpec((tm, tn), lambda i, j: (i, j)),
        out_shape=jax.ShapeDtypeStruct((t, pw), BF16),
        compiler_params=_cparams(("arbitrary", "arbitrary")),
        name="in_proj",
    )(h, w, gvec, ones_bd)


def _win_attn_kernel(flags_ref, q_ref, kp_ref, kc_ref, kn_ref, vp_ref, vc_ref, vn_ref, bias_ref, sink_ref,
                     o_ref, kcat, vcat, *, qs, nb):
    p = pl.program_id(0)
    sb = pl.program_id(1)
    blk = ATTN_BLOCK
    kcat[0:blk] = kp_ref[...]
    kcat[blk:blk + qs] = kc_ref[...]
    kcat[blk + qs:] = kn_ref[...]
    vcat[0:blk] = vp_ref[...]
    vcat[blk:blk + qs] = vc_ref[...]
    vcat[blk + qs:] = vn_ref[...]
    prev_ok = flags_ref[2 * sb]
    next_ok = flags_ref[2 * sb + 1]
    low_half = lax.broadcasted_iota(jnp.int32, (blk, LANES), 1) < HEAD_DIM
    row = lax.broadcasted_iota(jnp.int32, (2 * blk, 1), 0)
    sinkcol = jnp.where(row < blk, sink_ref[2 * p], sink_ref[2 * p + 1])
    kl = blk + 2 * WINDOW

    def body(b, carry):
        r0 = pl.multiple_of(b * blk, blk)
        q = q_ref[pl.ds(r0, blk), :]
        zero = jnp.zeros_like(q)
        q2 = jnp.concatenate([jnp.where(low_half, q, zero), jnp.where(low_half, zero, q)], axis=0)
        kw = kcat[pl.ds(r0, kl), :]
        vw = vcat[pl.ds(r0, kl), :]
        s = lax.dot_general(q2, kw, (((1,), (1,)), ((), ())), preferred_element_type=F32)
        variant = jnp.where((b == 0) & (prev_ok == 0), 1, jnp.where((b == nb - 1) & (next_ok == 0), 2, 0))
        s = s + bias_ref[variant].reshape(2 * blk, kl)
        m = jnp.maximum(jnp.max(s, axis=-1, keepdims=True), sinkcol)
        e = jnp.exp(s - m)
        denom = jnp.sum(e, axis=-1, keepdims=True) + jnp.exp(sinkcol - m)
        o2 = jnp.dot(e.astype(BF16), vw, preferred_element_type=F32) * (1.0 / denom)
        o_ref[pl.ds(r0, blk), :] = jnp.where(low_half, o2[:blk], o2[blk:]).astype(BF16)
        return carry

    lax.fori_loop(0, nb, body, 0)


def _win_attn_call(proj, bias3, sink, flags, qs, base):
    t = proj.shape[0]
    blk = ATTN_BLOCK
    nb = qs // blk
    n_super = t // qs
    n_blk_rows = t // blk
    kl = blk + 2 * WINDOW
    ka0, va0 = base + COL_KA // LANES, base + COL_VA // LANES

    def cur(col0):
        return pl.BlockSpec((qs, LANES), lambda p, s, f: (s, col0 + p // 2))

    def prev(col0):
        return pl.BlockSpec((blk, LANES), lambda p, s, f: (jnp.maximum(s * nb - 1, 0), col0 + p // 2))

    def nxt(col0):
        return pl.BlockSpec((blk, LANES), lambda p, s, f: (jnp.minimum((s + 1) * nb, n_blk_rows - 1), col0 + p // 2))

    grid_spec = pltpu.PrefetchScalarGridSpec(
        num_scalar_prefetch=1,
        grid=(N_HEAD_PAIRS, n_super),
        in_specs=[pl.BlockSpec((qs, LANES), lambda p, s, f: (s, base + COL_QA // LANES + p)),
                  prev(ka0), cur(ka0), nxt(ka0), prev(va0), cur(va0), nxt(va0),
                  pl.BlockSpec((3, 2, blk, kl), lambda p, s, f: (0, p, 0, 0)),
                  pl.BlockSpec(memory_space=pltpu.SMEM)],
        out_specs=pl.BlockSpec((qs, LANES), lambda p, s, f: (s, p)),
        scratch_shapes=[pltpu.VMEM((qs + 2 * blk, LANES), BF16), pltpu.VMEM((qs + 2 * blk, LANES), BF16)],
    )
    return pl.pallas_call(
        functools.partial(_win_attn_kernel, qs=qs, nb=nb),
        grid_spec=grid_spec,
        out_shape=jax.ShapeDtypeStruct((t, Q_A), BF16),
        compiler_params=_cparams(("arbitrary", "arbitrary")),
        name="win_attn",
    )(flags, proj, proj, proj, proj, proj, proj, proj, bias3, sink)


def _t5_bucket(rel):
    nb = T5_BUCKETS // 2
    max_exact = nb // 2
    ret = jnp.where(rel > 0, nb, 0)
    n = jnp.abs(rel)
    nf = jnp.maximum(n, 1).astype(jnp.float32)
    large = max_exact + (jnp.log(nf / max_exact) / math.log(T5_MAX_DIST / max_exact) * (nb - max_exact)).astype(jnp.int32)
    large = jnp.minimum(large, nb - 1)
    return ret + jnp.where(n < max_exact, n, large)


def _window_bias(t5_bias):
    blk, kl = ATTN_BLOCK, ATTN_BLOCK + 2 * WINDOW
    col = jnp.arange(kl)[None, :]
    rel = col - WINDOW - jnp.arange(blk)[:, None]
    band = jnp.abs(rel) <= WINDOW
    bias = t5_bias[_t5_bucket(rel)].astype(F32).transpose(2, 0, 1)
    keep = jnp.stack([band, band & (col >= WINDOW), band & (col < WINDOW + blk)])
    return jnp.where(keep[:, None], bias[None], MASKED)


def _na_plan(seq_lens, seg):
    qr, nkr = NA_GROUP_ROWS, NA_KEY_ROWS
    configs, cfg_ids, kstarts = {}, [], []
    tok = 0
    for s_len in seq_lens:
        rows = s_len // GRID_W
        assert rows >= nkr and rows % qr == 0
        for r in range(0, rows, qr):
            us = min(max(r - NA_ROWS // 2, 0), rows - nkr)
            rel = tuple(min(max(r + q - NA_ROWS // 2, 0), rows - NA_ROWS) - us for q in range(qr))
            key = (r - us, rel)
            cfg_ids.append(configs.setdefault(key, len(configs)))
            g_tok = tok + r * GRID_W
            seg_tok0 = (g_tok // seg) * seg
            kstarts.append(tok + us * GRID_W - seg_tok0 + NA_HALO_ROWS * GRID_W)
        tok += s_len
    m, nk = qr * GRID_W, nkr * GRID_W
    valid = np.zeros((len(configs), m, nk), bool)
    dr = np.zeros((len(configs), m, nk), np.int32)
    dc = np.zeros((len(configs), m, nk), np.int32)
    qi = np.arange(m)
    ki = np.arange(nk)
    q_row, q_col = qi // GRID_W, qi % GRID_W
    k_row, k_col = ki // GRID_W, ki % GRID_W
    win_start = np.clip(q_col - NA_COLS // 2, 0, GRID_W - NA_COLS)
    col_ok = (k_col[None, :] >= win_start[:, None]) & (k_col[None, :] < win_start[:, None] + NA_COLS)
    for (r_us, rel), c in configs.items():
        rel_q = np.asarray(rel)[q_row]
        row_ok = (k_row[None, :] >= rel_q[:, None]) & (k_row[None, :] < rel_q[:, None] + NA_ROWS)
        valid[c] = row_ok & col_ok
        dr[c] = np.clip(k_row[None, :] - r_us - q_row[:, None] + NA_ROWS - 1, 0, 2 * NA_ROWS - 2)
        dc[c] = np.clip(k_col[None, :] - q_col[:, None] + NA_COLS - 1, 0, 2 * NA_COLS - 2)
    return np.asarray(cfg_ids, np.int32), np.asarray(kstarts, np.int32), valid, dr, dc


def _na_attn_kernel(cfg_ref, ks_ref, q_ref, kp_ref, kc_ref, kn_ref, vp_ref, vc_ref, vn_ref, bias_ref,
                    o_ref, kcat, vcat, *, seg):
    s_id = pl.program_id(1)
    halo = NA_HALO_ROWS * GRID_W
    m, nk = NA_GROUP_ROWS * GRID_W, NA_KEY_ROWS * GRID_W
    n_groups = seg // m
    kcat[0:halo] = kp_ref[...]
    kcat[halo:halo + seg] = kc_ref[...]
    kcat[halo + seg:] = kn_ref[...]
    vcat[0:halo] = vp_ref[...]
    vcat[halo:halo + seg] = vc_ref[...]
    vcat[halo + seg:] = vn_ref[...]
    low_half = lax.broadcasted_iota(jnp.int32, (m, LANES), 1) < HEAD_DIM

    def body(g, carry):
        gg = s_id * n_groups + g
        cfg = cfg_ref[gg]
        ks = pl.multiple_of(ks_ref[gg], GRID_W)
        r0 = pl.multiple_of(g * m, m)
        q = q_ref[pl.ds(r0, m), :]
        zero = jnp.zeros_like(q)
        q2 = jnp.concatenate([jnp.where(low_half, q, zero), jnp.where(low_half, zero, q)], axis=0)
        kw = kcat[pl.ds(ks, nk), :]
        vw = vcat[pl.ds(ks, nk), :]
        s = lax.dot_general(q2, kw, (((1,), (1,)), ((), ())), preferred_element_type=F32)
        s = s + bias_ref[cfg].reshape(2 * m, nk)
        mx = jnp.max(s, axis=-1, keepdims=True)
        e = jnp.exp(s - mx)
        denom = jnp.sum(e, axis=-1, keepdims=True)
        o2 = jnp.dot(e.astype(BF16), vw, preferred_element_type=F32) * (1.0 / denom)
        o_ref[pl.ds(r0, m), :] = jnp.where(low_half, o2[:m], o2[m:]).astype(BF16)
        return carry

    lax.fori_loop(0, n_groups, body, 0)


def _na_attn_call(proj, bias_cfg, cfg_ids, kstarts, seg, base):
    t = proj.shape[0]
    halo = NA_HALO_ROWS * GRID_W
    n_seg = t // seg
    per = seg // halo
    n_halo_blocks = t // halo
    ncfg = bias_cfg.shape[0]
    m, nk = NA_GROUP_ROWS * GRID_W, NA_KEY_ROWS * GRID_W
    kb0, vb0 = base + COL_KB // LANES, base + COL_VB // LANES

    def cur(col0):
        return pl.BlockSpec((seg, LANES), lambda p, s, c, k: (s, col0 + p))

    def prev(col0):
        return pl.BlockSpec((halo, LANES), lambda p, s, c, k: (jnp.maximum(s * per - 1, 0), col0 + p))

    def nxt(col0):
        return pl.BlockSpec((halo, LANES), lambda p, s, c, k: (jnp.minimum((s + 1) * per, n_halo_blocks - 1), col0 + p))

    grid_spec = pltpu.PrefetchScalarGridSpec(
        num_scalar_prefetch=2,
        grid=(N_HEAD_PAIRS, n_seg),
        in_specs=[cur(base + COL_QB // LANES), prev(kb0), cur(kb0), nxt(kb0), prev(vb0), cur(vb0), nxt(vb0),
                  pl.BlockSpec((ncfg, 2, m, nk), lambda p, s, c, k: (0, p, 0, 0))],
        out_specs=pl.BlockSpec((seg, LANES), lambda p, s, c, k: (s, p)),
        scratch_shapes=[pltpu.VMEM((seg + 2 * halo, LANES), BF16), pltpu.VMEM((seg + 2 * halo, LANES), BF16)],
    )
    return pl.pallas_call(
        functools.partial(_na_attn_kernel, seg=seg),
        grid_spec=grid_spec,
        out_shape=jax.ShapeDtypeStruct((t, W_B), BF16),
        compiler_params=_cparams(("arbitrary", "arbitrary")),
        name="na_attn",
    )(cfg_ids, kstarts, proj, proj, proj, proj, proj, proj, proj, bias_cfg)


def _outproj_kernel(aa_ref, ab_ref, ga_ref, gb_ref, xp_ref, xs_ref, mod_ref, woa_ref, wob_ref, wout_ref,
                    g2_ref, wrh_ref, wrl_ref, br_ref, x1_ref, h2_ref, lg_ref, *, n_prompt_tiles):
    i = pl.program_id(0)
    ya = jnp.dot(aa_ref[...], woa_ref[...], preferred_element_type=F32)
    yb = jnp.dot(ab_ref[...], wob_ref[...], preferred_element_type=F32)
    merged = ga_ref[...].astype(F32) * ya + gb_ref[...].astype(F32) * yb
    z = jnp.dot(merged.astype(BF16), wout_ref[...], preferred_element_type=F32)
    x = jnp.where(i < n_prompt_tiles, xp_ref[...], xs_ref[...])
    m = mod_ref[0]
    x1 = x + m[2:3] * z
    x1_ref[...] = x1
    y = x1 * lax.rsqrt(jnp.mean(x1 * x1, axis=-1, keepdims=True) + RMS_EPS) * g2_ref[...]
    h2 = y * (1.0 + m[4:5]) + m[3:4]
    h2_hi = h2.astype(BF16)
    h2_ref[...] = h2_hi
    h2_lo = (h2 - h2_hi.astype(F32)).astype(BF16)
    lg = jnp.dot(h2_hi, wrh_ref[...], preferred_element_type=F32)
    lg = lg + jnp.dot(h2_lo, wrh_ref[...], preferred_element_type=F32)
    lg = lg + jnp.dot(h2_hi, wrl_ref[...], preferred_element_type=F32)
    lg_ref[...] = lg + br_ref[...]


def _outproj_call(attn_a, attn_b, proj, xp, xs, mod_seg, woa, wob, wout, g2, wr_hi, wr_lo, br, seg):
    t = attn_a.shape[0]
    d = xp.shape[1]
    tm = min(256, seg)
    npt, nst = xp.shape[0] // tm, xs.shape[0] // tm
    xp_spec, xs_spec = _two_group_specs(tm, d, npt, nst)
    def const(shape):
        return pl.BlockSpec(shape, lambda i: (0,) * len(shape), pipeline_mode=pl.Buffered(1))

    return pl.pallas_call(
        functools.partial(_outproj_kernel, n_prompt_tiles=npt),
        grid=(npt + nst,),
        in_specs=[pl.BlockSpec((tm, Q_A), lambda i: (i, 0)),
                  pl.BlockSpec((tm, W_B), lambda i: (i, 0)),
                  pl.BlockSpec((tm, d), lambda i: (i, 0)),
                  pl.BlockSpec((tm, d), lambda i: (i, 1)),
                  xp_spec, xs_spec,
                  pl.BlockSpec((1, 6, d), lambda i: (i * tm // seg, 0, 0)),
                  const((Q_A, d)), const((W_B, d)), const((d, d)), const((1, d)),
                  const((d, LANES)), const((d, LANES)), const((1, LANES))],
        out_specs=[pl.BlockSpec((tm, d), lambda i: (i, 0)),
                   pl.BlockSpec((tm, d), lambda i: (i, 0)),
                   pl.BlockSpec((tm, LANES), lambda i: (i, 0))],
        out_shape=[jax.ShapeDtypeStruct((t, d), F32),
                   jax.ShapeDtypeStruct((t, d), BF16),
                   jax.ShapeDtypeStruct((t, LANES), F32)],
        compiler_params=_cparams(("arbitrary",)),
        name="out_proj",
    )(attn_a, attn_b, proj, proj, xp, xs, mod_seg, woa, wob, wout, g2.reshape(1, d), wr_hi, wr_lo, br)


def _moe_kernel(te_ref, nv_ref, x_ref, wg_ref, wl_ref, bg_ref, bl_ref, wd_ref, bd_ref, o_ref, acc_ref, *, nf):
    t = pl.program_id(0)
    f = pl.program_id(1)
    live = t < nv_ref[0]

    @pl.when(live)
    def _():
        x = x_ref[...]
        gate = jnp.dot(x, wg_ref[0], preferred_element_type=F32) + bg_ref[0]
        lin = jnp.dot(x, wl_ref[0], preferred_element_type=F32) + bl_ref[0]
        gate = jnp.minimum(gate, SWIGLU_LIMIT)
        lin = jnp.clip(lin, -SWIGLU_LIMIT, SWIGLU_LIMIT)
        act = gate * _sigmoid(SWIGLU_ALPHA * gate) * (lin + 1.0)
        part = jnp.dot(act.astype(BF16), wd_ref[0], preferred_element_type=F32)

        @pl.when(f == 0)
        def _():
            acc_ref[...] = part + bd_ref[0]

        @pl.when(f > 0)
        def _():
            acc_ref[...] += part

        @pl.when(f == nf - 1)
        def _():
            o_ref[...] = acc_ref[...].astype(BF16)

    @pl.when(jnp.logical_not(live) & (f == 0))
    def _():
        o_ref[...] = jnp.zeros_like(o_ref)


def _moe_call(xin, tile_e, n_valid, w_gu, b_gu, w_down, b_down, tm):
    n_rows, d = xin.shape
    n_e, _, two_ff = w_gu.shape
    d_ff = two_ff // 2
    tf = min(512, d_ff)
    nf = d_ff // tf
    n_tiles = n_rows // tm

    def fidx(t, f, nv):
        return jnp.where(t < nv[0], f, nf - 1)

    grid_spec = pltpu.PrefetchScalarGridSpec(
        num_scalar_prefetch=2,
        grid=(n_tiles, nf),
        in_specs=[pl.BlockSpec((tm, d), lambda t, f, te, nv: (t, 0)),
                  pl.BlockSpec((1, d, tf), lambda t, f, te, nv: (te[t], 0, fidx(t, f, nv))),
                  pl.BlockSpec((1, d, tf), lambda t, f, te, nv: (te[t], 0, fidx(t, f, nv) + nf)),
                  pl.BlockSpec((1, 1, tf), lambda t, f, te, nv: (te[t], 0, fidx(t, f, nv))),
                  pl.BlockSpec((1, 1, tf), lambda t, f, te, nv: (te[t], 0, fidx(t, f, nv) + nf)),
                  pl.BlockSpec((1, tf, d), lambda t, f, te, nv: (te[t], fidx(t, f, nv), 0)),
                  pl.BlockSpec((1, 1, d), lambda t, f, te, nv: (te[t], 0, 0))],
        out_specs=pl.BlockSpec((tm, d), lambda t, f, te, nv: (t, 0)),
        scratch_shapes=[pltpu.VMEM((tm, d), F32)],
    )
    return pl.pallas_call(
        functools.partial(_moe_kernel, nf=nf),
        grid_spec=grid_spec,
        out_shape=jax.ShapeDtypeStruct((n_rows, d), BF16),
        compiler_params=_cparams(("arbitrary", "arbitrary")),
        name="moe_experts",
    )(tile_e, n_valid, xin, w_gu, w_gu, b_gu.reshape(n_e, 1, two_ff), b_gu.reshape(n_e, 1, two_ff),
      w_down, b_down.reshape(n_e, 1, d))


def _combine_kernel(x1_ref, ys_ref, w_ref, mod_ref, o_ref):
    w = w_ref[...]
    acc = w[:, 0:1] * ys_ref[0].astype(F32)
    for k in range(1, TOP_K):
        acc = acc + w[:, k:k + 1] * ys_ref[k].astype(F32)
    o_ref[...] = x1_ref[...] + mod_ref[0][5:6] * acc


def _combine_call(x1, ysel, top_w, mod_seg, row0, n_rows, seg):
    d = x1.shape[1]
    tm = min(512, seg)
    off = row0 // tm
    return pl.pallas_call(
        _combine_kernel,
        grid=(n_rows // tm,),
        in_specs=[pl.BlockSpec((tm, d), lambda i: (i + off, 0)),
                  pl.BlockSpec((TOP_K, tm, d), lambda i: (0, i + off, 0)),
                  pl.BlockSpec((tm, TOP_K), lambda i: (i + off, 0)),
                  pl.BlockSpec((1, 6, d), lambda i: ((i + off) * tm // seg, 0, 0))],
        out_specs=pl.BlockSpec((tm, d), lambda i: (i, 0)),
        out_shape=jax.ShapeDtypeStruct((n_rows, d), F32),
        compiler_params=_cparams(("arbitrary",)),
        name="moe_combine",
    )(x1, ysel, top_w, mod_seg)


def _pack_w_in(w_in, g_q_a, g_k_a, g_q_b, g_k_b):
    o1 = Q_A
    o2 = o1 + KV_A
    o3 = o2 + KV_A
    o4 = o3 + W_B
    o5 = o4 + W_B
    o6 = o5 + W_B
    d = w_in.shape[0]
    wqa, wka, wva = w_in[:, :o1], w_in[:, o1:o2], w_in[:, o2:o3]
    wqb, wkb, wvb, wg = w_in[:, o3:o4], w_in[:, o4:o5], w_in[:, o5:o6], w_in[:, o6:]

    def dup(w):
        w4 = w.reshape(d, N_KV_A, 1, HEAD_DIM)
        return jnp.broadcast_to(w4, (d, N_KV_A, 2, HEAD_DIM)).reshape(d, N_KV_A * LANES)

    w = jnp.concatenate([wg, wqa, wqb, wkb, dup(wka), dup(wva), wvb], axis=1).astype(BF16)
    q_scale = HEAD_DIM ** -0.5
    gvec = jnp.concatenate([jnp.ones((2 * d,), F32),
                            jnp.tile(g_q_a * q_scale, N_HEADS_A), jnp.tile(g_q_b * q_scale, N_HEADS_B),
                            jnp.tile(g_k_b, N_HEADS_B), jnp.tile(g_k_a, 2 * N_KV_A),
                            jnp.ones((COL_END - COL_VA,), F32)]).reshape(1, -1).astype(F32)
    return w, gvec


def kernel(x_prompt, x_sample, c_prompt, c_sample, w_ada, b_ada, g_norm1, w_in, g_q_a, g_k_a, g_q_b, g_k_b, sink_a, t5_bias, na_bias, w_o_a, w_o_b, w_out, g_norm2, w_router, b_router, w_gu, b_gu, w_down, b_down):
    bp, sp, d = x_prompt.shape
    bs, ss, _ = x_sample.shape
    n_e = w_router.shape[-1]
    tp, ts = bp * sp, bs * ss
    t = tp + ts
    seg = math.gcd(math.gcd(sp, ss), 2048)
    assert w_ada.shape[0] == 1 and sp % seg == 0 and ss % seg == 0 and seg % (NA_HALO_ROWS * GRID_W) == 0
    assert 2 * d == w_in.shape[-1] - (Q_A + 2 * KV_A + 3 * W_B)
    base = 2 * d // LANES
    xp = x_prompt.reshape(tp, d)
    xs = x_sample.reshape(ts, d)

    n_c = bp + bs
    c_pad = jnp.zeros((-(-n_c // 16) * 16, d), F32).at[:n_c].set(jnp.concatenate([c_prompt, c_sample], axis=0))
    mod = _ada_call(c_pad, w_ada[0], b_ada[0])[:n_c].reshape(n_c, 6, d)
    seg_owner = np.concatenate([np.repeat(np.arange(bp), sp // seg), bp + np.repeat(np.arange(bs), ss // seg)])
    mod_seg = mod[seg_owner]

    h1 = _norm_mod_call(xp, xs, mod_seg, g_norm1[0], seg)

    w_slab, gvec = _pack_w_in(w_in[0], g_q_a[0], g_k_a[0], g_q_b[0], g_k_b[0])
    grp = np.arange(MXU_DIM) // HEAD_DIM
    ones_bd = jnp.asarray(grp[:, None] == grp[None, :], BF16)
    proj = _inproj_call(h1, w_slab, gvec, ones_bd, tm=min(1024, seg))

    qs = min(1024, seg)
    sb_tok = np.arange(t // qs) * qs
    seq_start = np.where(sb_tok < tp, sb_tok // sp * sp, tp + (sb_tok - tp) // ss * ss)
    seq_len = np.where(sb_tok < tp, sp, ss)
    flags = np.stack([sb_tok != seq_start, sb_tok + qs != seq_start + seq_len], axis=1).astype(np.int32).reshape(-1)
    attn_a = _win_attn_call(proj, _window_bias(t5_bias), sink_a[0].astype(F32), jnp.asarray(flags), qs, base)

    cfg_ids, kstarts, valid, dr, dc = _na_plan([sp] * bp + [ss] * bs, seg)
    nb_tab = na_bias[0].astype(F32)[:, dr, dc]
    bias_cfg = jnp.where(valid[None], nb_tab, MASKED).transpose(1, 0, 2, 3)
    attn_b = _na_attn_call(proj, bias_cfg, jnp.asarray(cfg_ids), jnp.asarray(kstarts), seg, base)

    wr = jnp.zeros((d, LANES), F32).at[:, :n_e].set(w_router[0])
    wr_hi = wr.astype(BF16)
    wr_lo = (wr - wr_hi.astype(F32)).astype(BF16)
    br = jnp.zeros((1, LANES), F32).at[0, :n_e].set(b_router[0])
    x1, h2, logits = _outproj_call(attn_a, attn_b, proj, xp, xs, mod_seg, w_o_a[0].astype(BF16),
                                   w_o_b[0].astype(BF16), w_out[0].astype(BF16), g_norm2[0], wr_hi, wr_lo, br, seg)

    tm_e = min(1024, seg)
    top_logits, top_idx = lax.top_k(logits[:, :n_e], TOP_K)
    top_w = jax.nn.softmax(top_logits, axis=-1)
    n_assign = t * TOP_K
    e_flat = top_idx.reshape(-1)
    onehot = (e_flat[:, None] == jnp.arange(n_e, dtype=e_flat.dtype)[None, :]).astype(jnp.int32)
    csum = jnp.cumsum(onehot, axis=0)
    rank = jnp.take_along_axis(csum, e_flat[:, None], axis=1)[:, 0] - 1
    counts = csum[-1]
    padded = (counts + tm_e - 1) // tm_e * tm_e
    pend = jnp.cumsum(padded)
    dest = (pend - padded)[e_flat] + rank
    n_tiles = -(-n_assign // tm_e) + n_e
    n_rows = n_tiles * tm_e
    tile_e = jnp.minimum(jnp.searchsorted(pend, jnp.arange(n_tiles) * tm_e, side='right'), n_e - 1).astype(jnp.int32)
    n_valid = (pend[-1:] // tm_e).astype(jnp.int32)
    tok_flat = jnp.arange(n_assign, dtype=jnp.int32) // TOP_K
    buf_tok = jnp.full((n_rows,), t, jnp.int32).at[dest].set(tok_flat)
    xin = jnp.concatenate([h2, jnp.zeros((1, d), BF16)], axis=0)[buf_tok]

    yb = _moe_call(xin, tile_e, n_valid, w_gu[0].astype(BF16), b_gu[0], w_down[0].astype(BF16), b_down[0], tm_e)

    ysel = yb[dest.reshape(t, TOP_K).T]
    y_prompt = _combine_call(x1, ysel, top_w, mod_seg, 0, tp, seg)
    y_sample = _combine_call(x1, ysel, top_w, mod_seg, tp, ts, seg)
    return (y_prompt.reshape(bp, sp, d), y_sample.reshape(bs, ss, d))
```

```python
import functools
import math

import numpy as np
import jax
import jax.numpy as jnp
from jax import lax
from jax.experimental import pallas as pl
from jax.experimental.pallas import tpu as pltpu

F32 = jnp.float32
BF16 = jnp.bfloat16

HEAD_DIM = 64
N_HEADS_A = 16
N_KV_A = 4
GROUP_A = N_HEADS_A // N_KV_A
N_HEADS_B = 16
WINDOW = 128
ATTN_BLOCK = 128
T5_BUCKETS = 32
T5_MAX_DIST = 128
GRID_W = 64
NA_ROWS = 8
NA_COLS = 16
TOP_K = 4
SWIGLU_LIMIT = 7.0
SWIGLU_ALPHA = 1.702
RMS_EPS = 1e-6

Q_A = N_HEADS_A * HEAD_DIM
KV_A = N_KV_A * HEAD_DIM
W_B = N_HEADS_B * HEAD_DIM

LANES = 128
MXU_DIM = 256
VMEM_LIMIT_BYTES = 56 * 1024 * 1024
MASKED = -1e30

N_HEAD_PAIRS = Q_A // LANES
COL_QA = 0
COL_QB = COL_QA + Q_A
COL_KB = COL_QB + W_B
COL_KA = COL_KB + W_B
COL_VA = COL_KA + N_KV_A * LANES
COL_VB = COL_VA + N_KV_A * LANES
COL_END = COL_VB + W_B
NA_GROUP_ROWS = 2
NA_KEY_ROWS = NA_GROUP_ROWS + NA_ROWS - 1
NA_HALO_ROWS = NA_ROWS


def _sigmoid(x):
    return 1.0 / (1.0 + jnp.exp(-x))


def _cparams(sem):
    return pltpu.CompilerParams(dimension_semantics=sem, vmem_limit_bytes=VMEM_LIMIT_BYTES)


def _ada_kernel(c_ref, w_ref, b_ref, o_ref):
    c = c_ref[...]
    a = (c * _sigmoid(c)).astype(BF16)
    o_ref[...] = jnp.dot(a, w_ref[...].astype(BF16), preferred_element_type=F32) + b_ref[...]


def _ada_call(c_pad, w_ada, b_ada):
    rows, d = c_pad.shape
    n = w_ada.shape[1]
    tn = min(n, 1024)
    return pl.pallas_call(
        _ada_kernel,
        grid=(n // tn,),
        in_specs=[pl.BlockSpec((rows, d), lambda j: (0, 0)),
                  pl.BlockSpec((d, tn), lambda j: (0, j)),
                  pl.BlockSpec((1, tn), lambda j: (0, j))],
        out_specs=pl.BlockSpec((rows, tn), lambda j: (0, j)),
        out_shape=jax.ShapeDtypeStruct((rows, n), F32),
        compiler_params=_cparams(("arbitrary",)),
        name="ada_ln",
    )(c_pad, w_ada, b_ada.reshape(1, n))


def _two_group_specs(tm, d, n_prompt_tiles, n_sample_tiles):
    xp = pl.BlockSpec((tm, d), lambda i, *_: (jnp.minimum(i, n_prompt_tiles - 1), 0))
    xs = pl.BlockSpec((tm, d), lambda i, *_: (jnp.clip(i - n_prompt_tiles, 0, n_sample_tiles - 1), 0))
    return xp, xs


def _norm_mod_kernel(xp_ref, xs_ref, mod_ref, g_ref, o_ref, *, n_prompt_tiles):
    i = pl.program_id(0)
    x = jnp.where(i < n_prompt_tiles, xp_ref[...], xs_ref[...])
    y = x * lax.rsqrt(jnp.mean(x * x, axis=-1, keepdims=True) + RMS_EPS) * g_ref[...]
    m = mod_ref[0]
    o_ref[...] = (y * (1.0 + m[1:2]) + m[0:1]).astype(BF16)


def _norm_mod_call(xp, xs, mod_seg, g, seg):
    d = xp.shape[1]
    tm = min(512, seg)
    npt, nst = xp.shape[0] // tm, xs.shape[0] // tm
    xp_spec, xs_spec = _two_group_specs(tm, d, npt, nst)
    return pl.pallas_call(
        functools.partial(_norm_mod_kernel, n_prompt_tiles=npt),
        grid=(npt + nst,),
        in_specs=[xp_spec, xs_spec,
                  pl.BlockSpec((1, 6, d), lambda i: (i * tm // seg, 0, 0)),
                  pl.BlockSpec((1, d), lambda i: (0, 0))],
        out_specs=pl.BlockSpec((tm, d), lambda i: (i, 0)),
        out_shape=jax.ShapeDtypeStruct((xp.shape[0] + xs.shape[0], d), BF16),
        compiler_params=_cparams(("arbitrary",)),
        name="norm1_mod",
    )(xp, xs, mod_seg, g.reshape(1, d))


def _group_rms(y, g, ones_ref):
    sq = (y * y).astype(BF16)
    parts = []
    for c in range(y.shape[1] // MXU_DIM):
        parts.append(jnp.dot(sq[:, c * MXU_DIM:(c + 1) * MXU_DIM], ones_ref[...], preferred_element_type=F32))
    ss = jnp.concatenate(parts, axis=1) if len(parts) > 1 else parts[0]
    return y * lax.rsqrt(ss * (1.0 / HEAD_DIM) + RMS_EPS) * g


def _inproj_kernel(h_ref, w_ref, g_ref, ones_ref, o_ref, *, tn, gate_tiles):
    j = pl.program_id(1)
    y = jnp.dot(h_ref[...], w_ref[...], preferred_element_type=F32)
    half = tn // 2
    mixed_tile = gate_tiles + COL_KA // tn
    plain_tile = gate_tiles + COL_VB // tn

    @pl.when(j < gate_tiles)
    def _():
        o_ref[...] = _sigmoid(y).astype(BF16)

    @pl.when((j >= gate_tiles) & (j < mixed_tile))
    def _():
        o_ref[...] = _group_rms(y, g_ref[...], ones_ref).astype(BF16)

    @pl.when(j == mixed_tile)
    def _():
        o_ref[:, :half] = _group_rms(y[:, :half], g_ref[:, :half], ones_ref).astype(BF16)
        o_ref[:, half:] = y[:, half:].astype(BF16)

    @pl.when(j == plain_tile)
    def _():
        o_ref[...] = y.astype(BF16)


def _inproj_call(h, w, gvec, ones_bd, tm):
    t, d = h.shape
    pw = w.shape[1]
    tn = 1024
    assert (2 * d) % tn == 0 and COL_KA % tn == 0 and COL_VA - COL_KA == tn // 2 and COL_VB % tn == 0
    return pl.pallas_call(
        functools.partial(_inproj_kernel, tn=tn, gate_tiles=2 * d // tn),
        grid=(t // tm, pw // tn),
        in_specs=[pl.BlockSpec((tm, d), lambda i, j: (i, 0)),
                  pl.BlockSpec((d, tn), lambda i, j: (0, j)),
                  pl.BlockSpec((1, tn), lambda i, j: (0, j)),
                  pl.BlockSpec((MXU_DIM, MXU_DIM), lambda i, j: (0, 0))],
        out_specs=pl.BlockSpec((tm, tn), lambda i, j: (i, j)),
        out_shape=jax.ShapeDtypeStruct((t, pw), BF16),
        compiler_params=_cparams(("arbitrary", "arbitrary")),
        name="in_proj",
    )(h, w, gvec, ones_bd)


def _win_attn_kernel(flags_ref, q_ref, kp_ref, kc_ref, kn_ref, vp_ref, vc_ref, vn_ref, bias_ref, sink_ref,
                     o_ref, kcat, vcat, *, qs, nb):
    p = pl.program_id(0)
    sb = pl.program_id(1)
    blk = ATTN_BLOCK
    kcat[0:blk] = kp_ref[...]
    kcat[blk:blk + qs] = kc_ref[...]
    kcat[blk + qs:] = kn_ref[...]
    vcat[0:blk] = vp_ref[...]
    vcat[blk:blk + qs] = vc_ref[...]
    vcat[blk + qs:] = vn_ref[...]
    prev_ok = flags_ref[2 * sb]
    next_ok = flags_ref[2 * sb + 1]
    low_half = lax.broadcasted_iota(jnp.int32, (blk, LANES), 1) < HEAD_DIM
    row = lax.broadcasted_iota(jnp.int32, (2 * blk, 1), 0)
    sinkcol = jnp.where(row < blk, sink_ref[2 * p], sink_ref[2 * p + 1])
    kl = blk + 2 * WINDOW

    def body(b, carry):
        r0 = pl.multiple_of(b * blk, blk)
        q = q_ref[pl.ds(r0, blk), :]
        zero = jnp.zeros_like(q)
        q2 = jnp.concatenate([jnp.where(low_half, q, zero), jnp.where(low_half, zero, q)], axis=0)
        kw = kcat[pl.ds(r0, kl), :]
        vw = vcat[pl.ds(r0, kl), :]
        s = lax.dot_general(q2, kw, (((1,), (1,)), ((), ())), preferred_element_type=F32)
        variant = jnp.where((b == 0) & (prev_ok == 0), 1, jnp.where((b == nb - 1) & (next_ok == 0), 2, 0))
        s = s + bias_ref[variant].reshape(2 * blk, kl)
        m = jnp.maximum(jnp.max(s, axis=-1, keepdims=True), sinkcol)
        e = jnp.exp(s - m)
        denom = jnp.sum(e, axis=-1, keepdims=True) + jnp.exp(sinkcol - m)
        o2 = jnp.dot(e.astype(BF16), vw, preferred_element_type=F32) * (1.0 / denom)
        o_ref[pl.ds(r0, blk), :] = jnp.where(low_half, o2[:blk], o2[blk:]).astype(BF16)
        return carry

    lax.fori_loop(0, nb, body, 0)


def _win_attn_call(proj, bias3, sink, flags, qs, base):
    t = proj.shape[0]
    blk = ATTN_BLOCK
    nb = qs // blk
    n_super = t // qs
    n_blk_rows = t // blk
    kl = blk + 2 * WINDOW
    ka0, va0 = base + COL_KA // LANES, base + COL_VA // LANES

    def cur(col0):
        return pl.BlockSpec((qs, LANES), lambda p, s, f: (s, col0 + p // 2))

    def prev(col0):
        return pl.BlockSpec((blk, LANES), lambda p, s, f: (jnp.maximum(s * nb - 1, 0), col0 + p // 2))

    def nxt(col0):
        return pl.BlockSpec((blk, LANES), lambda p, s, f: (jnp.minimum((s + 1) * nb, n_blk_rows - 1), col0 + p // 2))

    grid_spec = pltpu.PrefetchScalarGridSpec(
        num_scalar_prefetch=1,
        grid=(N_HEAD_PAIRS, n_super),
        in_specs=[pl.BlockSpec((qs, LANES), lambda p, s, f: (s, base + COL_QA // LANES + p)),
                  prev(ka0), cur(ka0), nxt(ka0), prev(va0), cur(va0), nxt(va0),
                  pl.BlockSpec((3, 2, blk, kl), lambda p, s, f: (0, p, 0, 0)),
                  pl.BlockSpec(memory_space=pltpu.SMEM)],
        out_specs=pl.BlockSpec((qs, LANES), lambda p, s, f: (s, p)),
        scratch_shapes=[pltpu.VMEM((qs + 2 * blk, LANES), BF16), pltpu.VMEM((qs + 2 * blk, LANES), BF16)],
    )
    return pl.pallas_call(
        functools.partial(_win_attn_kernel, qs=qs, nb=nb),
        grid_spec=grid_spec,
        out_shape=jax.ShapeDtypeStruct((t, Q_A), BF16),
        compiler_params=_cparams(("arbitrary", "arbitrary")),
        name="win_attn",
    )(flags, proj, proj, proj, proj, proj, proj, proj, bias3, sink)


def _t5_bucket(rel):
    nb = T5_BUCKETS // 2
    max_exact = nb // 2
    ret = jnp.where(rel > 0, nb, 0)
    n = jnp.abs(rel)
    nf = jnp.maximum(n, 1).astype(jnp.float32)
    large = max_exact + (jnp.log(nf / max_exact) / math.log(T5_MAX_DIST / max_exact) * (nb - max_exact)).astype(jnp.int32)
    large = jnp.minimum(large, nb - 1)
    return ret + jnp.where(n < max_exact, n, large)


def _toeplitz(v, n_rows, n_cols):
    p = v.shape[-1]
    assert n_cols <= p - 1
    flat = jnp.tile(v, (1,) * (v.ndim - 1) + (n_rows,))[..., :n_rows * (p - 1)]
    return flat.reshape(v.shape[:-1] + (n_rows, p - 1))[..., :n_cols]


def _window_bias(t5_bias):
    blk, kl = ATTN_BLOCK, ATTN_BLOCK + 2 * WINDOW
    period = blk + kl - 1
    m = np.arange(period)
    rel_of_m = np.where(m < kl, m, m - period) - WINDOW
    per_rel = t5_bias[_t5_bucket(jnp.asarray(rel_of_m))].astype(F32).T
    bias = _toeplitz(per_rel, blk, kl)
    col = np.arange(kl)[None, :]
    band = np.abs(col - WINDOW - np.arange(blk)[:, None]) <= WINDOW
    keep = np.stack([band, band & (col >= WINDOW), band & (col < WINDOW + blk)])
    return jnp.where(keep[:, None], bias[None], MASKED)


def _na_plan(seq_lens, seg):
    qr, nkr = NA_GROUP_ROWS, NA_KEY_ROWS
    configs, cfg_ids, kstarts = {}, [], []
    tok = 0
    for s_len in seq_lens:
        rows = s_len // GRID_W
        assert rows >= nkr and rows % qr == 0
        for r in range(0, rows, qr):
            us = min(max(r - NA_ROWS // 2, 0), rows - nkr)
            rel = tuple(min(max(r + q - NA_ROWS // 2, 0), rows - NA_ROWS) - us for q in range(qr))
            key = (r - us, rel)
            cfg_ids.append(configs.setdefault(key, len(configs)))
            g_tok = tok + r * GRID_W
            seg_tok0 = (g_tok // seg) * seg
            kstarts.append(tok + us * GRID_W - seg_tok0 + NA_HALO_ROWS * GRID_W)
        tok += s_len
    blocks = [None] * len(configs)
    for (r_us, rel), c in configs.items():
        blocks[c] = [[kr - r_us - q + NA_ROWS - 1 if rel[q] <= kr < rel[q] + NA_ROWS else None
                      for kr in range(nkr)] for q in range(qr)]
    return np.asarray(cfg_ids, np.int32), np.asarray(kstarts, np.int32), blocks


def _na_bias_tables(na_bias, blocks):
    period = 2 * GRID_W - 1
    m = np.arange(period)
    dcol = np.where(m < GRID_W, m, m - period)
    per_dcol = na_bias.astype(F32)[:, :, np.clip(dcol + NA_COLS - 1, 0, 2 * NA_COLS - 2)]
    col_tab = _toeplitz(per_dcol, GRID_W, GRID_W)
    q_col, k_col = np.arange(GRID_W)[:, None], np.arange(GRID_W)[None, :]
    win_start = np.clip(q_col - NA_COLS // 2, 0, GRID_W - NA_COLS)
    col_ok = (k_col >= win_start) & (k_col < win_start + NA_COLS)
    col_tab = jnp.where(col_ok, col_tab, MASKED)
    masked_blk = jnp.full(col_tab.shape[:1] + col_tab.shape[2:], MASKED, F32)
    cfgs = []
    for cfg in blocks:
        rows = [jnp.concatenate([masked_blk if dr is None else col_tab[:, dr] for dr in qrow], axis=-1) for qrow in cfg]
        cfgs.append(jnp.concatenate(rows, axis=-2))
    return jnp.stack(cfgs)


def _na_attn_kernel(cfg_ref, ks_ref, q_ref, kp_ref, kc_ref, kn_ref, vp_ref, vc_ref, vn_ref, bias_ref,
                    o_ref, kcat, vcat, *, seg):
    s_id = pl.program_id(1)
    halo = NA_HALO_ROWS * GRID_W
    m, nk = NA_GROUP_ROWS * GRID_W, NA_KEY_ROWS * GRID_W
    n_groups = seg // m
    kcat[0:halo] = kp_ref[...]
    kcat[halo:halo + seg] = kc_ref[...]
    kcat[halo + seg:] = kn_ref[...]
    vcat[0:halo] = vp_ref[...]
    vcat[halo:halo + seg] = vc_ref[...]
    vcat[halo + seg:] = vn_ref[...]
    low_half = lax.broadcasted_iota(jnp.int32, (m, LANES), 1) < HEAD_DIM

    def body(g, carry):
        gg = s_id * n_groups + g
        cfg = cfg_ref[gg]
        ks = pl.multiple_of(ks_ref[gg], GRID_W)
        r0 = pl.multiple_of(g * m, m)
        q = q_ref[pl.ds(r0, m), :]
        zero = jnp.zeros_like(q)
        q2 = jnp.concatenate([jnp.where(low_half, q, zero), jnp.where(low_half, zero, q)], axis=0)
        kw = kcat[pl.ds(ks, nk), :]
        vw = vcat[pl.ds(ks, nk), :]
        s = lax.dot_general(q2, kw, (((1,), (1,)), ((), ())), preferred_element_type=F32)
        s = s + bias_ref[cfg].reshape(2 * m, nk)
        mx = jnp.max(s, axis=-1, keepdims=True)
        e = jnp.exp(s - mx)
        denom = jnp.sum(e, axis=-1, keepdims=True)
        o2 = jnp.dot(e.astype(BF16), vw, preferred_element_type=F32) * (1.0 / denom)
        o_ref[pl.ds(r0, m), :] = jnp.where(low_half, o2[:m], o2[m:]).astype(BF16)
        return carry

    lax.fori_loop(0, n_groups, body, 0)


def _na_attn_call(proj, bias_cfg, cfg_ids, kstarts, seg, base):
    t = proj.shape[0]
    halo = NA_HALO_ROWS * GRID_W
    n_seg = t // seg
    per = seg // halo
    n_halo_blocks = t // halo
    ncfg = bias_cfg.shape[0]
    m, nk = NA_GROUP_ROWS * GRID_W, NA_KEY_ROWS * GRID_W
    kb0, vb0 = base + COL_KB // LANES, base + COL_VB // LANES

    def cur(col0):
        return pl.BlockSpec((seg, LANES), lambda p, s, c, k: (s, col0 + p))

    def prev(col0):
        return pl.BlockSpec((halo, LANES), lambda p, s, c, k: (jnp.maximum(s * per - 1, 0), col0 + p))

    def nxt(col0):
        return pl.BlockSpec((halo, LANES), lambda p, s, c, k: (jnp.minimum((s + 1) * per, n_halo_blocks - 1), col0 + p))

    grid_spec = pltpu.PrefetchScalarGridSpec(
        num_scalar_prefetch=2,
        grid=(N_HEAD_PAIRS, n_seg),
        in_specs=[cur(base + COL_QB // LANES), prev(kb0), cur(kb0), nxt(kb0), prev(vb0), cur(vb0), nxt(vb0),
                  pl.BlockSpec((ncfg, 2, m, nk), lambda p, s, c, k: (0, p, 0, 0))],
        out_specs=pl.BlockSpec((seg, LANES), lambda p, s, c, k: (s, p)),
        scratch_shapes=[pltpu.VMEM((seg + 2 * halo, LANES), BF16), pltpu.VMEM((seg + 2 * halo, LANES), BF16)],
    )
    return pl.pallas_call(
        functools.partial(_na_attn_kernel, seg=seg),
        grid_spec=grid_spec,
        out_shape=jax.ShapeDtypeStruct((t, W_B), BF16),
        compiler_params=_cparams(("arbitrary", "arbitrary")),
        name="na_attn",
    )(cfg_ids, kstarts, proj, proj, proj, proj, proj, proj, proj, bias_cfg)


def _outproj_kernel(aa_ref, ab_ref, ga_ref, gb_ref, xp_ref, xs_ref, mod_ref, woa_ref, wob_ref, wout_ref,
                    g2_ref, wrh_ref, wrl_ref, br_ref, x1_ref, h2_ref, lg_ref, *, n_prompt_tiles):
    i = pl.program_id(0)
    ya = jnp.dot(aa_ref[...], woa_ref[...], preferred_element_type=F32)
    yb = jnp.dot(ab_ref[...], wob_ref[...], preferred_element_type=F32)
    merged = ga_ref[...].astype(F32) * ya + gb_ref[...].astype(F32) * yb
    z = jnp.dot(merged.astype(BF16), wout_ref[...], preferred_element_type=F32)
    x = jnp.where(i < n_prompt_tiles, xp_ref[...], xs_ref[...])
    m = mod_ref[0]
    x1 = x + m[2:3] * z
    x1_ref[...] = x1
    y = x1 * lax.rsqrt(jnp.mean(x1 * x1, axis=-1, keepdims=True) + RMS_EPS) * g2_ref[...]
    h2 = y * (1.0 + m[4:5]) + m[3:4]
    h2_hi = h2.astype(BF16)
    h2_ref[...] = h2_hi
    h2_lo = (h2 - h2_hi.astype(F32)).astype(BF16)
    lg = jnp.dot(h2_hi, wrh_ref[...], preferred_element_type=F32)
    lg = lg + jnp.dot(h2_lo, wrh_ref[...], preferred_element_type=F32)
    lg = lg + jnp.dot(h2_hi, wrl_ref[...], preferred_element_type=F32)
    lg_ref[...] = lg + br_ref[...]


def _outproj_call(attn_a, attn_b, proj, xp, xs, mod_seg, woa, wob, wout, g2, wr_hi, wr_lo, br, seg):
    t = attn_a.shape[0]
    d = xp.shape[1]
    tm = min(256, seg)
    npt, nst = xp.shape[0] // tm, xs.shape[0] // tm
    xp_spec, xs_spec = _two_group_specs(tm, d, npt, nst)
    def const(shape):
        return pl.BlockSpec(shape, lambda i: (0,) * len(shape), pipeline_mode=pl.Buffered(1))

    return pl.pallas_call(
        functools.partial(_outproj_kernel, n_prompt_tiles=npt),
        grid=(npt + nst,),
        in_specs=[pl.BlockSpec((tm, Q_A), lambda i: (i, 0)),
                  pl.BlockSpec((tm, W_B), lambda i: (i, 0)),
                  pl.BlockSpec((tm, d), lambda i: (i, 0)),
                  pl.BlockSpec((tm, d), lambda i: (i, 1)),
                  xp_spec, xs_spec,
                  pl.BlockSpec((1, 6, d), lambda i: (i * tm // seg, 0, 0)),
                  const((Q_A, d)), const((W_B, d)), const((d, d)), const((1, d)),
                  const((d, LANES)), const((d, LANES)), const((1, LANES))],
        out_specs=[pl.BlockSpec((tm, d), lambda i: (i, 0)),
                   pl.BlockSpec((tm, d), lambda i: (i, 0)),
                   pl.BlockSpec((tm, LANES), lambda i: (i, 0))],
        out_shape=[jax.ShapeDtypeStruct((t, d), F32),
                   jax.ShapeDtypeStruct((t, d), BF16),
                   jax.ShapeDtypeStruct((t, LANES), F32)],
        compiler_params=_cparams(("arbitrary",)),
        name="out_proj",
    )(attn_a, attn_b, proj, proj, xp, xs, mod_seg, woa, wob, wout, g2.reshape(1, d), wr_hi, wr_lo, br)


def _cast_kernel(w_ref, o_ref):
    o_ref[...] = w_ref[...].astype(BF16)


def _cast_call(w):
    n_e, r, c = w.shape
    tc = min(c, 2048)
    return pl.pallas_call(
        _cast_kernel,
        grid=(n_e, c // tc),
        in_specs=[pl.BlockSpec((1, r, tc), lambda e, j: (e, 0, j))],
        out_specs=pl.BlockSpec((1, r, tc), lambda e, j: (e, 0, j)),
        out_shape=jax.ShapeDtypeStruct(w.shape, BF16),
        compiler_params=_cparams(("arbitrary", "arbitrary")),
        name="cast_bf16",
    )(w)


def _moe_kernel(te_ref, nv_ref, x_ref, wg_ref, wl_ref, bg_ref, bl_ref, wd_ref, bd_ref, o_ref, acc_ref, *, nf):
    t = pl.program_id(0)
    f = pl.program_id(1)
    live = t < nv_ref[0]

    @pl.when(live)
    def _():
        x = x_ref[...]
        gate = jnp.dot(x, wg_ref[0], preferred_element_type=F32) + bg_ref[0]
        lin = jnp.dot(x, wl_ref[0], preferred_element_type=F32) + bl_ref[0]
        gate = jnp.minimum(gate, SWIGLU_LIMIT)
        lin = jnp.clip(lin, -SWIGLU_LIMIT, SWIGLU_LIMIT)
        act = gate * _sigmoid(SWIGLU_ALPHA * gate) * (lin + 1.0)
        part = jnp.dot(act.astype(BF16), wd_ref[0], preferred_element_type=F32)

        @pl.when(f == 0)
        def _():
            acc_ref[...] = part + bd_ref[0]

        @pl.when(f > 0)
        def _():
            acc_ref[...] += part

        @pl.when(f == nf - 1)
        def _():
            o_ref[...] = acc_ref[...].astype(BF16)

    @pl.when(jnp.logical_not(live) & (f == 0))
    def _():
        o_ref[...] = jnp.zeros_like(o_ref)


def _moe_call(xin, tile_e, n_valid, w_gu, b_gu, w_down, b_down, tm):
    n_rows, d = xin.shape
    n_e, _, two_ff = w_gu.shape
    d_ff = two_ff // 2
    tf = min(512, d_ff)
    nf = d_ff // tf
    n_tiles = n_rows // tm

    def fidx(t, f, nv):
        return jnp.where(t < nv[0], f, nf - 1)

    grid_spec = pltpu.PrefetchScalarGridSpec(
        num_scalar_prefetch=2,
        grid=(n_tiles, nf),
        in_specs=[pl.BlockSpec((tm, d), lambda t, f, te, nv: (t, 0)),
                  pl.BlockSpec((1, d, tf), lambda t, f, te, nv: (te[t], 0, fidx(t, f, nv))),
                  pl.BlockSpec((1, d, tf), lambda t, f, te, nv: (te[t], 0, fidx(t, f, nv) + nf)),
                  pl.BlockSpec((1, 1, tf), lambda t, f, te, nv: (te[t], 0, fidx(t, f, nv))),
                  pl.BlockSpec((1, 1, tf), lambda t, f, te, nv: (te[t], 0, fidx(t, f, nv) + nf)),
                  pl.BlockSpec((1, tf, d), lambda t, f, te, nv: (te[t], fidx(t, f, nv), 0)),
                  pl.BlockSpec((1, 1, d), lambda t, f, te, nv: (te[t], 0, 0))],
        out_specs=pl.BlockSpec((tm, d), lambda t, f, te, nv: (t, 0)),
        scratch_shapes=[pltpu.VMEM((tm, d), F32)],
    )
    return pl.pallas_call(
        functools.partial(_moe_kernel, nf=nf),
        grid_spec=grid_spec,
        out_shape=jax.ShapeDtypeStruct((n_rows, d), BF16),
        compiler_params=_cparams(("arbitrary", "arbitrary")),
        name="moe_experts",
    )(tile_e, n_valid, xin, w_gu, w_gu, b_gu.reshape(n_e, 1, two_ff), b_gu.reshape(n_e, 1, two_ff),
      w_down, b_down.reshape(n_e, 1, d))


def _combine_kernel(x1_ref, ys_ref, w_ref, mod_ref, o_ref):
    w = w_ref[...]
    acc = w[:, 0:1] * ys_ref[0].astype(F32)
    for k in range(1, TOP_K):
        acc = acc + w[:, k:k + 1] * ys_ref[k].astype(F32)
    o_ref[...] = x1_ref[...] + mod_ref[0][5:6] * acc


def _combine_call(x1, ysel, top_w, mod_seg, row0, n_rows, seg):
    d = x1.shape[1]
    tm = min(512, seg)
    off = row0 // tm
    return pl.pallas_call(
        _combine_kernel,
        grid=(n_rows // tm,),
        in_specs=[pl.BlockSpec((tm, d), lambda i: (i + off, 0)),
                  pl.BlockSpec((TOP_K, tm, d), lambda i: (0, i + off, 0)),
                  pl.BlockSpec((tm, TOP_K), lambda i: (i + off, 0)),
                  pl.BlockSpec((1, 6, d), lambda i: ((i + off) * tm // seg, 0, 0))],
        out_specs=pl.BlockSpec((tm, d), lambda i: (i, 0)),
        out_shape=jax.ShapeDtypeStruct((n_rows, d), F32),
        compiler_params=_cparams(("arbitrary",)),
        name="moe_combine",
    )(x1, ysel, top_w, mod_seg)


def _pack_w_in(w_in, g_q_a, g_k_a, g_q_b, g_k_b):
    o1 = Q_A
    o2 = o1 + KV_A
    o3 = o2 + KV_A
    o4 = o3 + W_B
    o5 = o4 + W_B
    o6 = o5 + W_B
    d = w_in.shape[0]
    wqa, wka, wva = w_in[:, :o1], w_in[:, o1:o2], w_in[:, o2:o3]
    wqb, wkb, wvb, wg = w_in[:, o3:o4], w_in[:, o4:o5], w_in[:, o5:o6], w_in[:, o6:]

    def dup(w):
        w4 = w.reshape(d, N_KV_A, 1, HEAD_DIM)
        return jnp.broadcast_to(w4, (d, N_KV_A, 2, HEAD_DIM)).reshape(d, N_KV_A * LANES)

    w = jnp.concatenate([wg, wqa, wqb, wkb, dup(wka), dup(wva), wvb], axis=1).astype(BF16)
    q_scale = HEAD_DIM ** -0.5
    gvec = jnp.concatenate([jnp.ones((2 * d,), F32),
                            jnp.tile(g_q_a * q_scale, N_HEADS_A), jnp.tile(g_q_b * q_scale, N_HEADS_B),
                            jnp.tile(g_k_b, N_HEADS_B), jnp.tile(g_k_a, 2 * N_KV_A),
                            jnp.ones((COL_END - COL_VA,), F32)]).reshape(1, -1).astype(F32)
    return w, gvec


def kernel(x_prompt, x_sample, c_prompt, c_sample, w_ada, b_ada, g_norm1, w_in, g_q_a, g_k_a, g_q_b, g_k_b, sink_a, t5_bias, na_bias, w_o_a, w_o_b, w_out, g_norm2, w_router, b_router, w_gu, b_gu, w_down, b_down):
    bp, sp, d = x_prompt.shape
    bs, ss, _ = x_sample.shape
    n_e = w_router.shape[-1]
    tp, ts = bp * sp, bs * ss
    t = tp + ts
    seg = math.gcd(math.gcd(sp, ss), 2048)
    assert w_ada.shape[0] == 1 and sp % seg == 0 and ss % seg == 0 and seg % (NA_HALO_ROWS * GRID_W) == 0
    assert 2 * d == w_in.shape[-1] - (Q_A + 2 * KV_A + 3 * W_B)
    base = 2 * d // LANES
    xp = x_prompt.reshape(tp, d)
    xs = x_sample.reshape(ts, d)

    n_c = bp + bs
    c_pad = jnp.zeros((-(-n_c // 16) * 16, d), F32).at[:n_c].set(jnp.concatenate([c_prompt, c_sample], axis=0))
    mod = _ada_call(c_pad, w_ada[0], b_ada[0])[:n_c].reshape(n_c, 6, d)
    seg_owner = np.concatenate([np.repeat(np.arange(bp), sp // seg), bp + np.repeat(np.arange(bs), ss // seg)])
    mod_seg = mod[seg_owner]

    h1 = _norm_mod_call(xp, xs, mod_seg, g_norm1[0], seg)

    w_slab, gvec = _pack_w_in(w_in[0], g_q_a[0], g_k_a[0], g_q_b[0], g_k_b[0])
    grp = np.arange(MXU_DIM) // HEAD_DIM
    ones_bd = jnp.asarray(grp[:, None] == grp[None, :], BF16)
    proj = _inproj_call(h1, w_slab, gvec, ones_bd, tm=min(1024, seg))

    qs = min(1024, seg)
    sb_tok = np.arange(t // qs) * qs
    seq_start = np.where(sb_tok < tp, sb_tok // sp * sp, tp + (sb_tok - tp) // ss * ss)
    seq_len = np.where(sb_tok < tp, sp, ss)
    flags = np.stack([sb_tok != seq_start, sb_tok + qs != seq_start + seq_len], axis=1).astype(np.int32).reshape(-1)
    attn_a = _win_attn_call(proj, _window_bias(t5_bias), sink_a[0].astype(F32), jnp.asarray(flags), qs, base)

    cfg_ids, kstarts, cfg_blocks = _na_plan([sp] * bp + [ss] * bs, seg)
    bias_cfg = _na_bias_tables(na_bias[0], cfg_blocks)
    attn_b = _na_attn_call(proj, bias_cfg, jnp.asarray(cfg_ids), jnp.asarray(kstarts), seg, base)

    wr = jnp.zeros((d, LANES), F32).at[:, :n_e].set(w_router[0])
    wr_hi = wr.astype(BF16)
    wr_lo = (wr - wr_hi.astype(F32)).astype(BF16)
    br = jnp.zeros((1, LANES), F32).at[0, :n_e].set(b_router[0])
    x1, h2, logits = _outproj_call(attn_a, attn_b, proj, xp, xs, mod_seg, w_o_a[0].astype(BF16),
                                   w_o_b[0].astype(BF16), w_out[0].astype(BF16), g_norm2[0], wr_hi, wr_lo, br, seg)

    tm_e = min(1024, seg)
    top_logits, top_idx = lax.top_k(logits[:, :n_e], TOP_K)
    top_w = jax.nn.softmax(top_logits, axis=-1)
    n_assign = t * TOP_K
    e_flat = top_idx.reshape(-1)
    onehot = (e_flat[:, None] == jnp.arange(n_e, dtype=e_flat.dtype)[None, :]).astype(jnp.int32)
    csum = jnp.cumsum(onehot, axis=0)
    rank = jnp.take_along_axis(csum, e_flat[:, None], axis=1)[:, 0] - 1
    counts = csum[-1]
    padded = (counts + tm_e - 1) // tm_e * tm_e
    pend = jnp.cumsum(padded)
    dest = (pend - padded)[e_flat] + rank
    n_tiles = -(-n_assign // tm_e) + n_e
    n_rows = n_tiles * tm_e
    tile_e = jnp.minimum(jnp.searchsorted(pend, jnp.arange(n_tiles) * tm_e, side='right'), n_e - 1).astype(jnp.int32)
    n_valid = (pend[-1:] // tm_e).astype(jnp.int32)
    tok_flat = jnp.arange(n_assign, dtype=jnp.int32) // TOP_K
    _, tok_sorted = lax.sort((e_flat, tok_flat), num_keys=1, is_stable=True)
    slot = jnp.arange(n_rows, dtype=jnp.int32)
    slot_e = jnp.repeat(tile_e, tm_e)
    idx_in_e = slot - (pend - padded)[slot_e]
    src = jnp.minimum((jnp.cumsum(counts) - counts)[slot_e] + idx_in_e, n_assign - 1)
    buf_tok = jnp.where(idx_in_e < counts[slot_e], tok_sorted[src], t)
    xin = jnp.concatenate([h2, jnp.zeros((1, d), BF16)], axis=0)[buf_tok]

    yb = _moe_call(xin, tile_e, n_valid, _cast_call(w_gu[0]), b_gu[0], _cast_call(w_down[0]), b_down[0], tm_e)

    ysel = yb[dest.reshape(t, TOP_K).T]
    y_prompt = _combine_call(x1, ysel, top_w, mod_seg, 0, tp, seg)
    y_sample = _combine_call(x1, ysel, top_w, mod_seg, tp, ts, seg)
    return (y_prompt.reshape(bp, sp, d), y_sample.reshape(bs, ss, d))
```

```python
import functools
import math

import numpy as np
import jax
import jax.numpy as jnp
from jax import lax
from jax.experimental import pallas as pl
from jax.experimental.pallas import tpu as pltpu

F32 = jnp.float32
BF16 = jnp.bfloat16

HEAD_DIM = 64
N_HEADS_A = 16
N_KV_A = 4
GROUP_A = N_HEADS_A // N_KV_A
N_HEADS_B = 16
WINDOW = 128
ATTN_BLOCK = 128
T5_BUCKETS = 32
T5_MAX_DIST = 128
GRID_W = 64
NA_ROWS = 8
NA_COLS = 16
TOP_K = 4
SWIGLU_LIMIT = 7.0
SWIGLU_ALPHA = 1.702
RMS_EPS = 1e-6

Q_A = N_HEADS_A * HEAD_DIM
KV_A = N_KV_A * HEAD_DIM
W_B = N_HEADS_B * HEAD_DIM

LANES = 128
MXU_DIM = 256
VMEM_LIMIT_BYTES = 56 * 1024 * 1024
MASKED = -1e30

N_HEAD_PAIRS = Q_A // LANES
COL_QA = 0
COL_QB = COL_QA + Q_A
COL_KB = COL_QB + W_B
COL_KA = COL_KB + W_B
COL_VA = COL_KA + N_KV_A * LANES
COL_VB = COL_VA + N_KV_A * LANES
COL_END = COL_VB + W_B
NA_GROUP_ROWS = 2
NA_KEY_ROWS = NA_GROUP_ROWS + NA_ROWS - 1
NA_HALO_ROWS = NA_ROWS
ATTN_UNROLL = 4


def _sigmoid(x):
    return 1.0 / (1.0 + jnp.exp(-x))


def _cparams(sem):
    return pltpu.CompilerParams(dimension_semantics=sem, vmem_limit_bytes=VMEM_LIMIT_BYTES)


def _ada_kernel(c_ref, w_ref, b_ref, o_ref):
    c = c_ref[...]
    a = (c * _sigmoid(c)).astype(BF16)
    o_ref[...] = jnp.dot(a, w_ref[...].astype(BF16), preferred_element_type=F32) + b_ref[...]


def _ada_call(c_pad, w_ada, b_ada):
    rows, d = c_pad.shape
    n = w_ada.shape[1]
    tn = min(n, 1024)
    return pl.pallas_call(
        _ada_kernel,
        grid=(n // tn,),
        in_specs=[pl.BlockSpec((rows, d), lambda j: (0, 0)),
                  pl.BlockSpec((d, tn), lambda j: (0, j)),
                  pl.BlockSpec((1, tn), lambda j: (0, j))],
        out_specs=pl.BlockSpec((rows, tn), lambda j: (0, j)),
        out_shape=jax.ShapeDtypeStruct((rows, n), F32),
        compiler_params=_cparams(("arbitrary",)),
        name="ada_ln",
    )(c_pad, w_ada, b_ada.reshape(1, n))


def _two_group_specs(tm, d, n_prompt_tiles, n_sample_tiles):
    xp = pl.BlockSpec((tm, d), lambda i, *_: (jnp.minimum(i, n_prompt_tiles - 1), 0))
    xs = pl.BlockSpec((tm, d), lambda i, *_: (jnp.clip(i - n_prompt_tiles, 0, n_sample_tiles - 1), 0))
    return xp, xs


def _norm_mod_kernel(xp_ref, xs_ref, mod_ref, g_ref, o_ref, *, n_prompt_tiles):
    i = pl.program_id(0)
    x = jnp.where(i < n_prompt_tiles, xp_ref[...], xs_ref[...])
    y = x * lax.rsqrt(jnp.mean(x * x, axis=-1, keepdims=True) + RMS_EPS) * g_ref[...]
    m = mod_ref[0]
    o_ref[...] = (y * (1.0 + m[1:2]) + m[0:1]).astype(BF16)


def _norm_mod_call(xp, xs, mod_seg, g, seg):
    d = xp.shape[1]
    tm = min(512, seg)
    npt, nst = xp.shape[0] // tm, xs.shape[0] // tm
    xp_spec, xs_spec = _two_group_specs(tm, d, npt, nst)
    return pl.pallas_call(
        functools.partial(_norm_mod_kernel, n_prompt_tiles=npt),
        grid=(npt + nst,),
        in_specs=[xp_spec, xs_spec,
                  pl.BlockSpec((1, 6, d), lambda i: (i * tm // seg, 0, 0)),
                  pl.BlockSpec((1, d), lambda i: (0, 0))],
        out_specs=pl.BlockSpec((tm, d), lambda i: (i, 0)),
        out_shape=jax.ShapeDtypeStruct((xp.shape[0] + xs.shape[0], d), BF16),
        compiler_params=_cparams(("arbitrary",)),
        name="norm1_mod",
    )(xp, xs, mod_seg, g.reshape(1, d))


def _group_rms(y, g, ones_ref):
    sq = (y * y).astype(BF16)
    parts = []
    for c in range(y.shape[1] // MXU_DIM):
        parts.append(jnp.dot(sq[:, c * MXU_DIM:(c + 1) * MXU_DIM], ones_ref[...], preferred_element_type=F32))
    ss = jnp.concatenate(parts, axis=1) if len(parts) > 1 else parts[0]
    return y * lax.rsqrt(ss * (1.0 / HEAD_DIM) + RMS_EPS) * g


def _inproj_kernel(h_ref, w_ref, g_ref, ones_ref, o_ref, *, tn, gate_tiles):
    j = pl.program_id(1)
    y = jnp.dot(h_ref[...], w_ref[...], preferred_element_type=F32)
    half = tn // 2
    mixed_tile = gate_tiles + COL_KA // tn
    plain_tile = gate_tiles + COL_VB // tn

    @pl.when(j < gate_tiles)
    def _():
        o_ref[...] = (0.5 * jnp.tanh(0.5 * y) + 0.5).astype(BF16)

    @pl.when((j >= gate_tiles) & (j < mixed_tile))
    def _():
        o_ref[...] = _group_rms(y, g_ref[...], ones_ref).astype(BF16)

    @pl.when(j == mixed_tile)
    def _():
        o_ref[:, :half] = _group_rms(y[:, :half], g_ref[:, :half], ones_ref).astype(BF16)
        o_ref[:, half:] = y[:, half:].astype(BF16)

    @pl.when(j == plain_tile)
    def _():
        o_ref[...] = y.astype(BF16)


def _inproj_call(h, w, gvec, ones_bd, tm):
    t, d = h.shape
    pw = w.shape[1]
    tn = 1024
    assert (2 * d) % tn == 0 and COL_KA % tn == 0 and COL_VA - COL_KA == tn // 2 and COL_VB % tn == 0
    return pl.pallas_call(
        functools.partial(_inproj_kernel, tn=tn, gate_tiles=2 * d // tn),
        grid=(t // tm, pw // tn),
        in_specs=[pl.BlockSpec((tm, d), lambda i, j: (i, 0)),
                  pl.BlockSpec((d, tn), lambda i, j: (0, j)),
                  pl.BlockSpec((1, tn), lambda i, j: (0, j)),
                  pl.BlockSpec((MXU_DIM, MXU_DIM), lambda i, j: (0, 0))],
        out_specs=pl.BlockSpec((tm, tn), lambda i, j: (i, j)),
        out_shape=jax.ShapeDtypeStruct((t, pw), BF16),
        compiler_params=_cparams(("arbitrary", "arbitrary")),
        name="in_proj",
    )(h, w, gvec, ones_bd)


def _win_attn_kernel(flags_ref, q_ref, kp_ref, kc_ref, kn_ref, vp_ref, vc_ref, vn_ref, bias_ref, sink_ref,
                     o_ref, kcat, vcat, *, qs, nb):
    p = pl.program_id(0)
    sb = pl.program_id(1)
    blk = ATTN_BLOCK
    kcat[0:blk] = kp_ref[...]
    kcat[blk:blk + qs] = kc_ref[...]
    kcat[blk + qs:] = kn_ref[...]
    vcat[0:blk] = vp_ref[...]
    vcat[blk:blk + qs] = vc_ref[...]
    vcat[blk + qs:] = vn_ref[...]
    prev_ok = flags_ref[2 * sb]
    next_ok = flags_ref[2 * sb + 1]
    low_half = lax.broadcasted_iota(jnp.int32, (blk, LANES), 1) < HEAD_DIM
    row = lax.broadcasted_iota(jnp.int32, (2 * blk, 1), 0)
    sinkcol = jnp.where(row < blk, sink_ref[2 * p], sink_ref[2 * p + 1])
    kl = blk + 2 * WINDOW

    def body(b, carry):
        r0 = pl.multiple_of(b * blk, blk)
        q = q_ref[pl.ds(r0, blk), :]
        zero = jnp.zeros_like(q)
        q2 = jnp.concatenate([jnp.where(low_half, q, zero), jnp.where(low_half, zero, q)], axis=0)
        kw = kcat[pl.ds(r0, kl), :]
        vw = vcat[pl.ds(r0, kl), :]
        s = lax.dot_general(q2, kw, (((1,), (1,)), ((), ())), preferred_element_type=F32)
        variant = jnp.where((b == 0) & (prev_ok == 0), 1, jnp.where((b == nb - 1) & (next_ok == 0), 2, 0))
        s = s + bias_ref[variant].reshape(2 * blk, kl)
        m = jnp.maximum(jnp.max(s, axis=-1, keepdims=True), sinkcol)
        e = jnp.exp(s - m)
        denom = jnp.sum(e, axis=-1, keepdims=True) + jnp.exp(sinkcol - m)
        o2 = jnp.dot(e.astype(BF16), vw, preferred_element_type=F32) * (1.0 / denom)
        o_ref[pl.ds(r0, blk), :] = jnp.where(low_half, o2[:blk], o2[blk:]).astype(BF16)
        return carry

    lax.fori_loop(0, nb, body, 0, unroll=math.gcd(nb, ATTN_UNROLL))


def _win_attn_call(proj, bias3, sink, flags, qs, base):
    t = proj.shape[0]
    blk = ATTN_BLOCK
    nb = qs // blk
    n_super = t // qs
    n_blk_rows = t // blk
    kl = blk + 2 * WINDOW
    ka0, va0 = base + COL_KA // LANES, base + COL_VA // LANES

    def cur(col0):
        return pl.BlockSpec((qs, LANES), lambda p, s, f: (s, col0 + p // 2))

    def prev(col0):
        return pl.BlockSpec((blk, LANES), lambda p, s, f: (jnp.maximum(s * nb - 1, 0), col0 + p // 2))

    def nxt(col0):
        return pl.BlockSpec((blk, LANES), lambda p, s, f: (jnp.minimum((s + 1) * nb, n_blk_rows - 1), col0 + p // 2))

    grid_spec = pltpu.PrefetchScalarGridSpec(
        num_scalar_prefetch=1,
        grid=(N_HEAD_PAIRS, n_super),
        in_specs=[pl.BlockSpec((qs, LANES), lambda p, s, f: (s, base + COL_QA // LANES + p)),
                  prev(ka0), cur(ka0), nxt(ka0), prev(va0), cur(va0), nxt(va0),
                  pl.BlockSpec((3, 2, blk, kl), lambda p, s, f: (0, p, 0, 0)),
                  pl.BlockSpec(memory_space=pltpu.SMEM)],
        out_specs=pl.BlockSpec((qs, LANES), lambda p, s, f: (s, p)),
        scratch_shapes=[pltpu.VMEM((qs + 2 * blk, LANES), BF16), pltpu.VMEM((qs + 2 * blk, LANES), BF16)],
    )
    return pl.pallas_call(
        functools.partial(_win_attn_kernel, qs=qs, nb=nb),
        grid_spec=grid_spec,
        out_shape=jax.ShapeDtypeStruct((t, Q_A), BF16),
        compiler_params=_cparams(("arbitrary", "arbitrary")),
        name="win_attn",
    )(flags, proj, proj, proj, proj, proj, proj, proj, bias3, sink)


def _t5_bucket(rel):
    nb = T5_BUCKETS // 2
    max_exact = nb // 2
    ret = jnp.where(rel > 0, nb, 0)
    n = jnp.abs(rel)
    nf = jnp.maximum(n, 1).astype(jnp.float32)
    large = max_exact + (jnp.log(nf / max_exact) / math.log(T5_MAX_DIST / max_exact) * (nb - max_exact)).astype(jnp.int32)
    large = jnp.minimum(large, nb - 1)
    return ret + jnp.where(n < max_exact, n, large)


def _toeplitz(v, n_rows, n_cols):
    p = v.shape[-1]
    assert n_cols <= p - 1
    flat = jnp.tile(v, (1,) * (v.ndim - 1) + (n_rows,))[..., :n_rows * (p - 1)]
    return flat.reshape(v.shape[:-1] + (n_rows, p - 1))[..., :n_cols]


def _window_bias(t5_bias):
    blk, kl = ATTN_BLOCK, ATTN_BLOCK + 2 * WINDOW
    period = blk + kl - 1
    m = np.arange(period)
    rel_of_m = np.where(m < kl, m, m - period) - WINDOW
    per_rel = t5_bias[_t5_bucket(jnp.asarray(rel_of_m))].astype(F32).T
    bias = _toeplitz(per_rel, blk, kl)
    col = np.arange(kl)[None, :]
    band = np.abs(col - WINDOW - np.arange(blk)[:, None]) <= WINDOW
    keep = np.stack([band, band & (col >= WINDOW), band & (col < WINDOW + blk)])
    return jnp.where(keep[:, None], bias[None], MASKED)


def _na_plan(seq_lens, seg):
    qr, nkr = NA_GROUP_ROWS, NA_KEY_ROWS
    configs, cfg_ids, kstarts = {}, [], []
    tok = 0
    for s_len in seq_lens:
        rows = s_len // GRID_W
        assert rows >= nkr and rows % qr == 0
        for r in range(0, rows, qr):
            us = min(max(r - NA_ROWS // 2, 0), rows - nkr)
            rel = tuple(min(max(r + q - NA_ROWS // 2, 0), rows - NA_ROWS) - us for q in range(qr))
            key = (r - us, rel)
            cfg_ids.append(configs.setdefault(key, len(configs)))
            g_tok = tok + r * GRID_W
            seg_tok0 = (g_tok // seg) * seg
            kstarts.append(tok + us * GRID_W - seg_tok0 + NA_HALO_ROWS * GRID_W)
        tok += s_len
    blocks = [None] * len(configs)
    for (r_us, rel), c in configs.items():
        blocks[c] = [[kr - r_us - q + NA_ROWS - 1 if rel[q] <= kr < rel[q] + NA_ROWS else None
                      for kr in range(nkr)] for q in range(qr)]
    return np.asarray(cfg_ids, np.int32), np.asarray(kstarts, np.int32), blocks


def _na_bias_tables(na_bias, blocks):
    period = 2 * GRID_W - 1
    m = np.arange(period)
    dcol = np.where(m < GRID_W, m, m - period)
    per_dcol = na_bias.astype(F32)[:, :, np.clip(dcol + NA_COLS - 1, 0, 2 * NA_COLS - 2)]
    col_tab = _toeplitz(per_dcol, GRID_W, GRID_W)
    q_col, k_col = np.arange(GRID_W)[:, None], np.arange(GRID_W)[None, :]
    win_start = np.clip(q_col - NA_COLS // 2, 0, GRID_W - NA_COLS)
    col_ok = (k_col >= win_start) & (k_col < win_start + NA_COLS)
    col_tab = jnp.where(col_ok, col_tab, MASKED)
    masked_blk = jnp.full(col_tab.shape[:1] + col_tab.shape[2:], MASKED, F32)
    cfgs = []
    for cfg in blocks:
        rows = [jnp.concatenate([masked_blk if dr is None else col_tab[:, dr] for dr in qrow], axis=-1) for qrow in cfg]
        cfgs.append(jnp.concatenate(rows, axis=-2))
    return jnp.stack(cfgs)


def _na_attn_kernel(cfg_ref, ks_ref, q_ref, kp_ref, kc_ref, kn_ref, vp_ref, vc_ref, vn_ref, bias_ref,
                    o_ref, kcat, vcat, *, seg):
    s_id = pl.program_id(1)
    halo = NA_HALO_ROWS * GRID_W
    m, nk = NA_GROUP_ROWS * GRID_W, NA_KEY_ROWS * GRID_W
    n_groups = seg // m
    kcat[0:halo] = kp_ref[...]
    kcat[halo:halo + seg] = kc_ref[...]
    kcat[halo + seg:] = kn_ref[...]
    vcat[0:halo] = vp_ref[...]
    vcat[halo:halo + seg] = vc_ref[...]
    vcat[halo + seg:] = vn_ref[...]
    low_half = lax.broadcasted_iota(jnp.int32, (m, LANES), 1) < HEAD_DIM

    def body(g, carry):
        gg = s_id * n_groups + g
        cfg = cfg_ref[gg]
        ks = pl.multiple_of(ks_ref[gg], GRID_W)
        r0 = pl.multiple_of(g * m, m)
        q = q_ref[pl.ds(r0, m), :]
        zero = jnp.zeros_like(q)
        q2 = jnp.concatenate([jnp.where(low_half, q, zero), jnp.where(low_half, zero, q)], axis=0)
        kw = kcat[pl.ds(ks, nk), :]
        vw = vcat[pl.ds(ks, nk), :]
        s = lax.dot_general(q2, kw, (((1,), (1,)), ((), ())), preferred_element_type=F32)
        s = s + bias_ref[cfg].reshape(2 * m, nk)
        mx = jnp.max(s, axis=-1, keepdims=True)
        e = jnp.exp(s - mx)
        denom = jnp.sum(e, axis=-1, keepdims=True)
        o2 = jnp.dot(e.astype(BF16), vw, preferred_element_type=F32) * (1.0 / denom)
        o_ref[pl.ds(r0, m), :] = jnp.where(low_half, o2[:m], o2[m:]).astype(BF16)
        return carry

    lax.fori_loop(0, n_groups, body, 0, unroll=math.gcd(n_groups, ATTN_UNROLL))


def _na_attn_call(proj, bias_cfg, cfg_ids, kstarts, seg, base):
    t = proj.shape[0]
    halo = NA_HALO_ROWS * GRID_W
    n_seg = t // seg
    per = seg // halo
    n_halo_blocks = t // halo
    ncfg = bias_cfg.shape[0]
    m, nk = NA_GROUP_ROWS * GRID_W, NA_KEY_ROWS * GRID_W
    kb0, vb0 = base + COL_KB // LANES, base + COL_VB // LANES

    def cur(col0):
        return pl.BlockSpec((seg, LANES), lambda p, s, c, k: (s, col0 + p))

    def prev(col0):
        return pl.BlockSpec((halo, LANES), lambda p, s, c, k: (jnp.maximum(s * per - 1, 0), col0 + p))

    def nxt(col0):
        return pl.BlockSpec((halo, LANES), lambda p, s, c, k: (jnp.minimum((s + 1) * per, n_halo_blocks - 1), col0 + p))

    grid_spec = pltpu.PrefetchScalarGridSpec(
        num_scalar_prefetch=2,
        grid=(N_HEAD_PAIRS, n_seg),
        in_specs=[cur(base + COL_QB // LANES), prev(kb0), cur(kb0), nxt(kb0), prev(vb0), cur(vb0), nxt(vb0),
                  pl.BlockSpec((ncfg, 2, m, nk), lambda p, s, c, k: (0, p, 0, 0))],
        out_specs=pl.BlockSpec((seg, LANES), lambda p, s, c, k: (s, p)),
        scratch_shapes=[pltpu.VMEM((seg + 2 * halo, LANES), BF16), pltpu.VMEM((seg + 2 * halo, LANES), BF16)],
    )
    return pl.pallas_call(
        functools.partial(_na_attn_kernel, seg=seg),
        grid_spec=grid_spec,
        out_shape=jax.ShapeDtypeStruct((t, W_B), BF16),
        compiler_params=_cparams(("arbitrary", "arbitrary")),
        name="na_attn",
    )(cfg_ids, kstarts, proj, proj, proj, proj, proj, proj, proj, bias_cfg)


def _outproj_kernel(aa_ref, ab_ref, ga_ref, gb_ref, xp_ref, xs_ref, mod_ref, woa_ref, wob_ref, wout_ref,
                    g2_ref, wrh_ref, wrl_ref, br_ref, x1_ref, h2_ref, lg_ref, *, n_prompt_tiles):
    i = pl.program_id(0)
    ya = jnp.dot(aa_ref[...], woa_ref[...], preferred_element_type=F32)
    yb = jnp.dot(ab_ref[...], wob_ref[...], preferred_element_type=F32)
    merged = ga_ref[...].astype(F32) * ya + gb_ref[...].astype(F32) * yb
    z = jnp.dot(merged.astype(BF16), wout_ref[...], preferred_element_type=F32)
    x = jnp.where(i < n_prompt_tiles, xp_ref[...], xs_ref[...])
    m = mod_ref[0]
    x1 = x + m[2:3] * z
    x1_ref[...] = x1
    y = x1 * lax.rsqrt(jnp.mean(x1 * x1, axis=-1, keepdims=True) + RMS_EPS) * g2_ref[...]
    h2 = y * (1.0 + m[4:5]) + m[3:4]
    h2_hi = h2.astype(BF16)
    h2_ref[...] = h2_hi
    h2_lo = (h2 - h2_hi.astype(F32)).astype(BF16)
    lg = jnp.dot(h2_hi, wrh_ref[...], preferred_element_type=F32)
    lg = lg + jnp.dot(h2_lo, wrh_ref[...], preferred_element_type=F32)
    lg = lg + jnp.dot(h2_hi, wrl_ref[...], preferred_element_type=F32)
    lg_ref[...] = lg + br_ref[...]


def _outproj_call(attn_a, attn_b, proj, xp, xs, mod_seg, woa, wob, wout, g2, wr_hi, wr_lo, br, seg):
    t = attn_a.shape[0]
    d = xp.shape[1]
    tm = min(256, seg)
    npt, nst = xp.shape[0] // tm, xs.shape[0] // tm
    xp_spec, xs_spec = _two_group_specs(tm, d, npt, nst)
    def const(shape):
        return pl.BlockSpec(shape, lambda i: (0,) * len(shape), pipeline_mode=pl.Buffered(1))

    return pl.pallas_call(
        functools.partial(_outproj_kernel, n_prompt_tiles=npt),
        grid=(npt + nst,),
        in_specs=[pl.BlockSpec((tm, Q_A), lambda i: (i, 0)),
                  pl.BlockSpec((tm, W_B), lambda i: (i, 0)),
                  pl.BlockSpec((tm, d), lambda i: (i, 0)),
                  pl.BlockSpec((tm, d), lambda i: (i, 1)),
                  xp_spec, xs_spec,
                  pl.BlockSpec((1, 6, d), lambda i: (i * tm // seg, 0, 0)),
                  const((Q_A, d)), const((W_B, d)), const((d, d)), const((1, d)),
                  const((d, LANES)), const((d, LANES)), const((1, LANES))],
        out_specs=[pl.BlockSpec((tm, d), lambda i: (i, 0)),
                   pl.BlockSpec((tm, d), lambda i: (i, 0)),
                   pl.BlockSpec((tm, LANES), lambda i: (i, 0))],
        out_shape=[jax.ShapeDtypeStruct((t, d), F32),
                   jax.ShapeDtypeStruct((t, d), BF16),
                   jax.ShapeDtypeStruct((t, LANES), F32)],
        compiler_params=_cparams(("arbitrary",)),
        name="out_proj",
    )(attn_a, attn_b, proj, proj, xp, xs, mod_seg, woa, wob, wout, g2.reshape(1, d), wr_hi, wr_lo, br)


def _cast_kernel(w_ref, o_ref):
    o_ref[...] = w_ref[...].astype(BF16)


def _cast_call(w, col_block, n_cols, tr, name):
    n_e, r, _ = w.shape
    tr = min(tr, r)
    return pl.pallas_call(
        _cast_kernel,
        grid=(n_e, r // tr),
        in_specs=[pl.BlockSpec((1, tr, n_cols), lambda e, i: (e, i, col_block))],
        out_specs=pl.BlockSpec((1, tr, n_cols), lambda e, i: (e, i, 0)),
        out_shape=jax.ShapeDtypeStruct((n_e, r, n_cols), BF16),
        compiler_params=_cparams(("arbitrary", "arbitrary")),
        name=name,
    )(w)


def _moe_kernel(te_ref, nv_ref, x_ref, wg_ref, wl_ref, bg_ref, bl_ref, wd_ref, bd_ref, o_ref, acc_ref, *, nf):
    t = pl.program_id(0)
    f = pl.program_id(1)
    live = t < nv_ref[0]

    @pl.when((t == 0) & (f == 0))
    def _():
        acc_ref[...] = jnp.zeros_like(acc_ref)

    @pl.when(live)
    def _():
        x = x_ref[...]
        gate = jnp.dot(x, wg_ref[0], preferred_element_type=F32) + bg_ref[0]
        lin = jnp.dot(x, wl_ref[0], preferred_element_type=F32) + bl_ref[0]
        gate = jnp.minimum(gate, SWIGLU_LIMIT)
        lin = jnp.clip(lin, -SWIGLU_LIMIT, SWIGLU_LIMIT)
        act = gate * _sigmoid(SWIGLU_ALPHA * gate) * (lin + 1.0)
        part = jnp.dot(act.astype(BF16), wd_ref[0], preferred_element_type=F32)
        total = part + jnp.where(f == 0, bd_ref[0], acc_ref[...])
        acc_ref[...] = total
        o_ref[...] = total.astype(BF16)

    @pl.when(jnp.logical_not(live) & (f == 0))
    def _():
        o_ref[...] = jnp.zeros_like(o_ref)


def _moe_call(xin, tile_e, n_valid, w_gate, w_lin, b_gu, w_down, b_down, tm):
    n_rows, d = xin.shape
    n_e, _, d_ff = w_gate.shape
    two_ff = 2 * d_ff
    tf = min(512, d_ff)
    nf = d_ff // tf
    n_tiles = n_rows // tm

    def fidx(t, f, nv):
        return jnp.where(t < nv[0], f, nf - 1)

    grid_spec = pltpu.PrefetchScalarGridSpec(
        num_scalar_prefetch=2,
        grid=(n_tiles, nf),
        in_specs=[pl.BlockSpec((tm, d), lambda t, f, te, nv: (t, 0)),
                  pl.BlockSpec((1, d, tf), lambda t, f, te, nv: (te[t], 0, fidx(t, f, nv))),
                  pl.BlockSpec((1, d, tf), lambda t, f, te, nv: (te[t], 0, fidx(t, f, nv))),
                  pl.BlockSpec((1, 1, tf), lambda t, f, te, nv: (te[t], 0, fidx(t, f, nv))),
                  pl.BlockSpec((1, 1, tf), lambda t, f, te, nv: (te[t], 0, fidx(t, f, nv) + nf)),
                  pl.BlockSpec((1, tf, d), lambda t, f, te, nv: (te[t], fidx(t, f, nv), 0)),
                  pl.BlockSpec((1, 1, d), lambda t, f, te, nv: (te[t], 0, 0))],
        out_specs=pl.BlockSpec((tm, d), lambda t, f, te, nv: (t, 0)),
        scratch_shapes=[pltpu.VMEM((tm, d), F32)],
    )
    return pl.pallas_call(
        functools.partial(_moe_kernel, nf=nf),
        grid_spec=grid_spec,
        out_shape=jax.ShapeDtypeStruct((n_rows, d), BF16),
        compiler_params=_cparams(("arbitrary", "arbitrary")),
        name="moe_experts",
    )(tile_e, n_valid, xin, w_gate, w_lin, b_gu.reshape(n_e, 1, two_ff), b_gu.reshape(n_e, 1, two_ff),
      w_down, b_down.reshape(n_e, 1, d))


def _combine_kernel(x1_ref, ys_ref, w_ref, mod_ref, o_ref):
    w = w_ref[...]
    acc = w[:, 0:1] * ys_ref[0].astype(F32)
    for k in range(1, TOP_K):
        acc = acc + w[:, k:k + 1] * ys_ref[k].astype(F32)
    o_ref[...] = x1_ref[...] + mod_ref[0][5:6] * acc


def _combine_call(x1, ysel, top_w, mod_seg, row0, n_rows, seg):
    d = x1.shape[1]
    tm = min(512, seg)
    off = row0 // tm
    return pl.pallas_call(
        _combine_kernel,
        grid=(n_rows // tm,),
        in_specs=[pl.BlockSpec((tm, d), lambda i: (i + off, 0)),
                  pl.BlockSpec((TOP_K, tm, d), lambda i: (0, i + off, 0)),
                  pl.BlockSpec((tm, TOP_K), lambda i: (i + off, 0)),
                  pl.BlockSpec((1, 6, d), lambda i: ((i + off) * tm // seg, 0, 0))],
        out_specs=pl.BlockSpec((tm, d), lambda i: (i, 0)),
        out_shape=jax.ShapeDtypeStruct((n_rows, d), F32),
        compiler_params=_cparams(("arbitrary",)),
        name="moe_combine",
    )(x1, ysel, top_w, mod_seg)


def _pack_w_in(w_in, g_q_a, g_k_a, g_q_b, g_k_b):
    o1 = Q_A
    o2 = o1 + KV_A
    o3 = o2 + KV_A
    o4 = o3 + W_B
    o5 = o4 + W_B
    o6 = o5 + W_B
    d = w_in.shape[0]
    wqa, wka, wva = w_in[:, :o1], w_in[:, o1:o2], w_in[:, o2:o3]
    wqb, wkb, wvb, wg = w_in[:, o3:o4], w_in[:, o4:o5], w_in[:, o5:o6], w_in[:, o6:]

    def dup(w):
        w4 = w.reshape(d, N_KV_A, 1, HEAD_DIM)
        return jnp.broadcast_to(w4, (d, N_KV_A, 2, HEAD_DIM)).reshape(d, N_KV_A * LANES)

    w = jnp.concatenate([wg, wqa, wqb, wkb, dup(wka), dup(wva), wvb], axis=1).astype(BF16)
    q_scale = HEAD_DIM ** -0.5
    gvec = jnp.concatenate([jnp.ones((2 * d,), F32),
                            jnp.tile(g_q_a * q_scale, N_HEADS_A), jnp.tile(g_q_b * q_scale, N_HEADS_B),
                            jnp.tile(g_k_b, N_HEADS_B), jnp.tile(g_k_a, 2 * N_KV_A),
                            jnp.ones((COL_END - COL_VA,), F32)]).reshape(1, -1).astype(F32)
    return w, gvec


def kernel(x_prompt, x_sample, c_prompt, c_sample, w_ada, b_ada, g_norm1, w_in, g_q_a, g_k_a, g_q_b, g_k_b, sink_a, t5_bias, na_bias, w_o_a, w_o_b, w_out, g_norm2, w_router, b_router, w_gu, b_gu, w_down, b_down):
    bp, sp, d = x_prompt.shape
    bs, ss, _ = x_sample.shape
    n_e = w_router.shape[-1]
    tp, ts = bp * sp, bs * ss
    t = tp + ts
    seg = math.gcd(math.gcd(sp, ss), 2048)
    assert w_ada.shape[0] == 1 and sp % seg == 0 and ss % seg == 0 and seg % (NA_HALO_ROWS * GRID_W) == 0
    assert 2 * d == w_in.shape[-1] - (Q_A + 2 * KV_A + 3 * W_B)
    base = 2 * d // LANES
    xp = x_prompt.reshape(tp, d)
    xs = x_sample.reshape(ts, d)

    n_c = bp + bs
    c_pad = jnp.zeros((-(-n_c // 16) * 16, d), F32).at[:n_c].set(jnp.concatenate([c_prompt, c_sample], axis=0))
    mod = _ada_call(c_pad, w_ada[0], b_ada[0])[:n_c].reshape(n_c, 6, d)
    seg_owner = np.concatenate([np.repeat(np.arange(bp), sp // seg), bp + np.repeat(np.arange(bs), ss // seg)])
    mod_seg = mod[seg_owner]

    h1 = _norm_mod_call(xp, xs, mod_seg, g_norm1[0], seg)

    w_slab, gvec = _pack_w_in(w_in[0], g_q_a[0], g_k_a[0], g_q_b[0], g_k_b[0])
    grp = np.arange(MXU_DIM) // HEAD_DIM
    ones_bd = jnp.asarray(grp[:, None] == grp[None, :], BF16)
    proj = _inproj_call(h1, w_slab, gvec, ones_bd, tm=min(1024, seg))

    qs = min(1024, seg)
    sb_tok = np.arange(t // qs) * qs
    seq_start = np.where(sb_tok < tp, sb_tok // sp * sp, tp + (sb_tok - tp) // ss * ss)
    seq_len = np.where(sb_tok < tp, sp, ss)
    flags = np.stack([sb_tok != seq_start, sb_tok + qs != seq_start + seq_len], axis=1).astype(np.int32).reshape(-1)
    attn_a = _win_attn_call(proj, _window_bias(t5_bias), sink_a[0].astype(F32), jnp.asarray(flags), qs, base)

    cfg_ids, kstarts, cfg_blocks = _na_plan([sp] * bp + [ss] * bs, seg)
    bias_cfg = _na_bias_tables(na_bias[0], cfg_blocks)
    attn_b = _na_attn_call(proj, bias_cfg, jnp.asarray(cfg_ids), jnp.asarray(kstarts), seg, base)

    wr = jnp.zeros((d, LANES), F32).at[:, :n_e].set(w_router[0])
    wr_hi = wr.astype(BF16)
    wr_lo = (wr - wr_hi.astype(F32)).astype(BF16)
    br = jnp.zeros((1, LANES), F32).at[0, :n_e].set(b_router[0])
    x1, h2, logits = _outproj_call(attn_a, attn_b, proj, xp, xs, mod_seg, w_o_a[0].astype(BF16),
                                   w_o_b[0].astype(BF16), w_out[0].astype(BF16), g_norm2[0], wr_hi, wr_lo, br, seg)

    tm_e = min(1024, seg)
    top_logits, top_idx = lax.top_k(logits[:, :n_e], TOP_K)
    top_w = jax.nn.softmax(top_logits, axis=-1)
    n_assign = t * TOP_K
    e_flat = top_idx.reshape(-1)
    onehot = (e_flat[:, None] == jnp.arange(n_e, dtype=e_flat.dtype)[None, :]).astype(jnp.int32)
    csum = jnp.cumsum(onehot, axis=0)
    rank = jnp.take_along_axis(csum, e_flat[:, None], axis=1)[:, 0] - 1
    counts = csum[-1]
    padded = (counts + tm_e - 1) // tm_e * tm_e
    pend = jnp.cumsum(padded)
    dest = (pend - padded)[e_flat] + rank
    n_tiles = -(-n_assign // tm_e) + n_e
    n_rows = n_tiles * tm_e
    tile_e = jnp.minimum(jnp.searchsorted(pend, jnp.arange(n_tiles) * tm_e, side='right'), n_e - 1).astype(jnp.int32)
    n_valid = (pend[-1:] // tm_e).astype(jnp.int32)
    tok_flat = jnp.arange(n_assign, dtype=jnp.int32) // TOP_K
    _, tok_sorted = lax.sort((e_flat, tok_flat), num_keys=1, is_stable=True)
    slot = jnp.arange(n_rows, dtype=jnp.int32)
    slot_e = jnp.repeat(tile_e, tm_e)
    idx_in_e = slot - (pend - padded)[slot_e]
    src = jnp.minimum((jnp.cumsum(counts) - counts)[slot_e] + idx_in_e, n_assign - 1)
    buf_tok = jnp.where(idx_in_e < counts[slot_e], tok_sorted[src], t)
    xin = jnp.concatenate([h2, jnp.zeros((1, d), BF16)], axis=0)[buf_tok]

    d_ff = w_down.shape[-2]
    w_gate = _cast_call(w_gu[0], 0, d_ff, 512, "cast_gate")
    w_lin = _cast_call(w_gu[0], 1, d_ff, 2048, "cast_lin")
    w_dn = _cast_call(w_down[0], 0, d, 512, "cast_down")
    yb = _moe_call(xin, tile_e, n_valid, w_gate, w_lin, b_gu[0], w_dn, b_down[0], tm_e)

    ysel = yb[dest.reshape(t, TOP_K).T]
    y_prompt = _combine_call(x1, ysel, top_w, mod_seg, 0, tp, seg)
    y_sample = _combine_call(x1, ysel, top_w, mod_seg, tp, ts, seg)
    return (y_prompt.reshape(bp, sp, d), y_sample.reshape(bs, ss, d))
```

```python
import functools
import math

import numpy as np
import jax
import jax.numpy as jnp
from jax import lax
from jax.experimental import pallas as pl
from jax.experimental.pallas import tpu as pltpu

F32 = jnp.float32
BF16 = jnp.bfloat16

HEAD_DIM = 64
N_HEADS_A = 16
N_KV_A = 4
GROUP_A = N_HEADS_A // N_KV_A
N_HEADS_B = 16
WINDOW = 128
ATTN_BLOCK = 128
T5_BUCKETS = 32
T5_MAX_DIST = 128
GRID_W = 64
NA_ROWS = 8
NA_COLS = 16
TOP_K = 4
SWIGLU_LIMIT = 7.0
SWIGLU_ALPHA = 1.702
RMS_EPS = 1e-6

Q_A = N_HEADS_A * HEAD_DIM
KV_A = N_KV_A * HEAD_DIM
W_B = N_HEADS_B * HEAD_DIM

LANES = 128
MXU_DIM = 256
VMEM_LIMIT_BYTES = 56 * 1024 * 1024
MASKED = -1e30

N_HEAD_PAIRS = Q_A // LANES
COL_QA = 0
COL_QB = COL_QA + Q_A
COL_KB = COL_QB + W_B
COL_KA = COL_KB + W_B
COL_VA = COL_KA + N_KV_A * LANES
COL_VB = COL_VA + N_KV_A * LANES
COL_END = COL_VB + W_B
NA_GROUP_ROWS = 2
NA_KEY_ROWS = NA_GROUP_ROWS + NA_ROWS - 1
NA_HALO_ROWS = NA_ROWS


def _sigmoid(x):
    return 1.0 / (1.0 + jnp.exp(-x))


def _cparams(sem):
    return pltpu.CompilerParams(dimension_semantics=sem, vmem_limit_bytes=VMEM_LIMIT_BYTES)


def _ada_kernel(c_ref, w_ref, b_ref, o_ref):
    c = c_ref[...]
    a = (c * _sigmoid(c)).astype(BF16)
    o_ref[...] = jnp.dot(a, w_ref[...].astype(BF16), preferred_element_type=F32) + b_ref[...]


def _ada_call(c_pad, w_ada, b_ada):
    rows, d = c_pad.shape
    n = w_ada.shape[1]
    tn = min(n, 1024)
    return pl.pallas_call(
        _ada_kernel,
        grid=(n // tn,),
        in_specs=[pl.BlockSpec((rows, d), lambda j: (0, 0)),
                  pl.BlockSpec((d, tn), lambda j: (0, j)),
                  pl.BlockSpec((1, tn), lambda j: (0, j))],
        out_specs=pl.BlockSpec((rows, tn), lambda j: (0, j)),
        out_shape=jax.ShapeDtypeStruct((rows, n), F32),
        compiler_params=_cparams(("arbitrary",)),
        name="ada_ln",
    )(c_pad, w_ada, b_ada.reshape(1, n))


def _two_group_specs(tm, d, n_prompt_tiles, n_sample_tiles):
    xp = pl.BlockSpec((tm, d), lambda i, *_: (jnp.minimum(i, n_prompt_tiles - 1), 0))
    xs = pl.BlockSpec((tm, d), lambda i, *_: (jnp.clip(i - n_prompt_tiles, 0, n_sample_tiles - 1), 0))
    return xp, xs


def _norm_mod_kernel(xp_ref, xs_ref, mod_ref, g_ref, o_ref, *, n_prompt_tiles):
    i = pl.program_id(0)
    x = jnp.where(i < n_prompt_tiles, xp_ref[...], xs_ref[...])
    y = x * lax.rsqrt(jnp.mean(x * x, axis=-1, keepdims=True) + RMS_EPS) * g_ref[...]
    m = mod_ref[0]
    o_ref[...] = (y * (1.0 + m[1:2]) + m[0:1]).astype(BF16)


def _norm_mod_call(xp, xs, mod_seg, g, seg):
    d = xp.shape[1]
    tm = min(512, seg)
    npt, nst = xp.shape[0] // tm, xs.shape[0] // tm
    xp_spec, xs_spec = _two_group_specs(tm, d, npt, nst)
    return pl.pallas_call(
        functools.partial(_norm_mod_kernel, n_prompt_tiles=npt),
        grid=(npt + nst,),
        in_specs=[xp_spec, xs_spec,
                  pl.BlockSpec((1, 6, d), lambda i: (i * tm // seg, 0, 0)),
                  pl.BlockSpec((1, d), lambda i: (0, 0))],
        out_specs=pl.BlockSpec((tm, d), lambda i: (i, 0)),
        out_shape=jax.ShapeDtypeStruct((xp.shape[0] + xs.shape[0], d), BF16),
        compiler_params=_cparams(("arbitrary",)),
        name="norm1_mod",
    )(xp, xs, mod_seg, g.reshape(1, d))


def _group_rms(y, g, ones_ref):
    sq = (y * y).astype(BF16)
    parts = []
    for c in range(y.shape[1] // MXU_DIM):
        parts.append(jnp.dot(sq[:, c * MXU_DIM:(c + 1) * MXU_DIM], ones_ref[...], preferred_element_type=F32))
    ss = jnp.concatenate(parts, axis=1) if len(parts) > 1 else parts[0]
    return y * lax.rsqrt(ss * (1.0 / HEAD_DIM) + RMS_EPS) * g


def _inproj_kernel(h_ref, w_ref, g_ref, ones_ref, o_ref, *, tn, gate_tiles):
    j = pl.program_id(1)
    y = jnp.dot(h_ref[...], w_ref[...], preferred_element_type=F32)
    half = tn // 2
    mixed_tile = gate_tiles + COL_KA // tn
    plain_tile = gate_tiles + COL_VB // tn

    @pl.when(j < gate_tiles)
    def _():
        o_ref[...] = (0.5 * jnp.tanh(0.5 * y) + 0.5).astype(BF16)

    @pl.when((j >= gate_tiles) & (j < mixed_tile))
    def _():
        o_ref[...] = _group_rms(y, g_ref[...], ones_ref).astype(BF16)

    @pl.when(j == mixed_tile)
    def _():
        o_ref[:, :half] = _group_rms(y[:, :half], g_ref[:, :half], ones_ref).astype(BF16)
        o_ref[:, half:] = y[:, half:].astype(BF16)

    @pl.when(j == plain_tile)
    def _():
        o_ref[...] = y.astype(BF16)


def _inproj_call(h, w, gvec, ones_bd, tm):
    t, d = h.shape
    pw = w.shape[1]
    tn = 1024
    assert (2 * d) % tn == 0 and COL_KA % tn == 0 and COL_VA - COL_KA == tn // 2 and COL_VB % tn == 0
    return pl.pallas_call(
        functools.partial(_inproj_kernel, tn=tn, gate_tiles=2 * d // tn),
        grid=(t // tm, pw // tn),
        in_specs=[pl.BlockSpec((tm, d), lambda i, j: (i, 0)),
                  pl.BlockSpec((d, tn), lambda i, j: (0, j)),
                  pl.BlockSpec((1, tn), lambda i, j: (0, j)),
                  pl.BlockSpec((MXU_DIM, MXU_DIM), lambda i, j: (0, 0))],
        out_specs=pl.BlockSpec((tm, tn), lambda i, j: (i, j)),
        out_shape=jax.ShapeDtypeStruct((t, pw), BF16),
        compiler_params=_cparams(("arbitrary", "arbitrary")),
        name="in_proj",
    )(h, w, gvec, ones_bd)


def _aligned(x, m):
    return x if isinstance(x, int) else pl.multiple_of(x, m)


def _three_stage_pipeline(n, stage_a, stage_b, stage_c):
    assert n >= 3
    stage_a(0)
    stage_a(1)
    stage_b(0)

    def body(g, carry):
        stage_a(g)
        stage_b(g - 1)
        stage_c(g - 2)
        return carry

    lax.fori_loop(2, n, body, 0)
    stage_b(n - 1)
    stage_c(n - 2)
    stage_c(n - 1)


def _win_attn_kernel(flags_ref, q_ref, kp_ref, kc_ref, kn_ref, vp_ref, vc_ref, vn_ref, bias_ref, sink_ref,
                     o_ref, kcat, vcat, s_scr, e_scr, r_scr, *, qs, nb):
    p = pl.program_id(0)
    sb = pl.program_id(1)
    blk = ATTN_BLOCK
    kcat[0:blk] = kp_ref[...]
    kcat[blk:blk + qs] = kc_ref[...]
    kcat[blk + qs:] = kn_ref[...]
    vcat[0:blk] = vp_ref[...]
    vcat[blk:blk + qs] = vc_ref[...]
    vcat[blk + qs:] = vn_ref[...]
    prev_ok = flags_ref[2 * sb]
    next_ok = flags_ref[2 * sb + 1]
    low_half = lax.broadcasted_iota(jnp.int32, (blk, LANES), 1) < HEAD_DIM
    row = lax.broadcasted_iota(jnp.int32, (2 * blk, 1), 0)
    sinkcol = jnp.where(row < blk, sink_ref[2 * p], sink_ref[2 * p + 1])
    kl = blk + 2 * WINDOW

    def scores(b):
        r0 = _aligned(b * blk, blk)
        q = q_ref[pl.ds(r0, blk), :]
        zero = jnp.zeros_like(q)
        q2 = jnp.concatenate([jnp.where(low_half, q, zero), jnp.where(low_half, zero, q)], axis=0)
        kw = kcat[pl.ds(r0, kl), :]
        s_scr[b % 2] = lax.dot_general(q2, kw, (((1,), (1,)), ((), ())), preferred_element_type=F32)

    def softmax(b):
        variant = jnp.where((b == 0) & (prev_ok == 0), 1, jnp.where((b == nb - 1) & (next_ok == 0), 2, 0))
        s = s_scr[b % 2] + bias_ref[variant].reshape(2 * blk, kl)
        m = jnp.maximum(jnp.max(s, axis=-1, keepdims=True), sinkcol)
        e = jnp.exp(s - m)
        denom = jnp.sum(e, axis=-1, keepdims=True) + jnp.exp(sinkcol - m)
        e_scr[b % 2] = e.astype(BF16)
        r_scr[b % 2] = 1.0 / denom

    def values(b):
        r0 = _aligned(b * blk, blk)
        vw = vcat[pl.ds(r0, kl), :]
        o2 = jnp.dot(e_scr[b % 2], vw, preferred_element_type=F32) * r_scr[b % 2]
        o_ref[pl.ds(r0, blk), :] = jnp.where(low_half, o2[:blk], o2[blk:]).astype(BF16)

    _three_stage_pipeline(nb, scores, softmax, values)


def _win_attn_call(proj, bias3, sink, flags, qs, base):
    t = proj.shape[0]
    blk = ATTN_BLOCK
    nb = qs // blk
    n_super = t // qs
    n_blk_rows = t // blk
    kl = blk + 2 * WINDOW
    ka0, va0 = base + COL_KA // LANES, base + COL_VA // LANES

    def cur(col0):
        return pl.BlockSpec((qs, LANES), lambda p, s, f: (s, col0 + p // 2))

    def prev(col0):
        return pl.BlockSpec((blk, LANES), lambda p, s, f: (jnp.maximum(s * nb - 1, 0), col0 + p // 2))

    def nxt(col0):
        return pl.BlockSpec((blk, LANES), lambda p, s, f: (jnp.minimum((s + 1) * nb, n_blk_rows - 1), col0 + p // 2))

    grid_spec = pltpu.PrefetchScalarGridSpec(
        num_scalar_prefetch=1,
        grid=(N_HEAD_PAIRS, n_super),
        in_specs=[pl.BlockSpec((qs, LANES), lambda p, s, f: (s, base + COL_QA // LANES + p)),
                  prev(ka0), cur(ka0), nxt(ka0), prev(va0), cur(va0), nxt(va0),
                  pl.BlockSpec((3, 2, blk, kl), lambda p, s, f: (0, p, 0, 0)),
                  pl.BlockSpec(memory_space=pltpu.SMEM)],
        out_specs=pl.BlockSpec((qs, LANES), lambda p, s, f: (s, p)),
        scratch_shapes=[pltpu.VMEM((qs + 2 * blk, LANES), BF16), pltpu.VMEM((qs + 2 * blk, LANES), BF16),
                        pltpu.VMEM((2, 2 * blk, kl), F32), pltpu.VMEM((2, 2 * blk, kl), BF16),
                        pltpu.VMEM((2, 2 * blk, 1), F32)],
    )
    return pl.pallas_call(
        functools.partial(_win_attn_kernel, qs=qs, nb=nb),
        grid_spec=grid_spec,
        out_shape=jax.ShapeDtypeStruct((t, Q_A), BF16),
        compiler_params=_cparams(("arbitrary", "arbitrary")),
        name="win_attn",
    )(flags, proj, proj, proj, proj, proj, proj, proj, bias3, sink)


def _t5_bucket(rel):
    nb = T5_BUCKETS // 2
    max_exact = nb // 2
    ret = jnp.where(rel > 0, nb, 0)
    n = jnp.abs(rel)
    nf = jnp.maximum(n, 1).astype(jnp.float32)
    large = max_exact + (jnp.log(nf / max_exact) / math.log(T5_MAX_DIST / max_exact) * (nb - max_exact)).astype(jnp.int32)
    large = jnp.minimum(large, nb - 1)
    return ret + jnp.where(n < max_exact, n, large)


def _toeplitz(v, n_rows, n_cols):
    p = v.shape[-1]
    assert n_cols <= p - 1
    flat = jnp.tile(v, (1,) * (v.ndim - 1) + (n_rows,))[..., :n_rows * (p - 1)]
    return flat.reshape(v.shape[:-1] + (n_rows, p - 1))[..., :n_cols]


def _window_bias(t5_bias):
    blk, kl = ATTN_BLOCK, ATTN_BLOCK + 2 * WINDOW
    period = blk + kl - 1
    m = np.arange(period)
    rel_of_m = np.where(m < kl, m, m - period) - WINDOW
    per_rel = t5_bias[_t5_bucket(jnp.asarray(rel_of_m))].astype(F32).T
    bias = _toeplitz(per_rel, blk, kl)
    col = np.arange(kl)[None, :]
    band = np.abs(col - WINDOW - np.arange(blk)[:, None]) <= WINDOW
    keep = np.stack([band, band & (col >= WINDOW), band & (col < WINDOW + blk)])
    return jnp.where(keep[:, None], bias[None], MASKED)


def _na_plan(seq_lens, seg):
    qr, nkr = NA_GROUP_ROWS, NA_KEY_ROWS
    configs, cfg_ids, kstarts = {}, [], []
    tok = 0
    for s_len in seq_lens:
        rows = s_len // GRID_W
        assert rows >= nkr and rows % qr == 0
        for r in range(0, rows, qr):
            us = min(max(r - NA_ROWS // 2, 0), rows - nkr)
            rel = tuple(min(max(r + q - NA_ROWS // 2, 0), rows - NA_ROWS) - us for q in range(qr))
            key = (r - us, rel)
            cfg_ids.append(configs.setdefault(key, len(configs)))
            g_tok = tok + r * GRID_W
            seg_tok0 = (g_tok // seg) * seg
            kstarts.append(tok + us * GRID_W - seg_tok0 + NA_HALO_ROWS * GRID_W)
        tok += s_len
    blocks = [None] * len(configs)
    for (r_us, rel), c in configs.items():
        blocks[c] = [[kr - r_us - q + NA_ROWS - 1 if rel[q] <= kr < rel[q] + NA_ROWS else None
                      for kr in range(nkr)] for q in range(qr)]
    return np.asarray(cfg_ids, np.int32), np.asarray(kstarts, np.int32), blocks


def _na_bias_tables(na_bias, blocks):
    period = 2 * GRID_W - 1
    m = np.arange(period)
    dcol = np.where(m < GRID_W, m, m - period)
    per_dcol = na_bias.astype(F32)[:, :, np.clip(dcol + NA_COLS - 1, 0, 2 * NA_COLS - 2)]
    col_tab = _toeplitz(per_dcol, GRID_W, GRID_W)
    q_col, k_col = np.arange(GRID_W)[:, None], np.arange(GRID_W)[None, :]
    win_start = np.clip(q_col - NA_COLS // 2, 0, GRID_W - NA_COLS)
    col_ok = (k_col >= win_start) & (k_col < win_start + NA_COLS)
    col_tab = jnp.where(col_ok, col_tab, MASKED)
    masked_blk = jnp.full(col_tab.shape[:1] + col_tab.shape[2:], MASKED, F32)
    cfgs = []
    for cfg in blocks:
        rows = [jnp.concatenate([masked_blk if dr is None else col_tab[:, dr] for dr in qrow], axis=-1) for qrow in cfg]
        cfgs.append(jnp.concatenate(rows, axis=-2))
    return jnp.stack(cfgs)


def _na_attn_kernel(cfg_ref, ks_ref, q_ref, kp_ref, kc_ref, kn_ref, vp_ref, vc_ref, vn_ref, bias_ref,
                    o_ref, kcat, vcat, s_scr, e_scr, r_scr, *, seg):
    s_id = pl.program_id(1)
    halo = NA_HALO_ROWS * GRID_W
    m, nk = NA_GROUP_ROWS * GRID_W, NA_KEY_ROWS * GRID_W
    n_groups = seg // m
    kcat[0:halo] = kp_ref[...]
    kcat[halo:halo + seg] = kc_ref[...]
    kcat[halo + seg:] = kn_ref[...]
    vcat[0:halo] = vp_ref[...]
    vcat[halo:halo + seg] = vc_ref[...]
    vcat[halo + seg:] = vn_ref[...]
    low_half = lax.broadcasted_iota(jnp.int32, (m, LANES), 1) < HEAD_DIM

    def scores(g):
        ks = pl.multiple_of(ks_ref[s_id * n_groups + g], GRID_W)
        r0 = _aligned(g * m, m)
        q = q_ref[pl.ds(r0, m), :]
        zero = jnp.zeros_like(q)
        q2 = jnp.concatenate([jnp.where(low_half, q, zero), jnp.where(low_half, zero, q)], axis=0)
        kw = kcat[pl.ds(ks, nk), :]
        s_scr[g % 2] = lax.dot_general(q2, kw, (((1,), (1,)), ((), ())), preferred_element_type=F32)

    def softmax(g):
        cfg = cfg_ref[s_id * n_groups + g]
        s = s_scr[g % 2] + bias_ref[cfg].reshape(2 * m, nk)
        mx = jnp.max(s, axis=-1, keepdims=True)
        e = jnp.exp(s - mx)
        e_scr[g % 2] = e.astype(BF16)
        r_scr[g % 2] = 1.0 / jnp.sum(e, axis=-1, keepdims=True)

    def values(g):
        ks = pl.multiple_of(ks_ref[s_id * n_groups + g], GRID_W)
        r0 = _aligned(g * m, m)
        vw = vcat[pl.ds(ks, nk), :]
        o2 = jnp.dot(e_scr[g % 2], vw, preferred_element_type=F32) * r_scr[g % 2]
        o_ref[pl.ds(r0, m), :] = jnp.where(low_half, o2[:m], o2[m:]).astype(BF16)

    _three_stage_pipeline(n_groups, scores, softmax, values)


def _na_attn_call(proj, bias_cfg, cfg_ids, kstarts, seg, base):
    t = proj.shape[0]
    halo = NA_HALO_ROWS * GRID_W
    n_seg = t // seg
    per = seg // halo
    n_halo_blocks = t // halo
    ncfg = bias_cfg.shape[0]
    m, nk = NA_GROUP_ROWS * GRID_W, NA_KEY_ROWS * GRID_W
    kb0, vb0 = base + COL_KB // LANES, base + COL_VB // LANES

    def cur(col0):
        return pl.BlockSpec((seg, LANES), lambda p, s, c, k: (s, col0 + p))

    def prev(col0):
        return pl.BlockSpec((halo, LANES), lambda p, s, c, k: (jnp.maximum(s * per - 1, 0), col0 + p))

    def nxt(col0):
        return pl.BlockSpec((halo, LANES), lambda p, s, c, k: (jnp.minimum((s + 1) * per, n_halo_blocks - 1), col0 + p))

    grid_spec = pltpu.PrefetchScalarGridSpec(
        num_scalar_prefetch=2,
        grid=(N_HEAD_PAIRS, n_seg),
        in_specs=[cur(base + COL_QB // LANES), prev(kb0), cur(kb0), nxt(kb0), prev(vb0), cur(vb0), nxt(vb0),
                  pl.BlockSpec((ncfg, 2, m, nk), lambda p, s, c, k: (0, p, 0, 0))],
        out_specs=pl.BlockSpec((seg, LANES), lambda p, s, c, k: (s, p)),
        scratch_shapes=[pltpu.VMEM((seg + 2 * halo, LANES), BF16), pltpu.VMEM((seg + 2 * halo, LANES), BF16),
                        pltpu.VMEM((2, 2 * m, nk), F32), pltpu.VMEM((2, 2 * m, nk), BF16),
                        pltpu.VMEM((2, 2 * m, 1), F32)],
    )
    return pl.pallas_call(
        functools.partial(_na_attn_kernel, seg=seg),
        grid_spec=grid_spec,
        out_shape=jax.ShapeDtypeStruct((t, W_B), BF16),
        compiler_params=_cparams(("arbitrary", "arbitrary")),
        name="na_attn",
    )(cfg_ids, kstarts, proj, proj, proj, proj, proj, proj, proj, bias_cfg)


def _outproj_kernel(aa_ref, ab_ref, ga_ref, gb_ref, xp_ref, xs_ref, mod_ref, woa_ref, wob_ref, wout_ref,
                    g2_ref, wrh_ref, wrl_ref, br_ref, x1_ref, h2_ref, lg_ref, *, n_prompt_tiles):
    i = pl.program_id(0)
    ya = jnp.dot(aa_ref[...], woa_ref[...], preferred_element_type=F32)
    yb = jnp.dot(ab_ref[...], wob_ref[...], preferred_element_type=F32)
    merged = ga_ref[...].astype(F32) * ya + gb_ref[...].astype(F32) * yb
    z = jnp.dot(merged.astype(BF16), wout_ref[...], preferred_element_type=F32)
    x = jnp.where(i < n_prompt_tiles, xp_ref[...], xs_ref[...])
    m = mod_ref[0]
    x1 = x + m[2:3] * z
    x1_ref[...] = x1
    y = x1 * lax.rsqrt(jnp.mean(x1 * x1, axis=-1, keepdims=True) + RMS_EPS) * g2_ref[...]
    h2 = y * (1.0 + m[4:5]) + m[3:4]
    h2_hi = h2.astype(BF16)
    h2_ref[...] = h2_hi
    h2_lo = (h2 - h2_hi.astype(F32)).astype(BF16)
    lg = jnp.dot(h2_hi, wrh_ref[...], preferred_element_type=F32)
    lg = lg + jnp.dot(h2_lo, wrh_ref[...], preferred_element_type=F32)
    lg = lg + jnp.dot(h2_hi, wrl_ref[...], preferred_element_type=F32)
    lg_ref[...] = lg + br_ref[...]


def _outproj_call(attn_a, attn_b, proj, xp, xs, mod_seg, woa, wob, wout, g2, wr_hi, wr_lo, br, seg):
    t = attn_a.shape[0]
    d = xp.shape[1]
    tm = min(256, seg)
    npt, nst = xp.shape[0] // tm, xs.shape[0] // tm
    xp_spec, xs_spec = _two_group_specs(tm, d, npt, nst)
    def const(shape):
        return pl.BlockSpec(shape, lambda i: (0,) * len(shape), pipeline_mode=pl.Buffered(1))

    return pl.pallas_call(
        functools.partial(_outproj_kernel, n_prompt_tiles=npt),
        grid=(npt + nst,),
        in_specs=[pl.BlockSpec((tm, Q_A), lambda i: (i, 0)),
                  pl.BlockSpec((tm, W_B), lambda i: (i, 0)),
                  pl.BlockSpec((tm, d), lambda i: (i, 0)),
                  pl.BlockSpec((tm, d), lambda i: (i, 1)),
                  xp_spec, xs_spec,
                  pl.BlockSpec((1, 6, d), lambda i: (i * tm // seg, 0, 0)),
                  const((Q_A, d)), const((W_B, d)), const((d, d)), const((1, d)),
                  const((d, LANES)), const((d, LANES)), const((1, LANES))],
        out_specs=[pl.BlockSpec((tm, d), lambda i: (i, 0)),
                   pl.BlockSpec((tm, d), lambda i: (i, 0)),
                   pl.BlockSpec((tm, LANES), lambda i: (i, 0))],
        out_shape=[jax.ShapeDtypeStruct((t, d), F32),
                   jax.ShapeDtypeStruct((t, d), BF16),
                   jax.ShapeDtypeStruct((t, LANES), F32)],
        compiler_params=_cparams(("arbitrary",)),
        name="out_proj",
    )(attn_a, attn_b, proj, proj, xp, xs, mod_seg, woa, wob, wout, g2.reshape(1, d), wr_hi, wr_lo, br)


def _cast_kernel(w_ref, o_ref):
    o_ref[...] = w_ref[...].astype(BF16)


def _cast_call(w, col_block, n_cols, tr, name):
    n_e, r, c = w.shape
    tr = min(tr, r)
    return pl.pallas_call(
        _cast_kernel,
        grid=(n_e * r // tr,),
        in_specs=[pl.BlockSpec((tr, n_cols), lambda i: (i, col_block))],
        out_specs=pl.BlockSpec((tr, n_cols), lambda i: (i, 0)),
        out_shape=jax.ShapeDtypeStruct((n_e * r, n_cols), BF16),
        compiler_params=_cparams(("arbitrary",)),
        name=name,
    )(w.reshape(n_e * r, c))


def _moe_kernel(te_ref, nv_ref, x_ref, wg_ref, wl_ref, bg_ref, bl_ref, wd_ref, bd_ref, o_ref, acc_ref, *, nf):
    t = pl.program_id(0)
    f = pl.program_id(1)
    live = t < nv_ref[0]

    @pl.when((t == 0) & (f == 0))
    def _():
        acc_ref[...] = jnp.zeros_like(acc_ref)

    @pl.when(live)
    def _():
        x = x_ref[...]
        gate = jnp.dot(x, wg_ref[...], preferred_element_type=F32) + bg_ref[0]
        lin = jnp.dot(x, wl_ref[...], preferred_element_type=F32) + bl_ref[0]
        gate = jnp.minimum(gate, SWIGLU_LIMIT)
        lin = jnp.clip(lin, -SWIGLU_LIMIT, SWIGLU_LIMIT)
        act = gate * _sigmoid(SWIGLU_ALPHA * gate) * (lin + 1.0)
        part = jnp.dot(act.astype(BF16), wd_ref[...], preferred_element_type=F32)
        total = part + jnp.where(f == 0, bd_ref[0], acc_ref[...])
        acc_ref[...] = total
        o_ref[...] = total.astype(BF16)

    @pl.when(jnp.logical_not(live) & (f == 0))
    def _():
        o_ref[...] = jnp.zeros_like(o_ref)


def _moe_call(xin, tile_e, n_valid, w_gate, w_lin, b_gu, w_down, b_down, tm):
    n_rows, d = xin.shape
    n_e, two_ff = b_gu.shape
    d_ff = two_ff // 2
    tf = min(512, d_ff)
    nf = d_ff // tf
    n_tiles = n_rows // tm

    def fidx(t, f, nv):
        return jnp.where(t < nv[0], f, nf - 1)

    grid_spec = pltpu.PrefetchScalarGridSpec(
        num_scalar_prefetch=2,
        grid=(n_tiles, nf),
        in_specs=[pl.BlockSpec((tm, d), lambda t, f, te, nv: (t, 0)),
                  pl.BlockSpec((d, tf), lambda t, f, te, nv: (te[t], fidx(t, f, nv))),
                  pl.BlockSpec((d, tf), lambda t, f, te, nv: (te[t], fidx(t, f, nv))),
                  pl.BlockSpec((1, 1, tf), lambda t, f, te, nv: (te[t], 0, fidx(t, f, nv))),
                  pl.BlockSpec((1, 1, tf), lambda t, f, te, nv: (te[t], 0, fidx(t, f, nv) + nf)),
                  pl.BlockSpec((tf, d), lambda t, f, te, nv: (te[t] * nf + fidx(t, f, nv), 0)),
                  pl.BlockSpec((1, 1, d), lambda t, f, te, nv: (te[t], 0, 0))],
        out_specs=pl.BlockSpec((tm, d), lambda t, f, te, nv: (t, 0)),
        scratch_shapes=[pltpu.VMEM((tm, d), F32)],
    )
    return pl.pallas_call(
        functools.partial(_moe_kernel, nf=nf),
        grid_spec=grid_spec,
        out_shape=jax.ShapeDtypeStruct((n_rows, d), BF16),
        compiler_params=_cparams(("arbitrary", "arbitrary")),
        name="moe_experts",
    )(tile_e, n_valid, xin, w_gate, w_lin, b_gu.reshape(n_e, 1, two_ff), b_gu.reshape(n_e, 1, two_ff),
      w_down, b_down.reshape(n_e, 1, d))


def _combine_kernel(x1_ref, ys_ref, w_ref, mod_ref, o_ref):
    w = w_ref[...]
    acc = w[:, 0:1] * ys_ref[0].astype(F32)
    for k in range(1, TOP_K):
        acc = acc + w[:, k:k + 1] * ys_ref[k].astype(F32)
    o_ref[...] = x1_ref[...] + mod_ref[0][5:6] * acc


def _combine_call(x1, ysel, top_w, mod_seg, row0, n_rows, seg):
    d = x1.shape[1]
    tm = min(512, seg)
    off = row0 // tm
    return pl.pallas_call(
        _combine_kernel,
        grid=(n_rows // tm,),
        in_specs=[pl.BlockSpec((tm, d), lambda i: (i + off, 0)),
                  pl.BlockSpec((TOP_K, tm, d), lambda i: (0, i + off, 0)),
                  pl.BlockSpec((tm, TOP_K), lambda i: (i + off, 0)),
                  pl.BlockSpec((1, 6, d), lambda i: ((i + off) * tm // seg, 0, 0))],
        out_specs=pl.BlockSpec((tm, d), lambda i: (i, 0)),
        out_shape=jax.ShapeDtypeStruct((n_rows, d), F32),
        compiler_params=_cparams(("arbitrary",)),
        name="moe_combine",
    )(x1, ysel, top_w, mod_seg)


def _pack_w_in(w_in, g_q_a, g_k_a, g_q_b, g_k_b):
    o1 = Q_A
    o2 = o1 + KV_A
    o3 = o2 + KV_A
    o4 = o3 + W_B
    o5 = o4 + W_B
    o6 = o5 + W_B
    d = w_in.shape[0]
    wqa, wka, wva = w_in[:, :o1], w_in[:, o1:o2], w_in[:, o2:o3]
    wqb, wkb, wvb, wg = w_in[:, o3:o4], w_in[:, o4:o5], w_in[:, o5:o6], w_in[:, o6:]

    def dup(w):
        w4 = w.reshape(d, N_KV_A, 1, HEAD_DIM)
        return jnp.broadcast_to(w4, (d, N_KV_A, 2, HEAD_DIM)).reshape(d, N_KV_A * LANES)

    w = jnp.concatenate([wg, wqa, wqb, wkb, dup(wka), dup(wva), wvb], axis=1).astype(BF16)
    q_scale = HEAD_DIM ** -0.5
    gvec = jnp.concatenate([jnp.ones((2 * d,), F32),
                            jnp.tile(g_q_a * q_scale, N_HEADS_A), jnp.tile(g_q_b * q_scale, N_HEADS_B),
                            jnp.tile(g_k_b, N_HEADS_B), jnp.tile(g_k_a, 2 * N_KV_A),
                            jnp.ones((COL_END - COL_VA,), F32)]).reshape(1, -1).astype(F32)
    return w, gvec


def kernel(x_prompt, x_sample, c_prompt, c_sample, w_ada, b_ada, g_norm1, w_in, g_q_a, g_k_a, g_q_b, g_k_b, sink_a, t5_bias, na_bias, w_o_a, w_o_b, w_out, g_norm2, w_router, b_router, w_gu, b_gu, w_down, b_down):
    bp, sp, d = x_prompt.shape
    bs, ss, _ = x_sample.shape
    n_e = w_router.shape[-1]
    tp, ts = bp * sp, bs * ss
    t = tp + ts
    seg = math.gcd(math.gcd(sp, ss), 2048)
    assert w_ada.shape[0] == 1 and sp % seg == 0 and ss % seg == 0 and seg % (NA_HALO_ROWS * GRID_W) == 0
    assert 2 * d == w_in.shape[-1] - (Q_A + 2 * KV_A + 3 * W_B)
    base = 2 * d // LANES
    xp = x_prompt.reshape(tp, d)
    xs = x_sample.reshape(ts, d)

    n_c = bp + bs
    c_pad = jnp.zeros((-(-n_c // 16) * 16, d), F32).at[:n_c].set(jnp.concatenate([c_prompt, c_sample], axis=0))
    mod = _ada_call(c_pad, w_ada[0], b_ada[0])[:n_c].reshape(n_c, 6, d)
    seg_owner = np.concatenate([np.repeat(np.arange(bp), sp // seg), bp + np.repeat(np.arange(bs), ss // seg)])
    mod_seg = mod[seg_owner]

    h1 = _norm_mod_call(xp, xs, mod_seg, g_norm1[0], seg)

    w_slab, gvec = _pack_w_in(w_in[0], g_q_a[0], g_k_a[0], g_q_b[0], g_k_b[0])
    grp = np.arange(MXU_DIM) // HEAD_DIM
    ones_bd = jnp.asarray(grp[:, None] == grp[None, :], BF16)
    proj = _inproj_call(h1, w_slab, gvec, ones_bd, tm=min(1024, seg))

    qs = min(1024, seg)
    sb_tok = np.arange(t // qs) * qs
    seq_start = np.where(sb_tok < tp, sb_tok // sp * sp, tp + (sb_tok - tp) // ss * ss)
    seq_len = np.where(sb_tok < tp, sp, ss)
    flags = np.stack([sb_tok != seq_start, sb_tok + qs != seq_start + seq_len], axis=1).astype(np.int32).reshape(-1)
    attn_a = _win_attn_call(proj, _window_bias(t5_bias), sink_a[0].astype(F32), jnp.asarray(flags), qs, base)

    cfg_ids, kstarts, cfg_blocks = _na_plan([sp] * bp + [ss] * bs, seg)
    bias_cfg = _na_bias_tables(na_bias[0], cfg_blocks)
    attn_b = _na_attn_call(proj, bias_cfg, jnp.asarray(cfg_ids), jnp.asarray(kstarts), seg, base)

    wr = jnp.zeros((d, LANES), F32).at[:, :n_e].set(w_router[0])
    wr_hi = wr.astype(BF16)
    wr_lo = (wr - wr_hi.astype(F32)).astype(BF16)
    br = jnp.zeros((1, LANES), F32).at[0, :n_e].set(b_router[0])
    x1, h2, logits = _outproj_call(attn_a, attn_b, proj, xp, xs, mod_seg, w_o_a[0].astype(BF16),
                                   w_o_b[0].astype(BF16), w_out[0].astype(BF16), g_norm2[0], wr_hi, wr_lo, br, seg)

    tm_e = min(1024, seg)
    top_logits, top_idx = lax.top_k(logits[:, :n_e], TOP_K)
    top_w = jax.nn.softmax(top_logits, axis=-1)
    n_assign = t * TOP_K
    e_flat = top_idx.reshape(-1)
    onehot = (e_flat[:, None] == jnp.arange(n_e, dtype=e_flat.dtype)[None, :]).astype(jnp.int32)
    csum = jnp.cumsum(onehot, axis=0)
    rank = jnp.take_along_axis(csum, e_flat[:, None], axis=1)[:, 0] - 1
    counts = csum[-1]
    padded = (counts + tm_e - 1) // tm_e * tm_e
    pend = jnp.cumsum(padded)
    dest = (pend - padded)[e_flat] + rank
    n_tiles = -(-n_assign // tm_e) + n_e
    n_rows = n_tiles * tm_e
    tile_e = jnp.minimum(jnp.searchsorted(pend, jnp.arange(n_tiles) * tm_e, side='right'), n_e - 1).astype(jnp.int32)
    n_valid = (pend[-1:] // tm_e).astype(jnp.int32)
    tok_flat = jnp.arange(n_assign, dtype=jnp.int32) // TOP_K
    _, tok_sorted = lax.sort((e_flat, tok_flat), num_keys=1, is_stable=True)
    slot = jnp.arange(n_rows, dtype=jnp.int32)
    slot_e = jnp.repeat(tile_e, tm_e)
    idx_in_e = slot - (pend - padded)[slot_e]
    src = jnp.minimum((jnp.cumsum(counts) - counts)[slot_e] + idx_in_e, n_assign - 1)
    buf_tok = jnp.where(idx_in_e < counts[slot_e], tok_sorted[src], t)
    xin = jnp.concatenate([h2, jnp.zeros((1, d), BF16)], axis=0)[buf_tok]

    d_ff = w_down.shape[-2]
    w_gate = _cast_call(w_gu[0], 0, d_ff, 1024, "cast_gate")
    w_lin = _cast_call(w_gu[0], 1, d_ff, 2048, "cast_lin")
    w_dn = _cast_call(w_down[0], 0, d, 1024, "cast_down")
    yb = _moe_call(xin, tile_e, n_valid, w_gate, w_lin, b_gu[0], w_dn, b_down[0], tm_e)

    ysel = yb[dest.reshape(t, TOP_K).T]
    y_prompt = _combine_call(x1, ysel, top_w, mod_seg, 0, tp, seg)
    y_sample = _combine_call(x1, ysel, top_w, mod_seg, tp, ts, seg)
    return (y_prompt.reshape(bp, sp, d), y_sample.reshape(bs, ss, d))
```

```python
import functools
import math

import numpy as np
import jax
import jax.numpy as jnp
from jax import lax
from jax.experimental import pallas as pl
from jax.experimental.pallas import tpu as pltpu

F32 = jnp.float32
BF16 = jnp.bfloat16

HEAD_DIM = 64
N_HEADS_A = 16
N_KV_A = 4
GROUP_A = N_HEADS_A // N_KV_A
N_HEADS_B = 16
WINDOW = 128
ATTN_BLOCK = 128
T5_BUCKETS = 32
T5_MAX_DIST = 128
GRID_W = 64
NA_ROWS = 8
NA_COLS = 16
TOP_K = 4
SWIGLU_LIMIT = 7.0
SWIGLU_ALPHA = 1.702
RMS_EPS = 1e-6

Q_A = N_HEADS_A * HEAD_DIM
KV_A = N_KV_A * HEAD_DIM
W_B = N_HEADS_B * HEAD_DIM

LANES = 128
MXU_DIM = 256
VMEM_LIMIT_BYTES = 56 * 1024 * 1024
MASKED = -1e30

N_HEAD_PAIRS = Q_A // LANES
COL_QA = 0
COL_QB = COL_QA + Q_A
COL_KB = COL_QB + W_B
COL_KA = COL_KB + W_B
COL_VA = COL_KA + N_KV_A * LANES
COL_VB = COL_VA + N_KV_A * LANES
COL_END = COL_VB + W_B
NA_GROUP_ROWS = 2
NA_KEY_ROWS = NA_GROUP_ROWS + NA_ROWS - 1
NA_HALO_ROWS = NA_ROWS


def _sigmoid(x):
    return 1.0 / (1.0 + jnp.exp(-x))


def _cparams(sem):
    return pltpu.CompilerParams(dimension_semantics=sem, vmem_limit_bytes=VMEM_LIMIT_BYTES)


def _ada_kernel(c_ref, w_ref, b_ref, o_ref):
    c = c_ref[...]
    a = (c * _sigmoid(c)).astype(BF16)
    o_ref[...] = jnp.dot(a, w_ref[...].astype(BF16), preferred_element_type=F32) + b_ref[...]


def _ada_call(c_pad, w_ada, b_ada):
    rows, d = c_pad.shape
    n = w_ada.shape[1]
    tn = min(n, 1024)
    return pl.pallas_call(
        _ada_kernel,
        grid=(n // tn,),
        in_specs=[pl.BlockSpec((rows, d), lambda j: (0, 0)),
                  pl.BlockSpec((d, tn), lambda j: (0, j)),
                  pl.BlockSpec((1, tn), lambda j: (0, j))],
        out_specs=pl.BlockSpec((rows, tn), lambda j: (0, j)),
        out_shape=jax.ShapeDtypeStruct((rows, n), F32),
        compiler_params=_cparams(("arbitrary",)),
        name="ada_ln",
    )(c_pad, w_ada, b_ada.reshape(1, n))


def _two_group_specs(tm, d, n_prompt_tiles, n_sample_tiles):
    xp = pl.BlockSpec((tm, d), lambda i, *_: (jnp.minimum(i, n_prompt_tiles - 1), 0))
    xs = pl.BlockSpec((tm, d), lambda i, *_: (jnp.clip(i - n_prompt_tiles, 0, n_sample_tiles - 1), 0))
    return xp, xs


def _norm_mod_kernel(xp_ref, xs_ref, mod_ref, g_ref, o_ref, *, n_prompt_tiles):
    i = pl.program_id(0)
    x = jnp.where(i < n_prompt_tiles, xp_ref[...], xs_ref[...])
    y = x * lax.rsqrt(jnp.mean(x * x, axis=-1, keepdims=True) + RMS_EPS) * g_ref[...]
    m = mod_ref[0]
    o_ref[...] = (y * (1.0 + m[1:2]) + m[0:1]).astype(BF16)


def _norm_mod_call(xp, xs, mod_seg, g, seg):
    d = xp.shape[1]
    tm = min(512, seg)
    npt, nst = xp.shape[0] // tm, xs.shape[0] // tm
    xp_spec, xs_spec = _two_group_specs(tm, d, npt, nst)
    return pl.pallas_call(
        functools.partial(_norm_mod_kernel, n_prompt_tiles=npt),
        grid=(npt + nst,),
        in_specs=[xp_spec, xs_spec,
                  pl.BlockSpec((1, 6, d), lambda i: (i * tm // seg, 0, 0)),
                  pl.BlockSpec((1, d), lambda i: (0, 0))],
        out_specs=pl.BlockSpec((tm, d), lambda i: (i, 0)),
        out_shape=jax.ShapeDtypeStruct((xp.shape[0] + xs.shape[0], d), BF16),
        compiler_params=_cparams(("arbitrary",)),
        name="norm1_mod",
    )(xp, xs, mod_seg, g.reshape(1, d))


def _group_rms(y, g, ones_ref):
    sq = (y * y).astype(BF16)
    parts = []
    for c in range(y.shape[1] // MXU_DIM):
        parts.append(jnp.dot(sq[:, c * MXU_DIM:(c + 1) * MXU_DIM], ones_ref[...], preferred_element_type=F32))
    ss = jnp.concatenate(parts, axis=1) if len(parts) > 1 else parts[0]
    return y * lax.rsqrt(ss * (1.0 / HEAD_DIM) + RMS_EPS) * g


def _inproj_kernel(h_ref, w_ref, g_ref, ones_ref, o_ref, *, tn, gate_tiles):
    j = pl.program_id(1)
    y = jnp.dot(h_ref[...], w_ref[...], preferred_element_type=F32)
    half = tn // 2
    mixed_tile = gate_tiles + COL_KA // tn
    plain_tile = gate_tiles + COL_VB // tn

    @pl.when(j < gate_tiles)
    def _():
        o_ref[...] = (0.5 * jnp.tanh(0.5 * y) + 0.5).astype(BF16)

    @pl.when((j >= gate_tiles) & (j < mixed_tile))
    def _():
        o_ref[...] = _group_rms(y, g_ref[...], ones_ref).astype(BF16)

    @pl.when(j == mixed_tile)
    def _():
        o_ref[:, :half] = _group_rms(y[:, :half], g_ref[:, :half], ones_ref).astype(BF16)
        o_ref[:, half:] = y[:, half:].astype(BF16)

    @pl.when(j == plain_tile)
    def _():
        o_ref[...] = y.astype(BF16)


def _inproj_call(h, w, gvec, ones_bd, tm):
    t, d = h.shape
    pw = w.shape[1]
    tn = 1024
    assert (2 * d) % tn == 0 and COL_KA % tn == 0 and COL_VA - COL_KA == tn // 2 and COL_VB % tn == 0
    return pl.pallas_call(
        functools.partial(_inproj_kernel, tn=tn, gate_tiles=2 * d // tn),
        grid=(t // tm, pw // tn),
        in_specs=[pl.BlockSpec((tm, d), lambda i, j: (i, 0)),
                  pl.BlockSpec((d, tn), lambda i, j: (0, j)),
                  pl.BlockSpec((1, tn), lambda i, j: (0, j)),
                  pl.BlockSpec((MXU_DIM, MXU_DIM), lambda i, j: (0, 0))],
        out_specs=pl.BlockSpec((tm, tn), lambda i, j: (i, j)),
        out_shape=jax.ShapeDtypeStruct((t, pw), BF16),
        compiler_params=_cparams(("arbitrary", "arbitrary")),
        name="in_proj",
    )(h, w, gvec, ones_bd)


def _aligned(x, m):
    return x if isinstance(x, int) else pl.multiple_of(x, m)


def _three_stage_pipeline(n, stage_a, stage_b, stage_c):
    assert n >= 6 and n % 2 == 0
    pairs = n // 2

    def trip(t, run_a=True, run_b=True, run_c=True):
        for k in range(2):
            if run_c:
                stage_c(2 * (t - 2) + k, k)
        for k in range(2):
            if run_b:
                stage_b(2 * (t - 1) + k, k)
        for k in range(2):
            if run_a:
                stage_a(2 * t + k, k)

    trip(0, run_b=False, run_c=False)
    trip(1, run_c=False)

    def body(t, carry):
        trip(t)
        return carry

    lax.fori_loop(2, pairs, body, 0)
    trip(pairs, run_a=False)
    trip(pairs + 1, run_a=False, run_b=False)


def _win_attn_kernel(flags_ref, q_ref, kp_ref, kc_ref, kn_ref, vp_ref, vc_ref, vn_ref, bias_ref, sink_ref,
                     o_ref, kcat, vcat, s_scr, e_scr, r_scr, *, qs, nb):
    p = pl.program_id(0)
    sb = pl.program_id(1)
    blk = ATTN_BLOCK
    kcat[0:blk] = kp_ref[...]
    kcat[blk:blk + qs] = kc_ref[...]
    kcat[blk + qs:] = kn_ref[...]
    vcat[0:blk] = vp_ref[...]
    vcat[blk:blk + qs] = vc_ref[...]
    vcat[blk + qs:] = vn_ref[...]
    prev_ok = flags_ref[2 * sb]
    next_ok = flags_ref[2 * sb + 1]
    low_half = lax.broadcasted_iota(jnp.int32, (blk, LANES), 1) < HEAD_DIM
    row = lax.broadcasted_iota(jnp.int32, (2 * blk, 1), 0)
    sinkcol = jnp.where(row < blk, sink_ref[2 * p], sink_ref[2 * p + 1])
    kl = blk + 2 * WINDOW

    def scores(b, slot):
        r0 = _aligned(b * blk, blk)
        q = q_ref[pl.ds(r0, blk), :]
        zero = jnp.zeros_like(q)
        q2 = jnp.concatenate([jnp.where(low_half, q, zero), jnp.where(low_half, zero, q)], axis=0)
        kw = kcat[pl.ds(r0, kl), :]
        s_scr[slot] = lax.dot_general(q2, kw, (((1,), (1,)), ((), ())), preferred_element_type=F32)

    def softmax(b, slot):
        variant = jnp.where((b == 0) & (prev_ok == 0), 1, jnp.where((b == nb - 1) & (next_ok == 0), 2, 0))
        s = s_scr[slot] + bias_ref[variant].reshape(2 * blk, kl)
        m = jnp.maximum(jnp.max(s, axis=-1, keepdims=True), sinkcol)
        e = jnp.exp(s - m)
        denom = jnp.sum(e, axis=-1, keepdims=True) + jnp.exp(sinkcol - m)
        e_scr[slot] = e.astype(BF16)
        r_scr[slot] = 1.0 / denom

    def values(b, slot):
        r0 = _aligned(b * blk, blk)
        vw = vcat[pl.ds(r0, kl), :]
        o2 = jnp.dot(e_scr[slot], vw, preferred_element_type=F32) * r_scr[slot]
        o_ref[pl.ds(r0, blk), :] = jnp.where(low_half, o2[:blk], o2[blk:]).astype(BF16)

    _three_stage_pipeline(nb, scores, softmax, values)


def _win_attn_call(proj, bias3, sink, flags, qs, base):
    t = proj.shape[0]
    blk = ATTN_BLOCK
    nb = qs // blk
    n_super = t // qs
    n_blk_rows = t // blk
    kl = blk + 2 * WINDOW
    ka0, va0 = base + COL_KA // LANES, base + COL_VA // LANES

    def cur(col0):
        return pl.BlockSpec((qs, LANES), lambda p, s, f: (s, col0 + p // 2))

    def prev(col0):
        return pl.BlockSpec((blk, LANES), lambda p, s, f: (jnp.maximum(s * nb - 1, 0), col0 + p // 2))

    def nxt(col0):
        return pl.BlockSpec((blk, LANES), lambda p, s, f: (jnp.minimum((s + 1) * nb, n_blk_rows - 1), col0 + p // 2))

    grid_spec = pltpu.PrefetchScalarGridSpec(
        num_scalar_prefetch=1,
        grid=(N_HEAD_PAIRS, n_super),
        in_specs=[pl.BlockSpec((qs, LANES), lambda p, s, f: (s, base + COL_QA // LANES + p)),
                  prev(ka0), cur(ka0), nxt(ka0), prev(va0), cur(va0), nxt(va0),
                  pl.BlockSpec((3, 2, blk, kl), lambda p, s, f: (0, p, 0, 0)),
                  pl.BlockSpec(memory_space=pltpu.SMEM)],
        out_specs=pl.BlockSpec((qs, LANES), lambda p, s, f: (s, p)),
        scratch_shapes=[pltpu.VMEM((qs + 2 * blk, LANES), BF16), pltpu.VMEM((qs + 2 * blk, LANES), BF16),
                        pltpu.VMEM((2, 2 * blk, kl), F32), pltpu.VMEM((2, 2 * blk, kl), BF16),
                        pltpu.VMEM((2, 2 * blk, 1), F32)],
    )
    return pl.pallas_call(
        functools.partial(_win_attn_kernel, qs=qs, nb=nb),
        grid_spec=grid_spec,
        out_shape=jax.ShapeDtypeStruct((t, Q_A), BF16),
        compiler_params=_cparams(("arbitrary", "arbitrary")),
        name="win_attn",
    )(flags, proj, proj, proj, proj, proj, proj, proj, bias3, sink)


def _t5_bucket(rel):
    nb = T5_BUCKETS // 2
    max_exact = nb // 2
    ret = jnp.where(rel > 0, nb, 0)
    n = jnp.abs(rel)
    nf = jnp.maximum(n, 1).astype(jnp.float32)
    large = max_exact + (jnp.log(nf / max_exact) / math.log(T5_MAX_DIST / max_exact) * (nb - max_exact)).astype(jnp.int32)
    large = jnp.minimum(large, nb - 1)
    return ret + jnp.where(n < max_exact, n, large)


def _toeplitz(v, n_rows, n_cols):
    p = v.shape[-1]
    assert n_cols <= p - 1
    flat = jnp.tile(v, (1,) * (v.ndim - 1) + (n_rows,))[..., :n_rows * (p - 1)]
    return flat.reshape(v.shape[:-1] + (n_rows, p - 1))[..., :n_cols]


def _window_bias(t5_bias):
    blk, kl = ATTN_BLOCK, ATTN_BLOCK + 2 * WINDOW
    period = blk + kl - 1
    m = np.arange(period)
    rel_of_m = np.where(m < kl, m, m - period) - WINDOW
    per_rel = t5_bias[_t5_bucket(jnp.asarray(rel_of_m))].astype(F32).T
    bias = _toeplitz(per_rel, blk, kl)
    col = np.arange(kl)[None, :]
    band = np.abs(col - WINDOW - np.arange(blk)[:, None]) <= WINDOW
    keep = np.stack([band, band & (col >= WINDOW), band & (col < WINDOW + blk)])
    return jnp.where(keep[:, None], bias[None], MASKED)


def _na_plan(seq_lens, seg):
    qr, nkr = NA_GROUP_ROWS, NA_KEY_ROWS
    configs, cfg_ids, kstarts = {}, [], []
    tok = 0
    for s_len in seq_lens:
        rows = s_len // GRID_W
        assert rows >= nkr and rows % qr == 0
        for r in range(0, rows, qr):
            us = min(max(r - NA_ROWS // 2, 0), rows - nkr)
            rel = tuple(min(max(r + q - NA_ROWS // 2, 0), rows - NA_ROWS) - us for q in range(qr))
            key = (r - us, rel)
            cfg_ids.append(configs.setdefault(key, len(configs)))
            g_tok = tok + r * GRID_W
            seg_tok0 = (g_tok // seg) * seg
            kstarts.append(tok + us * GRID_W - seg_tok0 + NA_HALO_ROWS * GRID_W)
        tok += s_len
    blocks = [None] * len(configs)
    for (r_us, rel), c in configs.items():
        blocks[c] = [[kr - r_us - q + NA_ROWS - 1 if rel[q] <= kr < rel[q] + NA_ROWS else None
                      for kr in range(nkr)] for q in range(qr)]
    return np.asarray(cfg_ids, np.int32), np.asarray(kstarts, np.int32), blocks


def _na_bias_tables(na_bias, blocks):
    period = 2 * GRID_W - 1
    m = np.arange(period)
    dcol = np.where(m < GRID_W, m, m - period)
    per_dcol = na_bias.astype(F32)[:, :, np.clip(dcol + NA_COLS - 1, 0, 2 * NA_COLS - 2)]
    col_tab = _toeplitz(per_dcol, GRID_W, GRID_W)
    q_col, k_col = np.arange(GRID_W)[:, None], np.arange(GRID_W)[None, :]
    win_start = np.clip(q_col - NA_COLS // 2, 0, GRID_W - NA_COLS)
    col_ok = (k_col >= win_start) & (k_col < win_start + NA_COLS)
    col_tab = jnp.where(col_ok, col_tab, MASKED)
    masked_blk = jnp.full(col_tab.shape[:1] + col_tab.shape[2:], MASKED, F32)
    cfgs = []
    for cfg in blocks:
        rows = [jnp.concatenate([masked_blk if dr is None else col_tab[:, dr] for dr in qrow], axis=-1) for qrow in cfg]
        cfgs.append(jnp.concatenate(rows, axis=-2))
    return jnp.stack(cfgs)


def _na_attn_kernel(cfg_ref, ks_ref, q_ref, kp_ref, kc_ref, kn_ref, vp_ref, vc_ref, vn_ref, bias_ref,
                    o_ref, kcat, vcat, s_scr, e_scr, r_scr, *, seg):
    s_id = pl.program_id(1)
    halo = NA_HALO_ROWS * GRID_W
    m, nk = NA_GROUP_ROWS * GRID_W, NA_KEY_ROWS * GRID_W
    n_groups = seg // m
    kcat[0:halo] = kp_ref[...]
    kcat[halo:halo + seg] = kc_ref[...]
    kcat[halo + seg:] = kn_ref[...]
    vcat[0:halo] = vp_ref[...]
    vcat[halo:halo + seg] = vc_ref[...]
    vcat[halo + seg:] = vn_ref[...]
    low_half = lax.broadcasted_iota(jnp.int32, (m, LANES), 1) < HEAD_DIM

    def scores(g, slot):
        ks = pl.multiple_of(ks_ref[s_id * n_groups + g], GRID_W)
        r0 = _aligned(g * m, m)
        q = q_ref[pl.ds(r0, m), :]
        zero = jnp.zeros_like(q)
        q2 = jnp.concatenate([jnp.where(low_half, q, zero), jnp.where(low_half, zero, q)], axis=0)
        kw = kcat[pl.ds(ks, nk), :]
        s_scr[slot] = lax.dot_general(q2, kw, (((1,), (1,)), ((), ())), preferred_element_type=F32)

    def softmax(g, slot):
        cfg = cfg_ref[s_id * n_groups + g]
        s = s_scr[slot] + bias_ref[cfg].reshape(2 * m, nk)
        mx = jnp.max(s, axis=-1, keepdims=True)
        e = jnp.exp(s - mx)
        e_scr[slot] = e.astype(BF16)
        r_scr[slot] = 1.0 / jnp.sum(e, axis=-1, keepdims=True)

    def values(g, slot):
        ks = pl.multiple_of(ks_ref[s_id * n_groups + g], GRID_W)
        r0 = _aligned(g * m, m)
        vw = vcat[pl.ds(ks, nk), :]
        o2 = jnp.dot(e_scr[slot], vw, preferred_element_type=F32) * r_scr[slot]
        o_ref[pl.ds(r0, m), :] = jnp.where(low_half, o2[:m], o2[m:]).astype(BF16)

    _three_stage_pipeline(n_groups, scores, softmax, values)


def _na_attn_call(proj, bias_cfg, cfg_ids, kstarts, seg, base):
    t = proj.shape[0]
    halo = NA_HALO_ROWS * GRID_W
    n_seg = t // seg
    per = seg // halo
    n_halo_blocks = t // halo
    ncfg = bias_cfg.shape[0]
    m, nk = NA_GROUP_ROWS * GRID_W, NA_KEY_ROWS * GRID_W
    kb0, vb0 = base + COL_KB // LANES, base + COL_VB // LANES

    def cur(col0):
        return pl.BlockSpec((seg, LANES), lambda p, s, c, k: (s, col0 + p))

    def prev(col0):
        return pl.BlockSpec((halo, LANES), lambda p, s, c, k: (jnp.maximum(s * per - 1, 0), col0 + p))

    def nxt(col0):
        return pl.BlockSpec((halo, LANES), lambda p, s, c, k: (jnp.minimum((s + 1) * per, n_halo_blocks - 1), col0 + p))

    grid_spec = pltpu.PrefetchScalarGridSpec(
        num_scalar_prefetch=2,
        grid=(N_HEAD_PAIRS, n_seg),
        in_specs=[cur(base + COL_QB // LANES), prev(kb0), cur(kb0), nxt(kb0), prev(vb0), cur(vb0), nxt(vb0),
                  pl.BlockSpec((ncfg, 2, m, nk), lambda p, s, c, k: (0, p, 0, 0))],
        out_specs=pl.BlockSpec((seg, LANES), lambda p, s, c, k: (s, p)),
        scratch_shapes=[pltpu.VMEM((seg + 2 * halo, LANES), BF16), pltpu.VMEM((seg + 2 * halo, LANES), BF16),
                        pltpu.VMEM((2, 2 * m, nk), F32), pltpu.VMEM((2, 2 * m, nk), BF16),
                        pltpu.VMEM((2, 2 * m, 1), F32)],
    )
    return pl.pallas_call(
        functools.partial(_na_attn_kernel, seg=seg),
        grid_spec=grid_spec,
        out_shape=jax.ShapeDtypeStruct((t, W_B), BF16),
        compiler_params=_cparams(("arbitrary", "arbitrary")),
        name="na_attn",
    )(cfg_ids, kstarts, proj, proj, proj, proj, proj, proj, proj, bias_cfg)


def _outproj_kernel(aa_ref, ab_ref, ga_ref, gb_ref, xp_ref, xs_ref, mod_ref, woa_ref, wob_ref, wout_ref,
                    g2_ref, wrh_ref, wrl_ref, br_ref, x1_ref, h2_ref, lg_ref, *, n_prompt_tiles):
    i = pl.program_id(0)
    ya = jnp.dot(aa_ref[...], woa_ref[...], preferred_element_type=F32)
    yb = jnp.dot(ab_ref[...], wob_ref[...], preferred_element_type=F32)
    merged = ga_ref[...].astype(F32) * ya + gb_ref[...].astype(F32) * yb
    z = jnp.dot(merged.astype(BF16), wout_ref[...], preferred_element_type=F32)
    x = jnp.where(i < n_prompt_tiles, xp_ref[...], xs_ref[...])
    m = mod_ref[0]
    x1 = x + m[2:3] * z
    x1_ref[...] = x1
    y = x1 * lax.rsqrt(jnp.mean(x1 * x1, axis=-1, keepdims=True) + RMS_EPS) * g2_ref[...]
    h2 = y * (1.0 + m[4:5]) + m[3:4]
    h2_hi = h2.astype(BF16)
    h2_ref[...] = h2_hi
    h2_lo = (h2 - h2_hi.astype(F32)).astype(BF16)
    lg = jnp.dot(h2_hi, wrh_ref[...], preferred_element_type=F32)
    lg = lg + jnp.dot(h2_lo, wrh_ref[...], preferred_element_type=F32)
    lg = lg + jnp.dot(h2_hi, wrl_ref[...], preferred_element_type=F32)
    lg_ref[...] = lg + br_ref[...]


def _outproj_call(attn_a, attn_b, proj, xp, xs, mod_seg, woa, wob, wout, g2, wr_hi, wr_lo, br, seg):
    t = attn_a.shape[0]
    d = xp.shape[1]
    tm = min(256, seg)
    npt, nst = xp.shape[0] // tm, xs.shape[0] // tm
    xp_spec, xs_spec = _two_group_specs(tm, d, npt, nst)
    def const(shape):
        return pl.BlockSpec(shape, lambda i: (0,) * len(shape), pipeline_mode=pl.Buffered(1))

    return pl.pallas_call(
        functools.partial(_outproj_kernel, n_prompt_tiles=npt),
        grid=(npt + nst,),
        in_specs=[pl.BlockSpec((tm, Q_A), lambda i: (i, 0)),
                  pl.BlockSpec((tm, W_B), lambda i: (i, 0)),
                  pl.BlockSpec((tm, d), lambda i: (i, 0)),
                  pl.BlockSpec((tm, d), lambda i: (i, 1)),
                  xp_spec, xs_spec,
                  pl.BlockSpec((1, 6, d), lambda i: (i * tm // seg, 0, 0)),
                  const((Q_A, d)), const((W_B, d)), const((d, d)), const((1, d)),
                  const((d, LANES)), const((d, LANES)), const((1, LANES))],
        out_specs=[pl.BlockSpec((tm, d), lambda i: (i, 0)),
                   pl.BlockSpec((tm, d), lambda i: (i, 0)),
                   pl.BlockSpec((tm, LANES), lambda i: (i, 0))],
        out_shape=[jax.ShapeDtypeStruct((t, d), F32),
                   jax.ShapeDtypeStruct((t, d), BF16),
                   jax.ShapeDtypeStruct((t, LANES), F32)],
        compiler_params=_cparams(("arbitrary",)),
        name="out_proj",
    )(attn_a, attn_b, proj, proj, xp, xs, mod_seg, woa, wob, wout, g2.reshape(1, d), wr_hi, wr_lo, br)


def _cast_kernel(g_ref, l_ref, d_ref, og_ref, ol_ref, od_ref):
    og_ref[...] = g_ref[...].astype(BF16)
    ol_ref[...] = l_ref[...].astype(BF16)
    od_ref[...] = d_ref[...].astype(BF16)


def _cast_experts_call(w_gu, w_down):
    n_e, d, two_ff = w_gu.shape
    d_ff = two_ff // 2
    assert w_down.shape == (n_e, d_ff, d) and d_ff == d
    tr = min(512, d)
    rows = n_e * d
    return pl.pallas_call(
        _cast_kernel,
        grid=(rows // tr,),
        in_specs=[pl.BlockSpec((tr, d_ff), lambda i: (i, 0)),
                  pl.BlockSpec((tr, d_ff), lambda i: (i, 1)),
                  pl.BlockSpec((tr, d), lambda i: (i, 0))],
        out_specs=[pl.BlockSpec((tr, d_ff), lambda i: (i, 0)),
                   pl.BlockSpec((tr, d_ff), lambda i: (i, 0)),
                   pl.BlockSpec((tr, d), lambda i: (i, 0))],
        out_shape=[jax.ShapeDtypeStruct((rows, d_ff), BF16),
                   jax.ShapeDtypeStruct((rows, d_ff), BF16),
                   jax.ShapeDtypeStruct((rows, d), BF16)],
        compiler_params=_cparams(("arbitrary",)),
        name="cast_experts",
    )(w_gu.reshape(rows, two_ff), w_gu.reshape(rows, two_ff), w_down.reshape(rows, d))


def _moe_kernel(te_ref, nv_ref, x_ref, wg_ref, wl_ref, bg_ref, bl_ref, wd_ref, bd_ref, o_ref, acc_ref, *, nf):
    t = pl.program_id(0)
    f = pl.program_id(1)
    live = t < nv_ref[0]

    @pl.when((t == 0) & (f == 0))
    def _():
        acc_ref[...] = jnp.zeros_like(acc_ref)

    @pl.when(live)
    def _():
        x = x_ref[...]
        gate = jnp.dot(x, wg_ref[...], preferred_element_type=F32) + bg_ref[0]
        lin = jnp.dot(x, wl_ref[...], preferred_element_type=F32) + bl_ref[0]
        gate = jnp.minimum(gate, SWIGLU_LIMIT)
        lin = jnp.clip(lin, -SWIGLU_LIMIT, SWIGLU_LIMIT)
        act = gate * _sigmoid(SWIGLU_ALPHA * gate) * (lin + 1.0)
        part = jnp.dot(act.astype(BF16), wd_ref[...], preferred_element_type=F32)
        total = part + jnp.where(f == 0, bd_ref[0], acc_ref[...])
        acc_ref[...] = total
        o_ref[...] = total.astype(BF16)

    @pl.when(jnp.logical_not(live) & (f == 0))
    def _():
        o_ref[...] = jnp.zeros_like(o_ref)


def _moe_call(xin, tile_e, n_valid, w_gate, w_lin, b_gu, w_down, b_down, tm):
    n_rows, d = xin.shape
    n_e, two_ff = b_gu.shape
    d_ff = two_ff // 2
    tf = min(512, d_ff)
    nf = d_ff // tf
    n_tiles = n_rows // tm

    def fidx(t, f, nv):
        return jnp.where(t < nv[0], f, nf - 1)

    grid_spec = pltpu.PrefetchScalarGridSpec(
        num_scalar_prefetch=2,
        grid=(n_tiles, nf),
        in_specs=[pl.BlockSpec((tm, d), lambda t, f, te, nv: (t, 0)),
                  pl.BlockSpec((d, tf), lambda t, f, te, nv: (te[t], fidx(t, f, nv))),
                  pl.BlockSpec((d, tf), lambda t, f, te, nv: (te[t], fidx(t, f, nv))),
                  pl.BlockSpec((1, 1, tf), lambda t, f, te, nv: (te[t], 0, fidx(t, f, nv))),
                  pl.BlockSpec((1, 1, tf), lambda t, f, te, nv: (te[t], 0, fidx(t, f, nv) + nf)),
                  pl.BlockSpec((tf, d), lambda t, f, te, nv: (te[t] * nf + fidx(t, f, nv), 0)),
                  pl.BlockSpec((1, 1, d), lambda t, f, te, nv: (te[t], 0, 0))],
        out_specs=pl.BlockSpec((tm, d), lambda t, f, te, nv: (t, 0)),
        scratch_shapes=[pltpu.VMEM((tm, d), F32)],
    )
    return pl.pallas_call(
        functools.partial(_moe_kernel, nf=nf),
        grid_spec=grid_spec,
        out_shape=jax.ShapeDtypeStruct((n_rows, d), BF16),
        compiler_params=_cparams(("arbitrary", "arbitrary")),
        name="moe_experts",
    )(tile_e, n_valid, xin, w_gate, w_lin, b_gu.reshape(n_e, 1, two_ff), b_gu.reshape(n_e, 1, two_ff),
      w_down, b_down.reshape(n_e, 1, d))


def _combine_kernel(x1_ref, ys_ref, w_ref, mod_ref, o_ref):
    w = w_ref[...]
    acc = w[:, 0:1] * ys_ref[0].astype(F32)
    for k in range(1, TOP_K):
        acc = acc + w[:, k:k + 1] * ys_ref[k].astype(F32)
    o_ref[...] = x1_ref[...] + mod_ref[0][5:6] * acc


def _combine_call(x1, ysel, top_w, mod_seg, row0, n_rows, seg):
    d = x1.shape[1]
    tm = min(512, seg)
    off = row0 // tm
    return pl.pallas_call(
        _combine_kernel,
        grid=(n_rows // tm,),
        in_specs=[pl.BlockSpec((tm, d), lambda i: (i + off, 0)),
                  pl.BlockSpec((TOP_K, tm, d), lambda i: (0, i + off, 0)),
                  pl.BlockSpec((tm, TOP_K), lambda i: (i + off, 0)),
                  pl.BlockSpec((1, 6, d), lambda i: ((i + off) * tm // seg, 0, 0))],
        out_specs=pl.BlockSpec((tm, d), lambda i: (i, 0)),
        out_shape=jax.ShapeDtypeStruct((n_rows, d), F32),
        compiler_params=_cparams(("arbitrary",)),
        name="moe_combine",
    )(x1, ysel, top_w, mod_seg)


def _pack_w_in(w_in, g_q_a, g_k_a, g_q_b, g_k_b):
    o1 = Q_A
    o2 = o1 + KV_A
    o3 = o2 + KV_A
    o4 = o3 + W_B
    o5 = o4 + W_B
    o6 = o5 + W_B
    d = w_in.shape[0]
    wqa, wka, wva = w_in[:, :o1], w_in[:, o1:o2], w_in[:, o2:o3]
    wqb, wkb, wvb, wg = w_in[:, o3:o4], w_in[:, o4:o5], w_in[:, o5:o6], w_in[:, o6:]

    def dup(w):
        w4 = w.reshape(d, N_KV_A, 1, HEAD_DIM)
        return jnp.broadcast_to(w4, (d, N_KV_A, 2, HEAD_DIM)).reshape(d, N_KV_A * LANES)

    w = jnp.concatenate([wg, wqa, wqb, wkb, dup(wka), dup(wva), wvb], axis=1).astype(BF16)
    q_scale = HEAD_DIM ** -0.5
    gvec = jnp.concatenate([jnp.ones((2 * d,), F32),
                            jnp.tile(g_q_a * q_scale, N_HEADS_A), jnp.tile(g_q_b * q_scale, N_HEADS_B),
                            jnp.tile(g_k_b, N_HEADS_B), jnp.tile(g_k_a, 2 * N_KV_A),
                            jnp.ones((COL_END - COL_VA,), F32)]).reshape(1, -1).astype(F32)
    return w, gvec


def kernel(x_prompt, x_sample, c_prompt, c_sample, w_ada, b_ada, g_norm1, w_in, g_q_a, g_k_a, g_q_b, g_k_b, sink_a, t5_bias, na_bias, w_o_a, w_o_b, w_out, g_norm2, w_router, b_router, w_gu, b_gu, w_down, b_down):
    bp, sp, d = x_prompt.shape
    bs, ss, _ = x_sample.shape
    n_e = w_router.shape[-1]
    tp, ts = bp * sp, bs * ss
    t = tp + ts
    seg = math.gcd(math.gcd(sp, ss), 2048)
    assert w_ada.shape[0] == 1 and sp % seg == 0 and ss % seg == 0 and seg % (NA_HALO_ROWS * GRID_W) == 0
    assert 2 * d == w_in.shape[-1] - (Q_A + 2 * KV_A + 3 * W_B)
    base = 2 * d // LANES
    xp = x_prompt.reshape(tp, d)
    xs = x_sample.reshape(ts, d)

    n_c = bp + bs
    c_pad = jnp.zeros((-(-n_c // 16) * 16, d), F32).at[:n_c].set(jnp.concatenate([c_prompt, c_sample], axis=0))
    mod = _ada_call(c_pad, w_ada[0], b_ada[0])[:n_c].reshape(n_c, 6, d)
    seg_owner = np.concatenate([np.repeat(np.arange(bp), sp // seg), bp + np.repeat(np.arange(bs), ss // seg)])
    mod_seg = mod[seg_owner]

    h1 = _norm_mod_call(xp, xs, mod_seg, g_norm1[0], seg)

    w_slab, gvec = _pack_w_in(w_in[0], g_q_a[0], g_k_a[0], g_q_b[0], g_k_b[0])
    grp = np.arange(MXU_DIM) // HEAD_DIM
    ones_bd = jnp.asarray(grp[:, None] == grp[None, :], BF16)
    proj = _inproj_call(h1, w_slab, gvec, ones_bd, tm=min(1024, seg))

    qs = min(2048, seg)
    sb_tok = np.arange(t // qs) * qs
    seq_start = np.where(sb_tok < tp, sb_tok // sp * sp, tp + (sb_tok - tp) // ss * ss)
    seq_len = np.where(sb_tok < tp, sp, ss)
    flags = np.stack([sb_tok != seq_start, sb_tok + qs != seq_start + seq_len], axis=1).astype(np.int32).reshape(-1)
    attn_a = _win_attn_call(proj, _window_bias(t5_bias), sink_a[0].astype(F32), jnp.asarray(flags), qs, base)

    cfg_ids, kstarts, cfg_blocks = _na_plan([sp] * bp + [ss] * bs, seg)
    bias_cfg = _na_bias_tables(na_bias[0], cfg_blocks)
    attn_b = _na_attn_call(proj, bias_cfg, jnp.asarray(cfg_ids), jnp.asarray(kstarts), seg, base)

    wr = jnp.zeros((d, LANES), F32).at[:, :n_e].set(w_router[0])
    wr_hi = wr.astype(BF16)
    wr_lo = (wr - wr_hi.astype(F32)).astype(BF16)
    br = jnp.zeros((1, LANES), F32).at[0, :n_e].set(b_router[0])
    x1, h2, logits = _outproj_call(attn_a, attn_b, proj, xp, xs, mod_seg, w_o_a[0].astype(BF16),
                                   w_o_b[0].astype(BF16), w_out[0].astype(BF16), g_norm2[0], wr_hi, wr_lo, br, seg)

    tm_e = min(1024, seg)
    top_logits, top_idx = lax.top_k(logits[:, :n_e], TOP_K)
    top_w = jax.nn.softmax(top_logits, axis=-1)
    n_assign = t * TOP_K
    e_flat = top_idx.reshape(-1)
    onehot = (e_flat[:, None] == jnp.arange(n_e, dtype=e_flat.dtype)[None, :]).astype(jnp.int32)
    csum = jnp.cumsum(onehot, axis=0)
    rank = jnp.take_along_axis(csum, e_flat[:, None], axis=1)[:, 0] - 1
    counts = csum[-1]
    padded = (counts + tm_e - 1) // tm_e * tm_e
    pend = jnp.cumsum(padded)
    dest = (pend - padded)[e_flat] + rank
    n_tiles = -(-n_assign // tm_e) + n_e
    n_rows = n_tiles * tm_e
    tile_e = jnp.minimum(jnp.searchsorted(pend, jnp.arange(n_tiles) * tm_e, side='right'), n_e - 1).astype(jnp.int32)
    n_valid = (pend[-1:] // tm_e).astype(jnp.int32)
    tok_flat = jnp.arange(n_assign, dtype=jnp.int32) // TOP_K
    _, tok_sorted = lax.sort((e_flat, tok_flat), num_keys=1, is_stable=True)
    slot = jnp.arange(n_rows, dtype=jnp.int32)
    slot_e = jnp.repeat(tile_e, tm_e)
    idx_in_e = slot - (pend - padded)[slot_e]
    src = jnp.minimum((jnp.cumsum(counts) - counts)[slot_e] + idx_in_e, n_assign - 1)
    xin = h2[tok_sorted[src]]

    w_gate, w_lin, w_dn = _cast_experts_call(w_gu[0], w_down[0])
    yb = _moe_call(xin, tile_e, n_valid, w_gate, w_lin, b_gu[0], w_dn, b_down[0], tm_e)

    ysel = yb[dest.reshape(t, TOP_K).T]
    y_prompt = _combine_call(x1, ysel, top_w, mod_seg, 0, tp, seg)
    y_sample = _combine_call(x1, ysel, top_w, mod_seg, tp, ts, seg)
    return (y_prompt.reshape(bp, sp, d), y_sample.reshape(bs, ss, d))
```

```python
import functools
import math

import numpy as np
import jax
import jax.numpy as jnp
from jax import lax
from jax.experimental import pallas as pl
from jax.experimental.pallas import tpu as pltpu

F32 = jnp.float32
BF16 = jnp.bfloat16

HEAD_DIM = 64
N_HEADS_A = 16
N_KV_A = 4
GROUP_A = N_HEADS_A // N_KV_A
N_HEADS_B = 16
WINDOW = 128
ATTN_BLOCK = 128
T5_BUCKETS = 32
T5_MAX_DIST = 128
GRID_W = 64
NA_ROWS = 8
NA_COLS = 16
TOP_K = 4
SWIGLU_LIMIT = 7.0
SWIGLU_ALPHA = 1.702
RMS_EPS = 1e-6

Q_A = N_HEADS_A * HEAD_DIM
KV_A = N_KV_A * HEAD_DIM
W_B = N_HEADS_B * HEAD_DIM

LANES = 128
MXU_DIM = 256
VMEM_LIMIT_BYTES = 56 * 1024 * 1024
MASKED = -1e30

N_HEAD_PAIRS = Q_A // LANES
COL_QA = 0
COL_QB = COL_QA + Q_A
COL_KB = COL_QB + W_B
COL_KA = COL_KB + W_B
COL_VA = COL_KA + N_KV_A * LANES
COL_VB = COL_VA + N_KV_A * LANES
COL_END = COL_VB + W_B
NA_GROUP_ROWS = 2
NA_KEY_ROWS = NA_GROUP_ROWS + NA_ROWS - 1
NA_HALO_ROWS = NA_ROWS


def _sigmoid(x):
    return 1.0 / (1.0 + jnp.exp(-x))


def _cparams(sem):
    return pltpu.CompilerParams(dimension_semantics=sem, vmem_limit_bytes=VMEM_LIMIT_BYTES)


def _ada_kernel(c_ref, w_ref, b_ref, o_ref):
    c = c_ref[...]
    a = (c * _sigmoid(c)).astype(BF16)
    o_ref[...] = jnp.dot(a, w_ref[...].astype(BF16), preferred_element_type=F32) + b_ref[...]


def _ada_call(c_pad, w_ada, b_ada):
    rows, d = c_pad.shape
    n = w_ada.shape[1]
    tn = min(n, 1024)
    return pl.pallas_call(
        _ada_kernel,
        grid=(n // tn,),
        in_specs=[pl.BlockSpec((rows, d), lambda j: (0, 0)),
                  pl.BlockSpec((d, tn), lambda j: (0, j)),
                  pl.BlockSpec((1, tn), lambda j: (0, j))],
        out_specs=pl.BlockSpec((rows, tn), lambda j: (0, j)),
        out_shape=jax.ShapeDtypeStruct((rows, n), F32),
        compiler_params=_cparams(("arbitrary",)),
        name="ada_ln",
    )(c_pad, w_ada, b_ada.reshape(1, n))


def _two_group_specs(tm, d, n_prompt_tiles, n_sample_tiles):
    xp = pl.BlockSpec((tm, d), lambda i, *_: (jnp.minimum(i, n_prompt_tiles - 1), 0))
    xs = pl.BlockSpec((tm, d), lambda i, *_: (jnp.clip(i - n_prompt_tiles, 0, n_sample_tiles - 1), 0))
    return xp, xs


def _norm_mod_kernel(xp_ref, xs_ref, mod_ref, g_ref, o_ref, *, n_prompt_tiles):
    i = pl.program_id(0)
    x = jnp.where(i < n_prompt_tiles, xp_ref[...], xs_ref[...])
    y = x * lax.rsqrt(jnp.mean(x * x, axis=-1, keepdims=True) + RMS_EPS) * g_ref[...]
    m = mod_ref[0]
    o_ref[...] = (y * (1.0 + m[1:2]) + m[0:1]).astype(BF16)


def _norm_mod_call(xp, xs, mod_seg, g, seg):
    d = xp.shape[1]
    tm = min(512, seg)
    npt, nst = xp.shape[0] // tm, xs.shape[0] // tm
    xp_spec, xs_spec = _two_group_specs(tm, d, npt, nst)
    return pl.pallas_call(
        functools.partial(_norm_mod_kernel, n_prompt_tiles=npt),
        grid=(npt + nst,),
        in_specs=[xp_spec, xs_spec,
                  pl.BlockSpec((1, 6, d), lambda i: (i * tm // seg, 0, 0)),
                  pl.BlockSpec((1, d), lambda i: (0, 0))],
        out_specs=pl.BlockSpec((tm, d), lambda i: (i, 0)),
        out_shape=jax.ShapeDtypeStruct((xp.shape[0] + xs.shape[0], d), BF16),
        compiler_params=_cparams(("arbitrary",)),
        name="norm1_mod",
    )(xp, xs, mod_seg, g.reshape(1, d))


def _group_rms(y, g, ones_ref):
    sq = (y * y).astype(BF16)
    parts = []
    for c in range(y.shape[1] // MXU_DIM):
        parts.append(jnp.dot(sq[:, c * MXU_DIM:(c + 1) * MXU_DIM], ones_ref[...], preferred_element_type=F32))
    ss = jnp.concatenate(parts, axis=1) if len(parts) > 1 else parts[0]
    return y * lax.rsqrt(ss * (1.0 / HEAD_DIM) + RMS_EPS) * g


INPROJ_ROW_CHUNKS = 2


def _inproj_kernel(h_ref, w_ref, g_ref, ones_ref, o_ref, *, tn, gate_tiles):
    j = pl.program_id(1)
    half = tn // 2
    mixed_tile = gate_tiles + COL_KA // tn
    plain_tile = gate_tiles + COL_VB // tn
    rows = h_ref.shape[0] // INPROJ_ROW_CHUNKS

    def chunked(epilogue):
        for c in range(INPROJ_ROW_CHUNKS):
            r = slice(c * rows, (c + 1) * rows)
            epilogue(jnp.dot(h_ref[r, :], w_ref[...], preferred_element_type=F32), r)

    @pl.when(j < gate_tiles)
    def _():
        def epilogue(y, r):
            o_ref[r, :] = (0.5 * jnp.tanh(0.5 * y) + 0.5).astype(BF16)
        chunked(epilogue)

    @pl.when((j >= gate_tiles) & (j < mixed_tile))
    def _():
        def epilogue(y, r):
            o_ref[r, :] = _group_rms(y, g_ref[...], ones_ref).astype(BF16)
        chunked(epilogue)

    @pl.when(j == mixed_tile)
    def _():
        def epilogue(y, r):
            o_ref[r, :half] = _group_rms(y[:, :half], g_ref[:, :half], ones_ref).astype(BF16)
            o_ref[r, half:] = y[:, half:].astype(BF16)
        chunked(epilogue)

    @pl.when(j == plain_tile)
    def _():
        def epilogue(y, r):
            o_ref[r, :] = y.astype(BF16)
        chunked(epilogue)


def _inproj_call(h, w, gvec, ones_bd, tm):
    t, d = h.shape
    pw = w.shape[1]
    tn = 1024
    assert (2 * d) % tn == 0 and COL_KA % tn == 0 and COL_VA - COL_KA == tn // 2 and COL_VB % tn == 0
    return pl.pallas_call(
        functools.partial(_inproj_kernel, tn=tn, gate_tiles=2 * d // tn),
        grid=(t // tm, pw // tn),
        in_specs=[pl.BlockSpec((tm, d), lambda i, j: (i, 0)),
                  pl.BlockSpec((d, tn), lambda i, j: (0, j)),
                  pl.BlockSpec((1, tn), lambda i, j: (0, j)),
                  pl.BlockSpec((MXU_DIM, MXU_DIM), lambda i, j: (0, 0))],
        out_specs=pl.BlockSpec((tm, tn), lambda i, j: (i, j)),
        out_shape=jax.ShapeDtypeStruct((t, pw), BF16),
        compiler_params=_cparams(("arbitrary", "arbitrary")),
        name="in_proj",
    )(h, w, gvec, ones_bd)


def _aligned(x, m):
    return x if isinstance(x, int) else pl.multiple_of(x, m)


def _three_stage_pipeline(n, stage_a, stage_b, stage_c):
    assert n >= 6 and n % 2 == 0
    pairs = n // 2

    def trip(t, run_a=True, run_b=True, run_c=True):
        for k in range(2):
            if run_c:
                stage_c(2 * (t - 2) + k, k)
        for k in range(2):
            if run_b:
                stage_b(2 * (t - 1) + k, k)
        for k in range(2):
            if run_a:
                stage_a(2 * t + k, k)

    trip(0, run_b=False, run_c=False)
    trip(1, run_c=False)

    def body(t, carry):
        trip(t)
        return carry

    lax.fori_loop(2, pairs, body, 0)
    trip(pairs, run_a=False)
    trip(pairs + 1, run_a=False, run_b=False)


def _win_attn_kernel(flags_ref, q_ref, kp_ref, kc_ref, kn_ref, vp_ref, vc_ref, vn_ref, bias_ref, sink_ref,
                     o_ref, kcat, vcat, s_scr, e_scr, r_scr, *, qs, nb):
    p = pl.program_id(0)
    sb = pl.program_id(1)
    blk = ATTN_BLOCK
    kcat[0:blk] = kp_ref[...]
    kcat[blk:blk + qs] = kc_ref[...]
    kcat[blk + qs:] = kn_ref[...]
    vcat[0:blk] = vp_ref[...]
    vcat[blk:blk + qs] = vc_ref[...]
    vcat[blk + qs:] = vn_ref[...]
    prev_ok = flags_ref[2 * sb]
    next_ok = flags_ref[2 * sb + 1]
    low_half = lax.broadcasted_iota(jnp.int32, (blk, LANES), 1) < HEAD_DIM
    row = lax.broadcasted_iota(jnp.int32, (2 * blk, 1), 0)
    sinkcol = jnp.where(row < blk, sink_ref[2 * p], sink_ref[2 * p + 1])
    kl = blk + 2 * WINDOW

    def scores(b, slot):
        r0 = _aligned(b * blk, blk)
        q = q_ref[pl.ds(r0, blk), :]
        zero = jnp.zeros_like(q)
        q2 = jnp.concatenate([jnp.where(low_half, q, zero), jnp.where(low_half, zero, q)], axis=0)
        kw = kcat[pl.ds(r0, kl), :]
        s_scr[slot] = lax.dot_general(q2, kw, (((1,), (1,)), ((), ())), preferred_element_type=F32)

    def softmax(b, slot):
        variant = jnp.where((b == 0) & (prev_ok == 0), 1, jnp.where((b == nb - 1) & (next_ok == 0), 2, 0))
        s = s_scr[slot] + bias_ref[variant].reshape(2 * blk, kl)
        m = jnp.maximum(jnp.max(s, axis=-1, keepdims=True), sinkcol)
        e = jnp.exp(s - m)
        denom = jnp.sum(e, axis=-1, keepdims=True) + jnp.exp(sinkcol - m)
        e_scr[slot] = e.astype(BF16)
        r_scr[slot] = 1.0 / denom

    def values(b, slot):
        r0 = _aligned(b * blk, blk)
        vw = vcat[pl.ds(r0, kl), :]
        o2 = jnp.dot(e_scr[slot], vw, preferred_element_type=F32) * r_scr[slot]
        o_ref[pl.ds(r0, blk), :] = jnp.where(low_half, o2[:blk], o2[blk:]).astype(BF16)

    _three_stage_pipeline(nb, scores, softmax, values)


def _win_attn_call(proj, bias3, sink, flags, qs, base):
    t = proj.shape[0]
    blk = ATTN_BLOCK
    nb = qs // blk
    n_super = t // qs
    n_blk_rows = t // blk
    kl = blk + 2 * WINDOW
    ka0, va0 = base + COL_KA // LANES, base + COL_VA // LANES

    def cur(col0):
        return pl.BlockSpec((qs, LANES), lambda p, s, f: (s, col0 + p // 2))

    def prev(col0):
        return pl.BlockSpec((blk, LANES), lambda p, s, f: (jnp.maximum(s * nb - 1, 0), col0 + p // 2))

    def nxt(col0):
        return pl.BlockSpec((blk, LANES), lambda p, s, f: (jnp.minimum((s + 1) * nb, n_blk_rows - 1), col0 + p // 2))

    grid_spec = pltpu.PrefetchScalarGridSpec(
        num_scalar_prefetch=1,
        grid=(N_HEAD_PAIRS, n_super),
        in_specs=[pl.BlockSpec((qs, LANES), lambda p, s, f: (s, base + COL_QA // LANES + p)),
                  prev(ka0), cur(ka0), nxt(ka0), prev(va0), cur(va0), nxt(va0),
                  pl.BlockSpec((3, 2, blk, kl), lambda p, s, f: (0, p, 0, 0)),
                  pl.BlockSpec(memory_space=pltpu.SMEM)],
        out_specs=pl.BlockSpec((qs, LANES), lambda p, s, f: (s, p)),
        scratch_shapes=[pltpu.VMEM((qs + 2 * blk, LANES), BF16), pltpu.VMEM((qs + 2 * blk, LANES), BF16),
                        pltpu.VMEM((2, 2 * blk, kl), F32), pltpu.VMEM((2, 2 * blk, kl), BF16),
                        pltpu.VMEM((2, 2 * blk, 1), F32)],
    )
    return pl.pallas_call(
        functools.partial(_win_attn_kernel, qs=qs, nb=nb),
        grid_spec=grid_spec,
        out_shape=jax.ShapeDtypeStruct((t, Q_A), BF16),
        compiler_params=_cparams(("arbitrary", "arbitrary")),
        name="win_attn",
    )(flags, proj, proj, proj, proj, proj, proj, proj, bias3, sink)


def _t5_bucket(rel):
    nb = T5_BUCKETS // 2
    max_exact = nb // 2
    ret = jnp.where(rel > 0, nb, 0)
    n = jnp.abs(rel)
    nf = jnp.maximum(n, 1).astype(jnp.float32)
    large = max_exact + (jnp.log(nf / max_exact) / math.log(T5_MAX_DIST / max_exact) * (nb - max_exact)).astype(jnp.int32)
    large = jnp.minimum(large, nb - 1)
    return ret + jnp.where(n < max_exact, n, large)


def _toeplitz(v, n_rows, n_cols):
    p = v.shape[-1]
    assert n_cols <= p - 1
    flat = jnp.tile(v, (1,) * (v.ndim - 1) + (n_rows,))[..., :n_rows * (p - 1)]
    return flat.reshape(v.shape[:-1] + (n_rows, p - 1))[..., :n_cols]


def _window_bias(t5_bias):
    blk, kl = ATTN_BLOCK, ATTN_BLOCK + 2 * WINDOW
    period = blk + kl - 1
    m = np.arange(period)
    rel_of_m = np.where(m < kl, m, m - period) - WINDOW
    per_rel = t5_bias[_t5_bucket(jnp.asarray(rel_of_m))].astype(F32).T
    bias = _toeplitz(per_rel, blk, kl)
    col = np.arange(kl)[None, :]
    band = np.abs(col - WINDOW - np.arange(blk)[:, None]) <= WINDOW
    keep = np.stack([band, band & (col >= WINDOW), band & (col < WINDOW + blk)])
    return jnp.where(keep[:, None], bias[None], MASKED)


def _na_plan(seq_lens, seg):
    qr, nkr = NA_GROUP_ROWS, NA_KEY_ROWS
    configs, cfg_ids, kstarts = {}, [], []
    tok = 0
    for s_len in seq_lens:
        rows = s_len // GRID_W
        assert rows >= nkr and rows % qr == 0
        for r in range(0, rows, qr):
            us = min(max(r - NA_ROWS // 2, 0), rows - nkr)
            rel = tuple(min(max(r + q - NA_ROWS // 2, 0), rows - NA_ROWS) - us for q in range(qr))
            key = (r - us, rel)
            cfg_ids.append(configs.setdefault(key, len(configs)))
            g_tok = tok + r * GRID_W
            seg_tok0 = (g_tok // seg) * seg
            kstarts.append(tok + us * GRID_W - seg_tok0 + NA_HALO_ROWS * GRID_W)
        tok += s_len
    blocks = [None] * len(configs)
    for (r_us, rel), c in configs.items():
        blocks[c] = [[kr - r_us - q + NA_ROWS - 1 if rel[q] <= kr < rel[q] + NA_ROWS else None
                      for kr in range(nkr)] for q in range(qr)]
    return np.asarray(cfg_ids, np.int32), np.asarray(kstarts, np.int32), blocks


def _na_bias_tables(na_bias, blocks):
    period = 2 * GRID_W - 1
    m = np.arange(period)
    dcol = np.where(m < GRID_W, m, m - period)
    per_dcol = na_bias.astype(F32)[:, :, np.clip(dcol + NA_COLS - 1, 0, 2 * NA_COLS - 2)]
    col_tab = _toeplitz(per_dcol, GRID_W, GRID_W)
    q_col, k_col = np.arange(GRID_W)[:, None], np.arange(GRID_W)[None, :]
    win_start = np.clip(q_col - NA_COLS // 2, 0, GRID_W - NA_COLS)
    col_ok = (k_col >= win_start) & (k_col < win_start + NA_COLS)
    col_tab = jnp.where(col_ok, col_tab, MASKED)
    masked_blk = jnp.full(col_tab.shape[:1] + col_tab.shape[2:], MASKED, F32)
    cfgs = []
    for cfg in blocks:
        rows = [jnp.concatenate([masked_blk if dr is None else col_tab[:, dr] for dr in qrow], axis=-1) for qrow in cfg]
        cfgs.append(jnp.concatenate(rows, axis=-2))
    return jnp.stack(cfgs)


def _na_attn_kernel(cfg_ref, ks_ref, q_ref, kp_ref, kc_ref, kn_ref, vp_ref, vc_ref, vn_ref, bias_ref,
                    o_ref, kcat, vcat, s_scr, e_scr, r_scr, *, seg):
    s_id = pl.program_id(1)
    halo = NA_HALO_ROWS * GRID_W
    m, nk = NA_GROUP_ROWS * GRID_W, NA_KEY_ROWS * GRID_W
    n_groups = seg // m
    kcat[0:halo] = kp_ref[...]
    kcat[halo:halo + seg] = kc_ref[...]
    kcat[halo + seg:] = kn_ref[...]
    vcat[0:halo] = vp_ref[...]
    vcat[halo:halo + seg] = vc_ref[...]
    vcat[halo + seg:] = vn_ref[...]
    low_half = lax.broadcasted_iota(jnp.int32, (m, LANES), 1) < HEAD_DIM

    def scores(g, slot):
        ks = pl.multiple_of(ks_ref[s_id * n_groups + g], GRID_W)
        r0 = _aligned(g * m, m)
        q = q_ref[pl.ds(r0, m), :]
        zero = jnp.zeros_like(q)
        q2 = jnp.concatenate([jnp.where(low_half, q, zero), jnp.where(low_half, zero, q)], axis=0)
        kw = kcat[pl.ds(ks, nk), :]
        s_scr[slot] = lax.dot_general(q2, kw, (((1,), (1,)), ((), ())), preferred_element_type=F32)

    def softmax(g, slot):
        cfg = cfg_ref[s_id * n_groups + g]
        s = s_scr[slot] + bias_ref[cfg].reshape(2 * m, nk)
        mx = jnp.max(s, axis=-1, keepdims=True)
        e = jnp.exp(s - mx)
        e_scr[slot] = e.astype(BF16)
        r_scr[slot] = 1.0 / jnp.sum(e, axis=-1, keepdims=True)

    def values(g, slot):
        ks = pl.multiple_of(ks_ref[s_id * n_groups + g], GRID_W)
        r0 = _aligned(g * m, m)
        vw = vcat[pl.ds(ks, nk), :]
        o2 = jnp.dot(e_scr[slot], vw, preferred_element_type=F32) * r_scr[slot]
        o_ref[pl.ds(r0, m), :] = jnp.where(low_half, o2[:m], o2[m:]).astype(BF16)

    _three_stage_pipeline(n_groups, scores, softmax, values)


def _na_attn_call(proj, bias_cfg, cfg_ids, kstarts, seg, base):
    t = proj.shape[0]
    halo = NA_HALO_ROWS * GRID_W
    n_seg = t // seg
    per = seg // halo
    n_halo_blocks = t // halo
    ncfg = bias_cfg.shape[0]
    m, nk = NA_GROUP_ROWS * GRID_W, NA_KEY_ROWS * GRID_W
    kb0, vb0 = base + COL_KB // LANES, base + COL_VB // LANES

    def cur(col0):
        return pl.BlockSpec((seg, LANES), lambda p, s, c, k: (s, col0 + p))

    def prev(col0):
        return pl.BlockSpec((halo, LANES), lambda p, s, c, k: (jnp.maximum(s * per - 1, 0), col0 + p))

    def nxt(col0):
        return pl.BlockSpec((halo, LANES), lambda p, s, c, k: (jnp.minimum((s + 1) * per, n_halo_blocks - 1), col0 + p))

    grid_spec = pltpu.PrefetchScalarGridSpec(
        num_scalar_prefetch=2,
        grid=(N_HEAD_PAIRS, n_seg),
        in_specs=[cur(base + COL_QB // LANES), prev(kb0), cur(kb0), nxt(kb0), prev(vb0), cur(vb0), nxt(vb0),
                  pl.BlockSpec((ncfg, 2, m, nk), lambda p, s, c, k: (0, p, 0, 0))],
        out_specs=pl.BlockSpec((seg, LANES), lambda p, s, c, k: (s, p)),
        scratch_shapes=[pltpu.VMEM((seg + 2 * halo, LANES), BF16), pltpu.VMEM((seg + 2 * halo, LANES), BF16),
                        pltpu.VMEM((2, 2 * m, nk), F32), pltpu.VMEM((2, 2 * m, nk), BF16),
                        pltpu.VMEM((2, 2 * m, 1), F32)],
    )
    return pl.pallas_call(
        functools.partial(_na_attn_kernel, seg=seg),
        grid_spec=grid_spec,
        out_shape=jax.ShapeDtypeStruct((t, W_B), BF16),
        compiler_params=_cparams(("arbitrary", "arbitrary")),
        name="na_attn",
    )(cfg_ids, kstarts, proj, proj, proj, proj, proj, proj, proj, bias_cfg)


def _outproj_kernel(aa_ref, ab_ref, ga_ref, gb_ref, xp_ref, xs_ref, mod_ref, woa_ref, wob_ref, wout_ref,
                    g2_ref, wrh_ref, wrl_ref, br_ref, x1_ref, h2_ref, lg_ref, *, n_prompt_tiles):
    i = pl.program_id(0)
    ya = jnp.dot(aa_ref[...], woa_ref[...], preferred_element_type=F32)
    yb = jnp.dot(ab_ref[...], wob_ref[...], preferred_element_type=F32)
    merged = ga_ref[...].astype(F32) * ya + gb_ref[...].astype(F32) * yb
    z = jnp.dot(merged.astype(BF16), wout_ref[...], preferred_element_type=F32)
    x = jnp.where(i < n_prompt_tiles, xp_ref[...], xs_ref[...])
    m = mod_ref[0]
    x1 = x + m[2:3] * z
    x1_ref[...] = x1
    y = x1 * lax.rsqrt(jnp.mean(x1 * x1, axis=-1, keepdims=True) + RMS_EPS) * g2_ref[...]
    h2 = y * (1.0 + m[4:5]) + m[3:4]
    h2_hi = h2.astype(BF16)
    h2_ref[...] = h2_hi
    h2_lo = (h2 - h2_hi.astype(F32)).astype(BF16)
    lg = jnp.dot(h2_hi, wrh_ref[...], preferred_element_type=F32)
    lg = lg + jnp.dot(h2_lo, wrh_ref[...], preferred_element_type=F32)
    lg = lg + jnp.dot(h2_hi, wrl_ref[...], preferred_element_type=F32)
    lg_ref[...] = lg + br_ref[...]


def _outproj_call(attn_a, attn_b, proj, xp, xs, mod_seg, woa, wob, wout, g2, wr_hi, wr_lo, br, seg):
    t = attn_a.shape[0]
    d = xp.shape[1]
    tm = min(256, seg)
    npt, nst = xp.shape[0] // tm, xs.shape[0] // tm
    xp_spec, xs_spec = _two_group_specs(tm, d, npt, nst)
    def const(shape):
        return pl.BlockSpec(shape, lambda i: (0,) * len(shape), pipeline_mode=pl.Buffered(1))

    return pl.pallas_call(
        functools.partial(_outproj_kernel, n_prompt_tiles=npt),
        grid=(npt + nst,),
        in_specs=[pl.BlockSpec((tm, Q_A), lambda i: (i, 0)),
                  pl.BlockSpec((tm, W_B), lambda i: (i, 0)),
                  pl.BlockSpec((tm, d), lambda i: (i, 0)),
                  pl.BlockSpec((tm, d), lambda i: (i, 1)),
                  xp_spec, xs_spec,
                  pl.BlockSpec((1, 6, d), lambda i: (i * tm // seg, 0, 0)),
                  const((Q_A, d)), const((W_B, d)), const((d, d)), const((1, d)),
                  const((d, LANES)), const((d, LANES)), const((1, LANES))],
        out_specs=[pl.BlockSpec((tm, d), lambda i: (i, 0)),
                   pl.BlockSpec((tm, d), lambda i: (i, 0)),
                   pl.BlockSpec((tm, LANES), lambda i: (i, 0))],
        out_shape=[jax.ShapeDtypeStruct((t, d), F32),
                   jax.ShapeDtypeStruct((t, d), BF16),
                   jax.ShapeDtypeStruct((t, LANES), F32)],
        compiler_params=_cparams(("arbitrary",)),
        name="out_proj",
    )(attn_a, attn_b, proj, proj, xp, xs, mod_seg, woa, wob, wout, g2.reshape(1, d), wr_hi, wr_lo, br)


def _cast_kernel(after_ref, g_ref, l_ref, d_ref, og_ref, ol_ref, od_ref):
    del after_ref
    og_ref[...] = g_ref[...].astype(BF16)
    ol_ref[...] = l_ref[...].astype(BF16)
    od_ref[...] = d_ref[...].astype(BF16)


def _cast_experts_call(w_gu, w_down, after):
    n_e, d, two_ff = w_gu.shape
    d_ff = two_ff // 2
    assert w_down.shape == (n_e, d_ff, d) and d_ff == d
    tr = min(512, d)
    rows = n_e * d
    return pl.pallas_call(
        _cast_kernel,
        grid=(rows // tr,),
        in_specs=[pl.BlockSpec(memory_space=pltpu.SMEM),
                  pl.BlockSpec((tr, d_ff), lambda i: (i, 0)),
                  pl.BlockSpec((tr, d_ff), lambda i: (i, 1)),
                  pl.BlockSpec((tr, d), lambda i: (i, 0))],
        out_specs=[pl.BlockSpec((tr, d_ff), lambda i: (i, 0)),
                   pl.BlockSpec((tr, d_ff), lambda i: (i, 0)),
                   pl.BlockSpec((tr, d), lambda i: (i, 0))],
        out_shape=[jax.ShapeDtypeStruct((rows, d_ff), BF16),
                   jax.ShapeDtypeStruct((rows, d_ff), BF16),
                   jax.ShapeDtypeStruct((rows, d), BF16)],
        compiler_params=_cparams(("arbitrary",)),
        name="cast_experts",
    )(after, w_gu.reshape(rows, two_ff), w_gu.reshape(rows, two_ff), w_down.reshape(rows, d))


def _moe_kernel(te_ref, nv_ref, x_ref, wg_ref, wl_ref, bg_ref, bl_ref, wd_ref, bd_ref, o_ref, acc_ref, *, nf):
    t = pl.program_id(0)
    f = pl.program_id(1)
    live = t < nv_ref[0]

    @pl.when((t == 0) & (f == 0))
    def _():
        acc_ref[...] = jnp.zeros_like(acc_ref)

    @pl.when(live)
    def _():
        x = x_ref[...]
        gate = jnp.dot(x, wg_ref[...], preferred_element_type=F32) + bg_ref[0]
        lin = jnp.dot(x, wl_ref[...], preferred_element_type=F32) + bl_ref[0]
        gate = jnp.minimum(gate, SWIGLU_LIMIT)
        lin = jnp.clip(lin, -SWIGLU_LIMIT, SWIGLU_LIMIT)
        act = gate * _sigmoid(SWIGLU_ALPHA * gate) * (lin + 1.0)
        part = jnp.dot(act.astype(BF16), wd_ref[...], preferred_element_type=F32)
        total = part + jnp.where(f == 0, bd_ref[0], acc_ref[...])
        acc_ref[...] = total
        o_ref[...] = total.astype(BF16)

    @pl.when(jnp.logical_not(live) & (f == 0))
    def _():
        o_ref[...] = jnp.zeros_like(o_ref)


def _moe_call(xin, tile_e, n_valid, w_gate, w_lin, b_gu, w_down, b_down, tm):
    n_rows, d = xin.shape
    n_e, two_ff = b_gu.shape
    d_ff = two_ff // 2
    tf = min(512, d_ff)
    nf = d_ff // tf
    n_tiles = n_rows // tm

    def fidx(t, f, nv):
        return jnp.where(t < nv[0], f, nf - 1)

    grid_spec = pltpu.PrefetchScalarGridSpec(
        num_scalar_prefetch=2,
        grid=(n_tiles, nf),
        in_specs=[pl.BlockSpec((tm, d), lambda t, f, te, nv: (t, 0)),
                  pl.BlockSpec((d, tf), lambda t, f, te, nv: (te[t], fidx(t, f, nv))),
                  pl.BlockSpec((d, tf), lambda t, f, te, nv: (te[t], fidx(t, f, nv))),
                  pl.BlockSpec((1, 1, tf), lambda t, f, te, nv: (te[t], 0, fidx(t, f, nv))),
                  pl.BlockSpec((1, 1, tf), lambda t, f, te, nv: (te[t], 0, fidx(t, f, nv) + nf)),
                  pl.BlockSpec((tf, d), lambda t, f, te, nv: (te[t] * nf + fidx(t, f, nv), 0)),
                  pl.BlockSpec((1, 1, d), lambda t, f, te, nv: (te[t], 0, 0))],
        out_specs=pl.BlockSpec((tm, d), lambda t, f, te, nv: (t, 0)),
        scratch_shapes=[pltpu.VMEM((tm, d), F32)],
    )
    return pl.pallas_call(
        functools.partial(_moe_kernel, nf=nf),
        grid_spec=grid_spec,
        out_shape=jax.ShapeDtypeStruct((n_rows, d), BF16),
        compiler_params=_cparams(("arbitrary", "arbitrary")),
        name="moe_experts",
    )(tile_e, n_valid, xin, w_gate, w_lin, b_gu.reshape(n_e, 1, two_ff), b_gu.reshape(n_e, 1, two_ff),
      w_down, b_down.reshape(n_e, 1, d))


def _combine_kernel(x1_ref, ys_ref, w_ref, mod_ref, o_ref):
    w = w_ref[...]
    acc = w[:, 0:1] * ys_ref[0].astype(F32)
    for k in range(1, TOP_K):
        acc = acc + w[:, k:k + 1] * ys_ref[k].astype(F32)
    o_ref[...] = x1_ref[...] + mod_ref[0][5:6] * acc


def _combine_call(x1, ysel, top_w, mod_seg, row0, n_rows, seg):
    d = x1.shape[1]
    tm = min(512, seg)
    off = row0 // tm
    return pl.pallas_call(
        _combine_kernel,
        grid=(n_rows // tm,),
        in_specs=[pl.BlockSpec((tm, d), lambda i: (i + off, 0)),
                  pl.BlockSpec((TOP_K, tm, d), lambda i: (0, i + off, 0)),
                  pl.BlockSpec((tm, TOP_K), lambda i: (i + off, 0)),
                  pl.BlockSpec((1, 6, d), lambda i: ((i + off) * tm // seg, 0, 0))],
        out_specs=pl.BlockSpec((tm, d), lambda i: (i, 0)),
        out_shape=jax.ShapeDtypeStruct((n_rows, d), F32),
        compiler_params=_cparams(("arbitrary",)),
        name="moe_combine",
    )(x1, ysel, top_w, mod_seg)


def _pack_w_in(w_in, g_q_a, g_k_a, g_q_b, g_k_b):
    o1 = Q_A
    o2 = o1 + KV_A
    o3 = o2 + KV_A
    o4 = o3 + W_B
    o5 = o4 + W_B
    o6 = o5 + W_B
    d = w_in.shape[0]
    wqa, wka, wva = w_in[:, :o1], w_in[:, o1:o2], w_in[:, o2:o3]
    wqb, wkb, wvb, wg = w_in[:, o3:o4], w_in[:, o4:o5], w_in[:, o5:o6], w_in[:, o6:]

    def dup(w):
        w4 = w.reshape(d, N_KV_A, 1, HEAD_DIM)
        return jnp.broadcast_to(w4, (d, N_KV_A, 2, HEAD_DIM)).reshape(d, N_KV_A * LANES)

    w = jnp.concatenate([wg, wqa, wqb, wkb, dup(wka), dup(wva), wvb], axis=1).astype(BF16)
    q_scale = HEAD_DIM ** -0.5
    gvec = jnp.concatenate([jnp.ones((2 * d,), F32),
                            jnp.tile(g_q_a * q_scale, N_HEADS_A), jnp.tile(g_q_b * q_scale, N_HEADS_B),
                            jnp.tile(g_k_b, N_HEADS_B), jnp.tile(g_k_a, 2 * N_KV_A),
                            jnp.ones((COL_END - COL_VA,), F32)]).reshape(1, -1).astype(F32)
    return w, gvec


def kernel(x_prompt, x_sample, c_prompt, c_sample, w_ada, b_ada, g_norm1, w_in, g_q_a, g_k_a, g_q_b, g_k_b, sink_a, t5_bias, na_bias, w_o_a, w_o_b, w_out, g_norm2, w_router, b_router, w_gu, b_gu, w_down, b_down):
    bp, sp, d = x_prompt.shape
    bs, ss, _ = x_sample.shape
    n_e = w_router.shape[-1]
    tp, ts = bp * sp, bs * ss
    t = tp + ts
    seg = math.gcd(math.gcd(sp, ss), 2048)
    assert w_ada.shape[0] == 1 and sp % seg == 0 and ss % seg == 0 and seg % (NA_HALO_ROWS * GRID_W) == 0
    assert 2 * d == w_in.shape[-1] - (Q_A + 2 * KV_A + 3 * W_B)
    base = 2 * d // LANES
    xp = x_prompt.reshape(tp, d)
    xs = x_sample.reshape(ts, d)

    n_c = bp + bs
    c_pad = jnp.zeros((-(-n_c // 16) * 16, d), F32).at[:n_c].set(jnp.concatenate([c_prompt, c_sample], axis=0))
    mod = _ada_call(c_pad, w_ada[0], b_ada[0])[:n_c].reshape(n_c, 6, d)
    seg_owner = np.concatenate([np.repeat(np.arange(bp), sp // seg), bp + np.repeat(np.arange(bs), ss // seg)])
    mod_seg = mod[seg_owner]

    h1 = _norm_mod_call(xp, xs, mod_seg, g_norm1[0], seg)

    w_slab, gvec = _pack_w_in(w_in[0], g_q_a[0], g_k_a[0], g_q_b[0], g_k_b[0])
    grp = np.arange(MXU_DIM) // HEAD_DIM
    ones_bd = jnp.asarray(grp[:, None] == grp[None, :], BF16)
    proj = _inproj_call(h1, w_slab, gvec, ones_bd, tm=min(1024, seg))

    qs = min(2048, seg)
    sb_tok = np.arange(t // qs) * qs
    seq_start = np.where(sb_tok < tp, sb_tok // sp * sp, tp + (sb_tok - tp) // ss * ss)
    seq_len = np.where(sb_tok < tp, sp, ss)
    flags = np.stack([sb_tok != seq_start, sb_tok + qs != seq_start + seq_len], axis=1).astype(np.int32).reshape(-1)
    attn_a = _win_attn_call(proj, _window_bias(t5_bias), sink_a[0].astype(F32), jnp.asarray(flags), qs, base)

    cfg_ids, kstarts, cfg_blocks = _na_plan([sp] * bp + [ss] * bs, seg)
    bias_cfg = _na_bias_tables(na_bias[0], cfg_blocks)
    attn_b = _na_attn_call(proj, bias_cfg, jnp.asarray(cfg_ids), jnp.asarray(kstarts), seg, base)

    wr = jnp.zeros((d, LANES), F32).at[:, :n_e].set(w_router[0])
    wr_hi = wr.astype(BF16)
    wr_lo = (wr - wr_hi.astype(F32)).astype(BF16)
    br = jnp.zeros((1, LANES), F32).at[0, :n_e].set(b_router[0])
    x1, h2, logits = _outproj_call(attn_a, attn_b, proj, xp, xs, mod_seg, w_o_a[0].astype(BF16),
                                   w_o_b[0].astype(BF16), w_out[0].astype(BF16), g_norm2[0], wr_hi, wr_lo, br, seg)

    tm_e = min(1024, seg)
    top_logits, top_idx = lax.top_k(logits[:, :n_e], TOP_K)
    top_w = jax.nn.softmax(top_logits, axis=-1)
    n_assign = t * TOP_K
    e_flat = top_idx.reshape(-1)
    onehot = (e_flat[:, None] == jnp.arange(n_e, dtype=e_flat.dtype)[None, :]).astype(jnp.int32)
    csum = jnp.cumsum(onehot, axis=0)
    rank = jnp.take_along_axis(csum, e_flat[:, None], axis=1)[:, 0] - 1
    counts = csum[-1]
    padded = (counts + tm_e - 1) // tm_e * tm_e
    pend = jnp.cumsum(padded)
    dest = (pend - padded)[e_flat] + rank
    n_tiles = -(-n_assign // tm_e) + n_e
    n_rows = n_tiles * tm_e
    tile_e = jnp.minimum(jnp.searchsorted(pend, jnp.arange(n_tiles) * tm_e, side='right'), n_e - 1).astype(jnp.int32)
    n_valid = (pend[-1:] // tm_e).astype(jnp.int32)
    tok_flat = jnp.arange(n_assign, dtype=jnp.int32) // TOP_K
    _, tok_sorted = lax.sort((e_flat, tok_flat), num_keys=1, is_stable=True)
    slot = jnp.arange(n_rows, dtype=jnp.int32)
    slot_e = jnp.repeat(tile_e, tm_e)
    idx_in_e = slot - (pend - padded)[slot_e]
    src = jnp.minimum((jnp.cumsum(counts) - counts)[slot_e] + idx_in_e, n_assign - 1)
    xin = h2[tok_sorted[src]]

    w_gate, w_lin, w_dn = _cast_experts_call(w_gu[0], w_down[0], n_valid)
    yb = _moe_call(xin, tile_e, n_valid, w_gate, w_lin, b_gu[0], w_dn, b_down[0], tm_e)

    ysel = yb[dest.reshape(t, TOP_K).T]
    y_prompt = _combine_call(x1, ysel, top_w, mod_seg, 0, tp, seg)
    y_sample = _combine_call(x1, ysel, top_w, mod_seg, tp, ts, seg)
    return (y_prompt.reshape(bp, sp, d), y_sample.reshape(bs, ss, d))
```

```python
import functools
import math

import numpy as np
import jax
import jax.numpy as jnp
from jax import lax
from jax.experimental import pallas as pl
from jax.experimental.pallas import tpu as pltpu

F32 = jnp.float32
BF16 = jnp.bfloat16

HEAD_DIM = 64
N_HEADS_A = 16
N_KV_A = 4
GROUP_A = N_HEADS_A // N_KV_A
N_HEADS_B = 16
WINDOW = 128
ATTN_BLOCK = 128
T5_BUCKETS = 32
T5_MAX_DIST = 128
GRID_W = 64
NA_ROWS = 8
NA_COLS = 16
TOP_K = 4
SWIGLU_LIMIT = 7.0
SWIGLU_ALPHA = 1.702
RMS_EPS = 1e-6

Q_A = N_HEADS_A * HEAD_DIM
KV_A = N_KV_A * HEAD_DIM
W_B = N_HEADS_B * HEAD_DIM

LANES = 128
MXU_DIM = 256
VMEM_LIMIT_BYTES = 56 * 1024 * 1024
MASKED = -1e30

N_HEAD_PAIRS = Q_A // LANES
COL_QA = 0
COL_QB = COL_QA + Q_A
COL_KB = COL_QB + W_B
COL_KA = COL_KB + W_B
COL_VA = COL_KA + N_KV_A * LANES
COL_VB = COL_VA + N_KV_A * LANES
COL_END = COL_VB + W_B
NA_GROUP_ROWS = 2
NA_KEY_ROWS = NA_GROUP_ROWS + NA_ROWS - 1
NA_HALO_ROWS = NA_ROWS


def _sigmoid(x):
    return 1.0 / (1.0 + jnp.exp(-x))


def _cparams(sem):
    return pltpu.CompilerParams(dimension_semantics=sem, vmem_limit_bytes=VMEM_LIMIT_BYTES)


def _ada_kernel(c_ref, w_ref, b_ref, o_ref):
    c = c_ref[...]
    a = (c * _sigmoid(c)).astype(BF16)
    o_ref[...] = jnp.dot(a, w_ref[...].astype(BF16), preferred_element_type=F32) + b_ref[...]


def _ada_call(c_pad, w_ada, b_ada):
    rows, d = c_pad.shape
    n = w_ada.shape[1]
    tn = min(n, 1024)
    return pl.pallas_call(
        _ada_kernel,
        grid=(n // tn,),
        in_specs=[pl.BlockSpec((rows, d), lambda j: (0, 0)),
                  pl.BlockSpec((d, tn), lambda j: (0, j)),
                  pl.BlockSpec((1, tn), lambda j: (0, j))],
        out_specs=pl.BlockSpec((rows, tn), lambda j: (0, j)),
        out_shape=jax.ShapeDtypeStruct((rows, n), F32),
        compiler_params=_cparams(("arbitrary",)),
        name="ada_ln",
    )(c_pad, w_ada, b_ada.reshape(1, n))


def _two_group_specs(tm, d, n_prompt_tiles, n_sample_tiles):
    xp = pl.BlockSpec((tm, d), lambda i, *_: (jnp.minimum(i, n_prompt_tiles - 1), 0))
    xs = pl.BlockSpec((tm, d), lambda i, *_: (jnp.clip(i - n_prompt_tiles, 0, n_sample_tiles - 1), 0))
    return xp, xs


def _norm_mod_kernel(xp_ref, xs_ref, mod_ref, g_ref, o_ref, *, n_prompt_tiles):
    i = pl.program_id(0)
    x = jnp.where(i < n_prompt_tiles, xp_ref[...], xs_ref[...])
    y = x * lax.rsqrt(jnp.mean(x * x, axis=-1, keepdims=True) + RMS_EPS) * g_ref[...]
    m = mod_ref[0]
    o_ref[...] = (y * (1.0 + m[1:2]) + m[0:1]).astype(BF16)


def _norm_mod_call(xp, xs, mod_seg, g, seg):
    d = xp.shape[1]
    tm = min(512, seg)
    npt, nst = xp.shape[0] // tm, xs.shape[0] // tm
    xp_spec, xs_spec = _two_group_specs(tm, d, npt, nst)
    return pl.pallas_call(
        functools.partial(_norm_mod_kernel, n_prompt_tiles=npt),
        grid=(npt + nst,),
        in_specs=[xp_spec, xs_spec,
                  pl.BlockSpec((1, 6, d), lambda i: (i * tm // seg, 0, 0)),
                  pl.BlockSpec((1, d), lambda i: (0, 0))],
        out_specs=pl.BlockSpec((tm, d), lambda i: (i, 0)),
        out_shape=jax.ShapeDtypeStruct((xp.shape[0] + xs.shape[0], d), BF16),
        compiler_params=_cparams(("arbitrary",)),
        name="norm1_mod",
    )(xp, xs, mod_seg, g.reshape(1, d))


def _group_rms(y, g, ones_ref):
    sq = (y * y).astype(BF16)
    parts = []
    for c in range(y.shape[1] // MXU_DIM):
        parts.append(jnp.dot(sq[:, c * MXU_DIM:(c + 1) * MXU_DIM], ones_ref[...], preferred_element_type=F32))
    ss = jnp.concatenate(parts, axis=1) if len(parts) > 1 else parts[0]
    return y * lax.rsqrt(ss * (1.0 / HEAD_DIM) + RMS_EPS) * g


INPROJ_ROW_CHUNKS = 2


def _inproj_kernel(h_ref, w_ref, g_ref, ones_ref, o_ref, *, tn, gate_tiles):
    j = pl.program_id(1)
    half = tn // 2
    mixed_tile = gate_tiles + COL_KA // tn
    plain_tile = gate_tiles + COL_VB // tn
    rows = h_ref.shape[0] // INPROJ_ROW_CHUNKS

    def chunked(epilogue):
        for c in range(INPROJ_ROW_CHUNKS):
            r = slice(c * rows, (c + 1) * rows)
            epilogue(jnp.dot(h_ref[r, :], w_ref[...], preferred_element_type=F32), r)

    @pl.when(j < gate_tiles)
    def _():
        def epilogue(y, r):
            o_ref[r, :] = (0.5 * jnp.tanh(0.5 * y) + 0.5).astype(BF16)
        chunked(epilogue)

    @pl.when((j >= gate_tiles) & (j < mixed_tile))
    def _():
        def epilogue(y, r):
            o_ref[r, :] = _group_rms(y, g_ref[...], ones_ref).astype(BF16)
        chunked(epilogue)

    @pl.when(j == mixed_tile)
    def _():
        def epilogue(y, r):
            o_ref[r, :half] = _group_rms(y[:, :half], g_ref[:, :half], ones_ref).astype(BF16)
            o_ref[r, half:] = y[:, half:].astype(BF16)
        chunked(epilogue)

    @pl.when(j == plain_tile)
    def _():
        def epilogue(y, r):
            o_ref[r, :] = y.astype(BF16)
        chunked(epilogue)


def _inproj_call(h, w, gvec, ones_bd, tm):
    t, d = h.shape
    pw = w.shape[1]
    tn = 1024
    assert (2 * d) % tn == 0 and COL_KA % tn == 0 and COL_VA - COL_KA == tn // 2 and COL_VB % tn == 0
    return pl.pallas_call(
        functools.partial(_inproj_kernel, tn=tn, gate_tiles=2 * d // tn),
        grid=(t // tm, pw // tn),
        in_specs=[pl.BlockSpec((tm, d), lambda i, j: (i, 0)),
                  pl.BlockSpec((d, tn), lambda i, j: (0, j)),
                  pl.BlockSpec((1, tn), lambda i, j: (0, j)),
                  pl.BlockSpec((MXU_DIM, MXU_DIM), lambda i, j: (0, 0))],
        out_specs=pl.BlockSpec((tm, tn), lambda i, j: (i, j)),
        out_shape=jax.ShapeDtypeStruct((t, pw), BF16),
        compiler_params=_cparams(("arbitrary", "arbitrary")),
        name="in_proj",
    )(h, w, gvec, ones_bd)


def _aligned(x, m):
    return x if isinstance(x, int) else pl.multiple_of(x, m)


def _three_stage_pipeline(n, stage_a, stage_b, stage_c):
    assert n >= 6 and n % 2 == 0
    pairs = n // 2

    def trip(t, run_a=True, run_b=True, run_c=True):
        for k in range(2):
            if run_c:
                stage_c(2 * (t - 2) + k, k)
        for k in range(2):
            if run_b:
                stage_b(2 * (t - 1) + k, k)
        for k in range(2):
            if run_a:
                stage_a(2 * t + k, k)

    trip(0, run_b=False, run_c=False)
    trip(1, run_c=False)

    def body(t, carry):
        trip(t)
        return carry

    lax.fori_loop(2, pairs, body, 0)
    trip(pairs, run_a=False)
    trip(pairs + 1, run_a=False, run_b=False)


CAST_ROWS = 512


class _CastRider:
    def __init__(self, w2d, col_block, n_cols, n_inner, n_steps):
        rows = w2d.shape[0]
        tr = min(CAST_ROWS, rows)
        self.n_chunks = rows // tr
        assert rows % tr == 0 and self.n_chunks <= n_steps
        self.operand = w2d

        def chunk(p, s, *_):
            return jnp.minimum(p * n_inner + s, self.n_chunks - 1)

        self.in_spec = pl.BlockSpec((tr, n_cols), lambda *g: (chunk(*g), col_block))
        self.out_spec = pl.BlockSpec((tr, n_cols), lambda *g: (chunk(*g), 0))
        self.out_shape = jax.ShapeDtypeStruct((rows, n_cols), BF16)


def _cast_rider_step(step, n_chunks, pairs):
    @pl.when(step < n_chunks)
    def _():
        for src, dst in pairs:
            dst[...] = src[...].astype(BF16)


def _win_attn_kernel(flags_ref, q_ref, kp_ref, kc_ref, kn_ref, vp_ref, vc_ref, vn_ref, bias_ref, sink_ref,
                     wc_ref, o_ref, oc_ref, kcat, vcat, s_scr, e_scr, r_scr, *, qs, nb, n_cast_chunks):
    p = pl.program_id(0)
    sb = pl.program_id(1)
    _cast_rider_step(p * pl.num_programs(1) + sb, n_cast_chunks, [(wc_ref, oc_ref)])
    blk = ATTN_BLOCK
    kcat[0:blk] = kp_ref[...]
    kcat[blk:blk + qs] = kc_ref[...]
    kcat[blk + qs:] = kn_ref[...]
    vcat[0:blk] = vp_ref[...]
    vcat[blk:blk + qs] = vc_ref[...]
    vcat[blk + qs:] = vn_ref[...]
    prev_ok = flags_ref[2 * sb]
    next_ok = flags_ref[2 * sb + 1]
    low_half = lax.broadcasted_iota(jnp.int32, (blk, LANES), 1) < HEAD_DIM
    row = lax.broadcasted_iota(jnp.int32, (2 * blk, 1), 0)
    sinkcol = jnp.where(row < blk, sink_ref[2 * p], sink_ref[2 * p + 1])
    kl = blk + 2 * WINDOW

    def scores(b, slot):
        r0 = _aligned(b * blk, blk)
        q = q_ref[pl.ds(r0, blk), :]
        zero = jnp.zeros_like(q)
        q2 = jnp.concatenate([jnp.where(low_half, q, zero), jnp.where(low_half, zero, q)], axis=0)
        kw = kcat[pl.ds(r0, kl), :]
        s_scr[slot] = lax.dot_general(q2, kw, (((1,), (1,)), ((), ())), preferred_element_type=F32)

    def softmax(b, slot):
        variant = jnp.where((b == 0) & (prev_ok == 0), 1, jnp.where((b == nb - 1) & (next_ok == 0), 2, 0))
        s = s_scr[slot] + bias_ref[variant].reshape(2 * blk, kl)
        m = jnp.maximum(jnp.max(s, axis=-1, keepdims=True), sinkcol)
        e = jnp.exp(s - m)
        denom = jnp.sum(e, axis=-1, keepdims=True) + jnp.exp(sinkcol - m)
        e_scr[slot] = e.astype(BF16)
        r_scr[slot] = 1.0 / denom

    def values(b, slot):
        r0 = _aligned(b * blk, blk)
        vw = vcat[pl.ds(r0, kl), :]
        o2 = jnp.dot(e_scr[slot], vw, preferred_element_type=F32) * r_scr[slot]
        o_ref[pl.ds(r0, blk), :] = jnp.where(low_half, o2[:blk], o2[blk:]).astype(BF16)

    _three_stage_pipeline(nb, scores, softmax, values)


def _win_attn_call(proj, bias3, sink, flags, qs, base, cast_src):
    t = proj.shape[0]
    blk = ATTN_BLOCK
    nb = qs // blk
    n_super = t // qs
    rider = _CastRider(*cast_src, n_inner=n_super, n_steps=N_HEAD_PAIRS * n_super)
    n_blk_rows = t // blk
    kl = blk + 2 * WINDOW
    ka0, va0 = base + COL_KA // LANES, base + COL_VA // LANES

    def cur(col0):
        return pl.BlockSpec((qs, LANES), lambda p, s, f: (s, col0 + p // 2))

    def prev(col0):
        return pl.BlockSpec((blk, LANES), lambda p, s, f: (jnp.maximum(s * nb - 1, 0), col0 + p // 2))

    def nxt(col0):
        return pl.BlockSpec((blk, LANES), lambda p, s, f: (jnp.minimum((s + 1) * nb, n_blk_rows - 1), col0 + p // 2))

    grid_spec = pltpu.PrefetchScalarGridSpec(
        num_scalar_prefetch=1,
        grid=(N_HEAD_PAIRS, n_super),
        in_specs=[pl.BlockSpec((qs, LANES), lambda p, s, f: (s, base + COL_QA // LANES + p)),
                  prev(ka0), cur(ka0), nxt(ka0), prev(va0), cur(va0), nxt(va0),
                  pl.BlockSpec((3, 2, blk, kl), lambda p, s, f: (0, p, 0, 0)),
                  pl.BlockSpec(memory_space=pltpu.SMEM),
                  rider.in_spec],
        out_specs=[pl.BlockSpec((qs, LANES), lambda p, s, f: (s, p)), rider.out_spec],
        scratch_shapes=[pltpu.VMEM((qs + 2 * blk, LANES), BF16), pltpu.VMEM((qs + 2 * blk, LANES), BF16),
                        pltpu.VMEM((2, 2 * blk, kl), F32), pltpu.VMEM((2, 2 * blk, kl), BF16),
                        pltpu.VMEM((2, 2 * blk, 1), F32)],
    )
    return pl.pallas_call(
        functools.partial(_win_attn_kernel, qs=qs, nb=nb, n_cast_chunks=rider.n_chunks),
        grid_spec=grid_spec,
        out_shape=[jax.ShapeDtypeStruct((t, Q_A), BF16), rider.out_shape],
        compiler_params=_cparams(("arbitrary", "arbitrary")),
        name="win_attn",
    )(flags, proj, proj, proj, proj, proj, proj, proj, bias3, sink, rider.operand)


def _t5_bucket(rel):
    nb = T5_BUCKETS // 2
    max_exact = nb // 2
    ret = jnp.where(rel > 0, nb, 0)
    n = jnp.abs(rel)
    nf = jnp.maximum(n, 1).astype(jnp.float32)
    large = max_exact + (jnp.log(nf / max_exact) / math.log(T5_MAX_DIST / max_exact) * (nb - max_exact)).astype(jnp.int32)
    large = jnp.minimum(large, nb - 1)
    return ret + jnp.where(n < max_exact, n, large)


def _toeplitz(v, n_rows, n_cols):
    p = v.shape[-1]
    assert n_cols <= p - 1
    flat = jnp.tile(v, (1,) * (v.ndim - 1) + (n_rows,))[..., :n_rows * (p - 1)]
    return flat.reshape(v.shape[:-1] + (n_rows, p - 1))[..., :n_cols]


def _window_bias(t5_bias):
    blk, kl = ATTN_BLOCK, ATTN_BLOCK + 2 * WINDOW
    period = blk + kl - 1
    m = np.arange(period)
    rel_of_m = np.where(m < kl, m, m - period) - WINDOW
    per_rel = t5_bias[_t5_bucket(jnp.asarray(rel_of_m))].astype(F32).T
    bias = _toeplitz(per_rel, blk, kl)
    col = np.arange(kl)[None, :]
    band = np.abs(col - WINDOW - np.arange(blk)[:, None]) <= WINDOW
    keep = np.stack([band, band & (col >= WINDOW), band & (col < WINDOW + blk)])
    return jnp.where(keep[:, None], bias[None], MASKED)


def _na_plan(seq_lens, seg):
    qr, nkr = NA_GROUP_ROWS, NA_KEY_ROWS
    configs, cfg_ids, kstarts = {}, [], []
    tok = 0
    for s_len in seq_lens:
        rows = s_len // GRID_W
        assert rows >= nkr and rows % qr == 0
        for r in range(0, rows, qr):
            us = min(max(r - NA_ROWS // 2, 0), rows - nkr)
            rel = tuple(min(max(r + q - NA_ROWS // 2, 0), rows - NA_ROWS) - us for q in range(qr))
            key = (r - us, rel)
            cfg_ids.append(configs.setdefault(key, len(configs)))
            g_tok = tok + r * GRID_W
            seg_tok0 = (g_tok // seg) * seg
            kstarts.append(tok + us * GRID_W - seg_tok0 + NA_HALO_ROWS * GRID_W)
        tok += s_len
    blocks = [None] * len(configs)
    for (r_us, rel), c in configs.items():
        blocks[c] = [[kr - r_us - q + NA_ROWS - 1 if rel[q] <= kr < rel[q] + NA_ROWS else None
                      for kr in range(nkr)] for q in range(qr)]
    return np.asarray(cfg_ids, np.int32), np.asarray(kstarts, np.int32), blocks


def _na_bias_tables(na_bias, blocks):
    period = 2 * GRID_W - 1
    m = np.arange(period)
    dcol = np.where(m < GRID_W, m, m - period)
    per_dcol = na_bias.astype(F32)[:, :, np.clip(dcol + NA_COLS - 1, 0, 2 * NA_COLS - 2)]
    col_tab = _toeplitz(per_dcol, GRID_W, GRID_W)
    q_col, k_col = np.arange(GRID_W)[:, None], np.arange(GRID_W)[None, :]
    win_start = np.clip(q_col - NA_COLS // 2, 0, GRID_W - NA_COLS)
    col_ok = (k_col >= win_start) & (k_col < win_start + NA_COLS)
    col_tab = jnp.where(col_ok, col_tab, MASKED)
    masked_blk = jnp.full(col_tab.shape[:1] + col_tab.shape[2:], MASKED, F32)
    cfgs = []
    for cfg in blocks:
        rows = [jnp.concatenate([masked_blk if dr is None else col_tab[:, dr] for dr in qrow], axis=-1) for qrow in cfg]
        cfgs.append(jnp.concatenate(rows, axis=-2))
    return jnp.stack(cfgs)


def _na_attn_kernel(cfg_ref, ks_ref, q_ref, kp_ref, kc_ref, kn_ref, vp_ref, vc_ref, vn_ref, bias_ref,
                    wc1_ref, wc2_ref, o_ref, oc1_ref, oc2_ref, kcat, vcat, s_scr, e_scr, r_scr,
                    *, seg, n_cast_chunks):
    s_id = pl.program_id(1)
    _cast_rider_step(pl.program_id(0) * pl.num_programs(1) + s_id, n_cast_chunks,
                     [(wc1_ref, oc1_ref), (wc2_ref, oc2_ref)])
    halo = NA_HALO_ROWS * GRID_W
    m, nk = NA_GROUP_ROWS * GRID_W, NA_KEY_ROWS * GRID_W
    n_groups = seg // m
    kcat[0:halo] = kp_ref[...]
    kcat[halo:halo + seg] = kc_ref[...]
    kcat[halo + seg:] = kn_ref[...]
    vcat[0:halo] = vp_ref[...]
    vcat[halo:halo + seg] = vc_ref[...]
    vcat[halo + seg:] = vn_ref[...]
    low_half = lax.broadcasted_iota(jnp.int32, (m, LANES), 1) < HEAD_DIM

    def scores(g, slot):
        ks = pl.multiple_of(ks_ref[s_id * n_groups + g], GRID_W)
        r0 = _aligned(g * m, m)
        q = q_ref[pl.ds(r0, m), :]
        zero = jnp.zeros_like(q)
        q2 = jnp.concatenate([jnp.where(low_half, q, zero), jnp.where(low_half, zero, q)], axis=0)
        kw = kcat[pl.ds(ks, nk), :]
        s_scr[slot] = lax.dot_general(q2, kw, (((1,), (1,)), ((), ())), preferred_element_type=F32)

    def softmax(g, slot):
        cfg = cfg_ref[s_id * n_groups + g]
        s = s_scr[slot] + bias_ref[cfg].reshape(2 * m, nk)
        mx = jnp.max(s, axis=-1, keepdims=True)
        e = jnp.exp(s - mx)
        e_scr[slot] = e.astype(BF16)
        r_scr[slot] = 1.0 / jnp.sum(e, axis=-1, keepdims=True)

    def values(g, slot):
        ks = pl.multiple_of(ks_ref[s_id * n_groups + g], GRID_W)
        r0 = _aligned(g * m, m)
        vw = vcat[pl.ds(ks, nk), :]
        o2 = jnp.dot(e_scr[slot], vw, preferred_element_type=F32) * r_scr[slot]
        o_ref[pl.ds(r0, m), :] = jnp.where(low_half, o2[:m], o2[m:]).astype(BF16)

    _three_stage_pipeline(n_groups, scores, softmax, values)


def _na_attn_call(proj, bias_cfg, cfg_ids, kstarts, seg, base, cast_src1, cast_src2):
    t = proj.shape[0]
    halo = NA_HALO_ROWS * GRID_W
    n_seg = t // seg
    riders = [_CastRider(*src, n_inner=n_seg, n_steps=N_HEAD_PAIRS * n_seg) for src in (cast_src1, cast_src2)]
    assert riders[0].n_chunks == riders[1].n_chunks
    per = seg // halo
    n_halo_blocks = t // halo
    ncfg = bias_cfg.shape[0]
    m, nk = NA_GROUP_ROWS * GRID_W, NA_KEY_ROWS * GRID_W
    kb0, vb0 = base + COL_KB // LANES, base + COL_VB // LANES

    def cur(col0):
        return pl.BlockSpec((seg, LANES), lambda p, s, c, k: (s, col0 + p))

    def prev(col0):
        return pl.BlockSpec((halo, LANES), lambda p, s, c, k: (jnp.maximum(s * per - 1, 0), col0 + p))

    def nxt(col0):
        return pl.BlockSpec((halo, LANES), lambda p, s, c, k: (jnp.minimum((s + 1) * per, n_halo_blocks - 1), col0 + p))

    grid_spec = pltpu.PrefetchScalarGridSpec(
        num_scalar_prefetch=2,
        grid=(N_HEAD_PAIRS, n_seg),
        in_specs=[cur(base + COL_QB // LANES), prev(kb0), cur(kb0), nxt(kb0), prev(vb0), cur(vb0), nxt(vb0),
                  pl.BlockSpec((ncfg, 2, m, nk), lambda p, s, c, k: (0, p, 0, 0)),
                  riders[0].in_spec, riders[1].in_spec],
        out_specs=[pl.BlockSpec((seg, LANES), lambda p, s, c, k: (s, p)), riders[0].out_spec, riders[1].out_spec],
        scratch_shapes=[pltpu.VMEM((seg + 2 * halo, LANES), BF16), pltpu.VMEM((seg + 2 * halo, LANES), BF16),
                        pltpu.VMEM((2, 2 * m, nk), F32), pltpu.VMEM((2, 2 * m, nk), BF16),
                        pltpu.VMEM((2, 2 * m, 1), F32)],
    )
    return pl.pallas_call(
        functools.partial(_na_attn_kernel, seg=seg, n_cast_chunks=riders[0].n_chunks),
        grid_spec=grid_spec,
        out_shape=[jax.ShapeDtypeStruct((t, W_B), BF16), riders[0].out_shape, riders[1].out_shape],
        compiler_params=_cparams(("arbitrary", "arbitrary")),
        name="na_attn",
    )(cfg_ids, kstarts, proj, proj, proj, proj, proj, proj, proj, bias_cfg, riders[0].operand, riders[1].operand)


def _outproj_kernel(aa_ref, ab_ref, ga_ref, gb_ref, xp_ref, xs_ref, mod_ref, woa_ref, wob_ref, wout_ref,
                    g2_ref, wrh_ref, wrl_ref, br_ref, x1_ref, h2_ref, lg_ref, *, n_prompt_tiles):
    i = pl.program_id(0)
    ya = jnp.dot(aa_ref[...], woa_ref[...], preferred_element_type=F32)
    yb = jnp.dot(ab_ref[...], wob_ref[...], preferred_element_type=F32)
    merged = ga_ref[...].astype(F32) * ya + gb_ref[...].astype(F32) * yb
    z = jnp.dot(merged.astype(BF16), wout_ref[...], preferred_element_type=F32)
    x = jnp.where(i < n_prompt_tiles, xp_ref[...], xs_ref[...])
    m = mod_ref[0]
    x1 = x + m[2:3] * z
    x1_ref[...] = x1
    y = x1 * lax.rsqrt(jnp.mean(x1 * x1, axis=-1, keepdims=True) + RMS_EPS) * g2_ref[...]
    h2 = y * (1.0 + m[4:5]) + m[3:4]
    h2_hi = h2.astype(BF16)
    h2_ref[...] = h2_hi
    h2_lo = (h2 - h2_hi.astype(F32)).astype(BF16)
    lg = jnp.dot(h2_hi, wrh_ref[...], preferred_element_type=F32)
    lg = lg + jnp.dot(h2_lo, wrh_ref[...], preferred_element_type=F32)
    lg = lg + jnp.dot(h2_hi, wrl_ref[...], preferred_element_type=F32)
    lg_ref[...] = lg + br_ref[...]


def _outproj_call(attn_a, attn_b, proj, xp, xs, mod_seg, woa, wob, wout, g2, wr_hi, wr_lo, br, seg):
    t = attn_a.shape[0]
    d = xp.shape[1]
    tm = min(256, seg)
    npt, nst = xp.shape[0] // tm, xs.shape[0] // tm
    xp_spec, xs_spec = _two_group_specs(tm, d, npt, nst)
    def const(shape):
        return pl.BlockSpec(shape, lambda i: (0,) * len(shape), pipeline_mode=pl.Buffered(1))

    return pl.pallas_call(
        functools.partial(_outproj_kernel, n_prompt_tiles=npt),
        grid=(npt + nst,),
        in_specs=[pl.BlockSpec((tm, Q_A), lambda i: (i, 0)),
                  pl.BlockSpec((tm, W_B), lambda i: (i, 0)),
                  pl.BlockSpec((tm, d), lambda i: (i, 0)),
                  pl.BlockSpec((tm, d), lambda i: (i, 1)),
                  xp_spec, xs_spec,
                  pl.BlockSpec((1, 6, d), lambda i: (i * tm // seg, 0, 0)),
                  const((Q_A, d)), const((W_B, d)), const((d, d)), const((1, d)),
                  const((d, LANES)), const((d, LANES)), const((1, LANES))],
        out_specs=[pl.BlockSpec((tm, d), lambda i: (i, 0)),
                   pl.BlockSpec((tm, d), lambda i: (i, 0)),
                   pl.BlockSpec((tm, LANES), lambda i: (i, 0))],
        out_shape=[jax.ShapeDtypeStruct((t, d), F32),
                   jax.ShapeDtypeStruct((t, d), BF16),
                   jax.ShapeDtypeStruct((t, LANES), F32)],
        compiler_params=_cparams(("arbitrary",)),
        name="out_proj",
    )(attn_a, attn_b, proj, proj, xp, xs, mod_seg, woa, wob, wout, g2.reshape(1, d), wr_hi, wr_lo, br)


def _moe_kernel(te_ref, nv_ref, x_ref, wg_ref, wl_ref, bg_ref, bl_ref, wd_ref, bd_ref, o_ref, acc_ref, *, nf):
    t = pl.program_id(0)
    f = pl.program_id(1)
    live = t < nv_ref[0]

    @pl.when((t == 0) & (f == 0))
    def _():
        acc_ref[...] = jnp.zeros_like(acc_ref)

    @pl.when(live)
    def _():
        x = x_ref[...]
        gate = jnp.dot(x, wg_ref[...], preferred_element_type=F32) + bg_ref[0]
        lin = jnp.dot(x, wl_ref[...], preferred_element_type=F32) + bl_ref[0]
        gate = jnp.minimum(gate, SWIGLU_LIMIT)
        lin = jnp.clip(lin, -SWIGLU_LIMIT, SWIGLU_LIMIT)
        act = gate * _sigmoid(SWIGLU_ALPHA * gate) * (lin + 1.0)
        part = jnp.dot(act.astype(BF16), wd_ref[...], preferred_element_type=F32)
        total = part + jnp.where(f == 0, bd_ref[0], acc_ref[...])
        acc_ref[...] = total
        o_ref[...] = total.astype(BF16)

    @pl.when(jnp.logical_not(live) & (f == 0))
    def _():
        o_ref[...] = jnp.zeros_like(o_ref)


def _moe_call(xin, tile_e, n_valid, w_gate, w_lin, b_gu, w_down, b_down, tm):
    n_rows, d = xin.shape
    n_e, two_ff = b_gu.shape
    d_ff = two_ff // 2
    tf = min(512, d_ff)
    nf = d_ff // tf
    n_tiles = n_rows // tm

    def fidx(t, f, nv):
        return jnp.where(t < nv[0], f, nf - 1)

    grid_spec = pltpu.PrefetchScalarGridSpec(
        num_scalar_prefetch=2,
        grid=(n_tiles, nf),
        in_specs=[pl.BlockSpec((tm, d), lambda t, f, te, nv: (t, 0)),
                  pl.BlockSpec((d, tf), lambda t, f, te, nv: (te[t], fidx(t, f, nv))),
                  pl.BlockSpec((d, tf), lambda t, f, te, nv: (te[t], fidx(t, f, nv))),
                  pl.BlockSpec((1, 1, tf), lambda t, f, te, nv: (te[t], 0, fidx(t, f, nv))),
                  pl.BlockSpec((1, 1, tf), lambda t, f, te, nv: (te[t], 0, fidx(t, f, nv) + nf)),
                  pl.BlockSpec((tf, d), lambda t, f, te, nv: (te[t] * nf + fidx(t, f, nv), 0)),
                  pl.BlockSpec((1, 1, d), lambda t, f, te, nv: (te[t], 0, 0))],
        out_specs=pl.BlockSpec((tm, d), lambda t, f, te, nv: (t, 0)),
        scratch_shapes=[pltpu.VMEM((tm, d), F32)],
    )
    return pl.pallas_call(
        functools.partial(_moe_kernel, nf=nf),
        grid_spec=grid_spec,
        out_shape=jax.ShapeDtypeStruct((n_rows, d), BF16),
        compiler_params=_cparams(("arbitrary", "arbitrary")),
        name="moe_experts",
    )(tile_e, n_valid, xin, w_gate, w_lin, b_gu.reshape(n_e, 1, two_ff), b_gu.reshape(n_e, 1, two_ff),
      w_down, b_down.reshape(n_e, 1, d))


def _combine_kernel(x1_ref, ys_ref, w_ref, mod_ref, o_ref):
    w = w_ref[...]
    acc = w[:, 0:1] * ys_ref[0].astype(F32)
    for k in range(1, TOP_K):
        acc = acc + w[:, k:k + 1] * ys_ref[k].astype(F32)
    o_ref[...] = x1_ref[...] + mod_ref[0][5:6] * acc


def _combine_call(x1, ysel, top_w, mod_seg, row0, n_rows, seg):
    d = x1.shape[1]
    tm = min(512, seg)
    off = row0 // tm
    return pl.pallas_call(
        _combine_kernel,
        grid=(n_rows // tm,),
        in_specs=[pl.BlockSpec((tm, d), lambda i: (i + off, 0)),
                  pl.BlockSpec((TOP_K, tm, d), lambda i: (0, i + off, 0)),
                  pl.BlockSpec((tm, TOP_K), lambda i: (i + off, 0)),
                  pl.BlockSpec((1, 6, d), lambda i: ((i + off) * tm // seg, 0, 0))],
        out_specs=pl.BlockSpec((tm, d), lambda i: (i, 0)),
        out_shape=jax.ShapeDtypeStruct((n_rows, d), F32),
        compiler_params=_cparams(("arbitrary",)),
        name="moe_combine",
    )(x1, ysel, top_w, mod_seg)


def _pack_w_in(w_in, g_q_a, g_k_a, g_q_b, g_k_b):
    o1 = Q_A
    o2 = o1 + KV_A
    o3 = o2 + KV_A
    o4 = o3 + W_B
    o5 = o4 + W_B
    o6 = o5 + W_B
    d = w_in.shape[0]
    wqa, wka, wva = w_in[:, :o1], w_in[:, o1:o2], w_in[:, o2:o3]
    wqb, wkb, wvb, wg = w_in[:, o3:o4], w_in[:, o4:o5], w_in[:, o5:o6], w_in[:, o6:]

    def dup(w):
        w4 = w.reshape(d, N_KV_A, 1, HEAD_DIM)
        return jnp.broadcast_to(w4, (d, N_KV_A, 2, HEAD_DIM)).reshape(d, N_KV_A * LANES)

    w = jnp.concatenate([wg, wqa, wqb, wkb, dup(wka), dup(wva), wvb], axis=1).astype(BF16)
    q_scale = HEAD_DIM ** -0.5
    gvec = jnp.concatenate([jnp.ones((2 * d,), F32),
                            jnp.tile(g_q_a * q_scale, N_HEADS_A), jnp.tile(g_q_b * q_scale, N_HEADS_B),
                            jnp.tile(g_k_b, N_HEADS_B), jnp.tile(g_k_a, 2 * N_KV_A),
                            jnp.ones((COL_END - COL_VA,), F32)]).reshape(1, -1).astype(F32)
    return w, gvec


def kernel(x_prompt, x_sample, c_prompt, c_sample, w_ada, b_ada, g_norm1, w_in, g_q_a, g_k_a, g_q_b, g_k_b, sink_a, t5_bias, na_bias, w_o_a, w_o_b, w_out, g_norm2, w_router, b_router, w_gu, b_gu, w_down, b_down):
    bp, sp, d = x_prompt.shape
    bs, ss, _ = x_sample.shape
    n_e = w_router.shape[-1]
    tp, ts = bp * sp, bs * ss
    t = tp + ts
    seg = math.gcd(math.gcd(sp, ss), 2048)
    assert w_ada.shape[0] == 1 and sp % seg == 0 and ss % seg == 0 and seg % (NA_HALO_ROWS * GRID_W) == 0
    assert 2 * d == w_in.shape[-1] - (Q_A + 2 * KV_A + 3 * W_B)
    base = 2 * d // LANES
    xp = x_prompt.reshape(tp, d)
    xs = x_sample.reshape(ts, d)

    n_c = bp + bs
    c_pad = jnp.zeros((-(-n_c // 16) * 16, d), F32).at[:n_c].set(jnp.concatenate([c_prompt, c_sample], axis=0))
    mod = _ada_call(c_pad, w_ada[0], b_ada[0])[:n_c].reshape(n_c, 6, d)
    seg_owner = np.concatenate([np.repeat(np.arange(bp), sp // seg), bp + np.repeat(np.arange(bs), ss // seg)])
    mod_seg = mod[seg_owner]

    h1 = _norm_mod_call(xp, xs, mod_seg, g_norm1[0], seg)

    w_slab, gvec = _pack_w_in(w_in[0], g_q_a[0], g_k_a[0], g_q_b[0], g_k_b[0])
    grp = np.arange(MXU_DIM) // HEAD_DIM
    ones_bd = jnp.asarray(grp[:, None] == grp[None, :], BF16)
    proj = _inproj_call(h1, w_slab, gvec, ones_bd, tm=min(1024, seg))

    qs = min(2048, seg)
    sb_tok = np.arange(t // qs) * qs
    seq_start = np.where(sb_tok < tp, sb_tok // sp * sp, tp + (sb_tok - tp) // ss * ss)
    seq_len = np.where(sb_tok < tp, sp, ss)
    flags = np.stack([sb_tok != seq_start, sb_tok + qs != seq_start + seq_len], axis=1).astype(np.int32).reshape(-1)
    n_e, d_ff = w_down.shape[1], w_down.shape[2]
    assert d_ff == d and w_gu.shape[1:] == (n_e, d, 2 * d_ff)
    w_gu2d = w_gu[0].reshape(n_e * d, 2 * d_ff)
    w_down2d = w_down[0].reshape(n_e * d_ff, d)
    attn_a, w_gate = _win_attn_call(proj, _window_bias(t5_bias), sink_a[0].astype(F32), jnp.asarray(flags), qs, base,
                                    (w_gu2d, 0, d_ff))

    cfg_ids, kstarts, cfg_blocks = _na_plan([sp] * bp + [ss] * bs, seg)
    bias_cfg = _na_bias_tables(na_bias[0], cfg_blocks)
    attn_b, w_lin, w_dn = _na_attn_call(proj, bias_cfg, jnp.asarray(cfg_ids), jnp.asarray(kstarts), seg, base,
                                        (w_gu2d, 1, d_ff), (w_down2d, 0, d))

    wr = jnp.zeros((d, LANES), F32).at[:, :n_e].set(w_router[0])
    wr_hi = wr.astype(BF16)
    wr_lo = (wr - wr_hi.astype(F32)).astype(BF16)
    br = jnp.zeros((1, LANES), F32).at[0, :n_e].set(b_router[0])
    x1, h2, logits = _outproj_call(attn_a, attn_b, proj, xp, xs, mod_seg, w_o_a[0].astype(BF16),
                                   w_o_b[0].astype(BF16), w_out[0].astype(BF16), g_norm2[0], wr_hi, wr_lo, br, seg)

    tm_e = min(1024, seg)
    top_logits, top_idx = lax.top_k(logits[:, :n_e], TOP_K)
    top_w = jax.nn.softmax(top_logits, axis=-1)
    n_assign = t * TOP_K
    e_flat = top_idx.reshape(-1)
    onehot = (e_flat[:, None] == jnp.arange(n_e, dtype=e_flat.dtype)[None, :]).astype(jnp.int32)
    csum = jnp.cumsum(onehot, axis=0)
    rank = jnp.take_along_axis(csum, e_flat[:, None], axis=1)[:, 0] - 1
    counts = csum[-1]
    padded = (counts + tm_e - 1) // tm_e * tm_e
    pend = jnp.cumsum(padded)
    dest = (pend - padded)[e_flat] + rank
    n_tiles = -(-n_assign // tm_e) + n_e
    n_rows = n_tiles * tm_e
    tile_e = jnp.minimum(jnp.searchsorted(pend, jnp.arange(n_tiles) * tm_e, side='right'), n_e - 1).astype(jnp.int32)
    n_valid = (pend[-1:] // tm_e).astype(jnp.int32)
    tok_flat = jnp.arange(n_assign, dtype=jnp.int32) // TOP_K
    _, tok_sorted = lax.sort((e_flat, tok_flat), num_keys=1, is_stable=True)
    slot = jnp.arange(n_rows, dtype=jnp.int32)
    slot_e = jnp.repeat(tile_e, tm_e)
    idx_in_e = slot - (pend - padded)[slot_e]
    src = jnp.minimum((jnp.cumsum(counts) - counts)[slot_e] + idx_in_e, n_assign - 1)
    xin = h2[tok_sorted[src]]

    yb = _moe_call(xin, tile_e, n_valid, w_gate, w_lin, b_gu[0], w_dn, b_down[0], tm_e)

    ysel = yb[dest.reshape(t, TOP_K).T]
    y_prompt = _combine_call(x1, ysel, top_w, mod_seg, 0, tp, seg)
    y_sample = _combine_call(x1, ysel, top_w, mod_seg, tp, ts, seg)
    return (y_prompt.reshape(bp, sp, d), y_sample.reshape(bs, ss, d))
```

```python
import functools
import math

import numpy as np
import jax
import jax.numpy as jnp
from jax import lax
from jax.experimental import pallas as pl
from jax.experimental.pallas import tpu as pltpu

F32 = jnp.float32
BF16 = jnp.bfloat16

HEAD_DIM = 64
N_HEADS_A = 16
N_KV_A = 4
GROUP_A = N_HEADS_A // N_KV_A
N_HEADS_B = 16
WINDOW = 128
ATTN_BLOCK = 128
T5_BUCKETS = 32
T5_MAX_DIST = 128
GRID_W = 64
NA_ROWS = 8
NA_COLS = 16
TOP_K = 4
SWIGLU_LIMIT = 7.0
SWIGLU_ALPHA = 1.702
RMS_EPS = 1e-6

Q_A = N_HEADS_A * HEAD_DIM
KV_A = N_KV_A * HEAD_DIM
W_B = N_HEADS_B * HEAD_DIM

LANES = 128
MXU_DIM = 256
VMEM_LIMIT_BYTES = 56 * 1024 * 1024
MASKED = -1e30

N_HEAD_PAIRS = Q_A // LANES
COL_QA = 0
COL_QB = COL_QA + Q_A
COL_KB = COL_QB + W_B
COL_KA = COL_KB + W_B
COL_VA = COL_KA + N_KV_A * LANES
COL_VB = COL_VA + N_KV_A * LANES
COL_END = COL_VB + W_B
NA_GROUP_ROWS = 2
NA_KEY_ROWS = NA_GROUP_ROWS + NA_ROWS - 1
NA_HALO_ROWS = NA_ROWS


def _sigmoid(x):
    return 1.0 / (1.0 + jnp.exp(-x))


def _cparams(sem):
    return pltpu.CompilerParams(dimension_semantics=sem, vmem_limit_bytes=VMEM_LIMIT_BYTES)


def _ada_kernel(c_ref, w_ref, b_ref, o_ref):
    c = c_ref[...]
    a = (c * _sigmoid(c)).astype(BF16)
    o_ref[...] = jnp.dot(a, w_ref[...].astype(BF16), preferred_element_type=F32) + b_ref[...]


def _ada_call(c_pad, w_ada, b_ada):
    rows, d = c_pad.shape
    n = w_ada.shape[1]
    tn = min(n, 1024)
    return pl.pallas_call(
        _ada_kernel,
        grid=(n // tn,),
        in_specs=[pl.BlockSpec((rows, d), lambda j: (0, 0)),
                  pl.BlockSpec((d, tn), lambda j: (0, j)),
                  pl.BlockSpec((1, tn), lambda j: (0, j))],
        out_specs=pl.BlockSpec((rows, tn), lambda j: (0, j)),
        out_shape=jax.ShapeDtypeStruct((rows, n), F32),
        compiler_params=_cparams(("arbitrary",)),
        name="ada_ln",
    )(c_pad, w_ada, b_ada.reshape(1, n))


def _two_group_specs(tm, d, n_prompt_tiles, n_sample_tiles):
    xp = pl.BlockSpec((tm, d), lambda i, *_: (jnp.minimum(i, n_prompt_tiles - 1), 0))
    xs = pl.BlockSpec((tm, d), lambda i, *_: (jnp.clip(i - n_prompt_tiles, 0, n_sample_tiles - 1), 0))
    return xp, xs


def _norm_mod_kernel(xp_ref, xs_ref, mod_ref, g_ref, o_ref, *, n_prompt_tiles):
    i = pl.program_id(0)
    x = jnp.where(i < n_prompt_tiles, xp_ref[...], xs_ref[...])
    y = x * lax.rsqrt(jnp.mean(x * x, axis=-1, keepdims=True) + RMS_EPS) * g_ref[...]
    m = mod_ref[0]
    o_ref[...] = (y * (1.0 + m[1:2]) + m[0:1]).astype(BF16)


def _norm_mod_call(xp, xs, mod_seg, g, seg):
    d = xp.shape[1]
    tm = min(512, seg)
    npt, nst = xp.shape[0] // tm, xs.shape[0] // tm
    xp_spec, xs_spec = _two_group_specs(tm, d, npt, nst)
    return pl.pallas_call(
        functools.partial(_norm_mod_kernel, n_prompt_tiles=npt),
        grid=(npt + nst,),
        in_specs=[xp_spec, xs_spec,
                  pl.BlockSpec((1, 6, d), lambda i: (i * tm // seg, 0, 0)),
                  pl.BlockSpec((1, d), lambda i: (0, 0))],
        out_specs=pl.BlockSpec((tm, d), lambda i: (i, 0)),
        out_shape=jax.ShapeDtypeStruct((xp.shape[0] + xs.shape[0], d), BF16),
        compiler_params=_cparams(("arbitrary",)),
        name="norm1_mod",
    )(xp, xs, mod_seg, g.reshape(1, d))


def _group_rms(y, g, ones_ref):
    sq = (y * y).astype(BF16)
    parts = []
    for c in range(y.shape[1] // MXU_DIM):
        parts.append(jnp.dot(sq[:, c * MXU_DIM:(c + 1) * MXU_DIM], ones_ref[...], preferred_element_type=F32))
    ss = jnp.concatenate(parts, axis=1) if len(parts) > 1 else parts[0]
    return y * lax.rsqrt(ss * (1.0 / HEAD_DIM) + RMS_EPS) * g


INPROJ_ROW_CHUNKS = 2


def _inproj_kernel(h_ref, w_ref, g_ref, ones_ref, o_ref, *, tn, gate_tiles):
    j = pl.program_id(1)
    half = tn // 2
    mixed_tile = gate_tiles + COL_KA // tn
    plain_tile = gate_tiles + COL_VB // tn
    rows = h_ref.shape[0] // INPROJ_ROW_CHUNKS

    def chunked(epilogue):
        for c in range(INPROJ_ROW_CHUNKS):
            r = slice(c * rows, (c + 1) * rows)
            epilogue(jnp.dot(h_ref[r, :], w_ref[...], preferred_element_type=F32), r)

    @pl.when(j < gate_tiles)
    def _():
        def epilogue(y, r):
            o_ref[r, :] = (0.5 * jnp.tanh(0.5 * y) + 0.5).astype(BF16)
        chunked(epilogue)

    @pl.when((j >= gate_tiles) & (j < mixed_tile))
    def _():
        def epilogue(y, r):
            o_ref[r, :] = _group_rms(y, g_ref[...], ones_ref).astype(BF16)
        chunked(epilogue)

    @pl.when(j == mixed_tile)
    def _():
        def epilogue(y, r):
            o_ref[r, :half] = _group_rms(y[:, :half], g_ref[:, :half], ones_ref).astype(BF16)
            o_ref[r, half:] = y[:, half:].astype(BF16)
        chunked(epilogue)

    @pl.when(j == plain_tile)
    def _():
        def epilogue(y, r):
            o_ref[r, :] = y.astype(BF16)
        chunked(epilogue)


def _inproj_call(h, w, gvec, ones_bd, tm):
    t, d = h.shape
    pw = w.shape[1]
    tn = 1024
    assert (2 * d) % tn == 0 and COL_KA % tn == 0 and COL_VA - COL_KA == tn // 2 and COL_VB % tn == 0
    return pl.pallas_call(
        functools.partial(_inproj_kernel, tn=tn, gate_tiles=2 * d // tn),
        grid=(t // tm, pw // tn),
        in_specs=[pl.BlockSpec((tm, d), lambda i, j: (i, 0)),
                  pl.BlockSpec((d, tn), lambda i, j: (0, j)),
                  pl.BlockSpec((1, tn), lambda i, j: (0, j)),
                  pl.BlockSpec((MXU_DIM, MXU_DIM), lambda i, j: (0, 0))],
        out_specs=pl.BlockSpec((tm, tn), lambda i, j: (i, j)),
        out_shape=jax.ShapeDtypeStruct((t, pw), BF16),
        compiler_params=_cparams(("arbitrary", "arbitrary")),
        name="in_proj",
    )(h, w, gvec, ones_bd)


def _aligned(x, m):
    return x if isinstance(x, int) else pl.multiple_of(x, m)


def _three_stage_pipeline(n, stage_a, stage_b, stage_c):
    assert n >= 6 and n % 2 == 0
    pairs = n // 2

    def trip(t, run_a=True, run_b=True, run_c=True):
        for k in range(2):
            if run_c:
                stage_c(2 * (t - 2) + k, k)
        for k in range(2):
            if run_b:
                stage_b(2 * (t - 1) + k, k)
        for k in range(2):
            if run_a:
                stage_a(2 * t + k, k)

    trip(0, run_b=False, run_c=False)
    trip(1, run_c=False)

    def body(t, carry):
        trip(t)
        return carry

    lax.fori_loop(2, pairs, body, 0)
    trip(pairs, run_a=False)
    trip(pairs + 1, run_a=False, run_b=False)


CAST_ROWS = 512
MOE_CHUNKS = 4


class _CastRider:
    def __init__(self, w2d, col_block, n_cols, n_inner, n_steps):
        rows = w2d.shape[0]
        tr = min(CAST_ROWS, rows)
        self.n_chunks = rows // tr
        assert rows % tr == 0 and self.n_chunks <= n_steps
        self.operand = w2d

        def chunk(p, s, *_):
            return jnp.minimum(p * n_inner + s, self.n_chunks - 1)

        self.in_spec = pl.BlockSpec((tr, n_cols), lambda *g: (chunk(*g), col_block))
        self.out_spec = pl.BlockSpec((tr, n_cols), lambda *g: (chunk(*g), 0))
        self.out_shape = jax.ShapeDtypeStruct((rows, n_cols), BF16)


def _cast_rider_step(step, n_chunks, pairs):
    @pl.when(step < n_chunks)
    def _():
        for src, dst in pairs:
            dst[...] = src[...].astype(BF16)


def _win_attn_kernel(flags_ref, q_ref, kp_ref, kc_ref, kn_ref, vp_ref, vc_ref, vn_ref, bias_ref, sink_ref,
                     wc_ref, o_ref, oc_ref, kcat, vcat, s_scr, e_scr, r_scr, *, qs, nb, n_cast_chunks):
    p = pl.program_id(0)
    sb = pl.program_id(1)
    _cast_rider_step(p * pl.num_programs(1) + sb, n_cast_chunks, [(wc_ref, oc_ref)])
    blk = ATTN_BLOCK
    kcat[0:blk] = kp_ref[...]
    kcat[blk:blk + qs] = kc_ref[...]
    kcat[blk + qs:] = kn_ref[...]
    vcat[0:blk] = vp_ref[...]
    vcat[blk:blk + qs] = vc_ref[...]
    vcat[blk + qs:] = vn_ref[...]
    prev_ok = flags_ref[2 * sb]
    next_ok = flags_ref[2 * sb + 1]
    low_half = lax.broadcasted_iota(jnp.int32, (blk, LANES), 1) < HEAD_DIM
    row = lax.broadcasted_iota(jnp.int32, (2 * blk, 1), 0)
    sinkcol = jnp.where(row < blk, sink_ref[2 * p], sink_ref[2 * p + 1])
    kl = blk + 2 * WINDOW

    def scores(b, slot):
        r0 = _aligned(b * blk, blk)
        q = q_ref[pl.ds(r0, blk), :]
        zero = jnp.zeros_like(q)
        q2 = jnp.concatenate([jnp.where(low_half, q, zero), jnp.where(low_half, zero, q)], axis=0)
        kw = kcat[pl.ds(r0, kl), :]
        s_scr[slot] = lax.dot_general(q2, kw, (((1,), (1,)), ((), ())), preferred_element_type=F32)

    def softmax(b, slot):
        variant = jnp.where((b == 0) & (prev_ok == 0), 1, jnp.where((b == nb - 1) & (next_ok == 0), 2, 0))
        s = s_scr[slot] + bias_ref[variant].reshape(2 * blk, kl)
        m = jnp.maximum(jnp.max(s, axis=-1, keepdims=True), sinkcol)
        e = jnp.exp(s - m)
        denom = jnp.sum(e, axis=-1, keepdims=True) + jnp.exp(sinkcol - m)
        e_scr[slot] = e.astype(BF16)
        r_scr[slot] = 1.0 / denom

    def values(b, slot):
        r0 = _aligned(b * blk, blk)
        vw = vcat[pl.ds(r0, kl), :]
        o2 = jnp.dot(e_scr[slot], vw, preferred_element_type=F32) * r_scr[slot]
        o_ref[pl.ds(r0, blk), :] = jnp.where(low_half, o2[:blk], o2[blk:]).astype(BF16)

    _three_stage_pipeline(nb, scores, softmax, values)


def _win_attn_call(proj, bias3, sink, flags, qs, base, cast_src):
    t = proj.shape[0]
    blk = ATTN_BLOCK
    nb = qs // blk
    n_super = t // qs
    rider = _CastRider(*cast_src, n_inner=n_super, n_steps=N_HEAD_PAIRS * n_super)
    n_blk_rows = t // blk
    kl = blk + 2 * WINDOW
    ka0, va0 = base + COL_KA // LANES, base + COL_VA // LANES

    def cur(col0):
        return pl.BlockSpec((qs, LANES), lambda p, s, f: (s, col0 + p // 2))

    def prev(col0):
        return pl.BlockSpec((blk, LANES), lambda p, s, f: (jnp.maximum(s * nb - 1, 0), col0 + p // 2))

    def nxt(col0):
        return pl.BlockSpec((blk, LANES), lambda p, s, f: (jnp.minimum((s + 1) * nb, n_blk_rows - 1), col0 + p // 2))

    grid_spec = pltpu.PrefetchScalarGridSpec(
        num_scalar_prefetch=1,
        grid=(N_HEAD_PAIRS, n_super),
        in_specs=[pl.BlockSpec((qs, LANES), lambda p, s, f: (s, base + COL_QA // LANES + p)),
                  prev(ka0), cur(ka0), nxt(ka0), prev(va0), cur(va0), nxt(va0),
                  pl.BlockSpec((3, 2, blk, kl), lambda p, s, f: (0, p, 0, 0)),
                  pl.BlockSpec(memory_space=pltpu.SMEM),
                  rider.in_spec],
        out_specs=[pl.BlockSpec((qs, LANES), lambda p, s, f: (s, p)), rider.out_spec],
        scratch_shapes=[pltpu.VMEM((qs + 2 * blk, LANES), BF16), pltpu.VMEM((qs + 2 * blk, LANES), BF16),
                        pltpu.VMEM((2, 2 * blk, kl), F32), pltpu.VMEM((2, 2 * blk, kl), BF16),
                        pltpu.VMEM((2, 2 * blk, 1), F32)],
    )
    return pl.pallas_call(
        functools.partial(_win_attn_kernel, qs=qs, nb=nb, n_cast_chunks=rider.n_chunks),
        grid_spec=grid_spec,
        out_shape=[jax.ShapeDtypeStruct((t, Q_A), BF16), rider.out_shape],
        compiler_params=_cparams(("arbitrary", "arbitrary")),
        name="win_attn",
    )(flags, proj, proj, proj, proj, proj, proj, proj, bias3, sink, rider.operand)


def _t5_bucket(rel):
    nb = T5_BUCKETS // 2
    max_exact = nb // 2
    ret = jnp.where(rel > 0, nb, 0)
    n = jnp.abs(rel)
    nf = jnp.maximum(n, 1).astype(jnp.float32)
    large = max_exact + (jnp.log(nf / max_exact) / math.log(T5_MAX_DIST / max_exact) * (nb - max_exact)).astype(jnp.int32)
    large = jnp.minimum(large, nb - 1)
    return ret + jnp.where(n < max_exact, n, large)


def _toeplitz(v, n_rows, n_cols):
    p = v.shape[-1]
    assert n_cols <= p - 1
    flat = jnp.tile(v, (1,) * (v.ndim - 1) + (n_rows,))[..., :n_rows * (p - 1)]
    return flat.reshape(v.shape[:-1] + (n_rows, p - 1))[..., :n_cols]


def _window_bias(t5_bias):
    blk, kl = ATTN_BLOCK, ATTN_BLOCK + 2 * WINDOW
    period = blk + kl - 1
    m = np.arange(period)
    rel_of_m = np.where(m < kl, m, m - period) - WINDOW
    per_rel = t5_bias[_t5_bucket(jnp.asarray(rel_of_m))].astype(F32).T
    bias = _toeplitz(per_rel, blk, kl)
    col = np.arange(kl)[None, :]
    band = np.abs(col - WINDOW - np.arange(blk)[:, None]) <= WINDOW
    keep = np.stack([band, band & (col >= WINDOW), band & (col < WINDOW + blk)])
    return jnp.where(keep[:, None], bias[None], MASKED)


def _na_plan(seq_lens, seg):
    qr, nkr = NA_GROUP_ROWS, NA_KEY_ROWS
    configs, cfg_ids, kstarts = {}, [], []
    tok = 0
    for s_len in seq_lens:
        rows = s_len // GRID_W
        assert rows >= nkr and rows % qr == 0
        for r in range(0, rows, qr):
            us = min(max(r - NA_ROWS // 2, 0), rows - nkr)
            rel = tuple(min(max(r + q - NA_ROWS // 2, 0), rows - NA_ROWS) - us for q in range(qr))
            key = (r - us, rel)
            cfg_ids.append(configs.setdefault(key, len(configs)))
            g_tok = tok + r * GRID_W
            seg_tok0 = (g_tok // seg) * seg
            kstarts.append(tok + us * GRID_W - seg_tok0 + NA_HALO_ROWS * GRID_W)
        tok += s_len
    blocks = [None] * len(configs)
    for (r_us, rel), c in configs.items():
        blocks[c] = [[kr - r_us - q + NA_ROWS - 1 if rel[q] <= kr < rel[q] + NA_ROWS else None
                      for kr in range(nkr)] for q in range(qr)]
    return np.asarray(cfg_ids, np.int32), np.asarray(kstarts, np.int32), blocks


def _na_bias_tables(na_bias, blocks):
    period = 2 * GRID_W - 1
    m = np.arange(period)
    dcol = np.where(m < GRID_W, m, m - period)
    per_dcol = na_bias.astype(F32)[:, :, np.clip(dcol + NA_COLS - 1, 0, 2 * NA_COLS - 2)]
    col_tab = _toeplitz(per_dcol, GRID_W, GRID_W)
    q_col, k_col = np.arange(GRID_W)[:, None], np.arange(GRID_W)[None, :]
    win_start = np.clip(q_col - NA_COLS // 2, 0, GRID_W - NA_COLS)
    col_ok = (k_col >= win_start) & (k_col < win_start + NA_COLS)
    col_tab = jnp.where(col_ok, col_tab, MASKED)
    masked_blk = jnp.full(col_tab.shape[:1] + col_tab.shape[2:], MASKED, F32)
    cfgs = []
    for cfg in blocks:
        rows = [jnp.concatenate([masked_blk if dr is None else col_tab[:, dr] for dr in qrow], axis=-1) for qrow in cfg]
        cfgs.append(jnp.concatenate(rows, axis=-2))
    return jnp.stack(cfgs)


def _na_attn_kernel(cfg_ref, ks_ref, q_ref, kp_ref, kc_ref, kn_ref, vp_ref, vc_ref, vn_ref, bias_ref,
                    wc1_ref, wc2_ref, o_ref, oc1_ref, oc2_ref, kcat, vcat, s_scr, e_scr, r_scr,
                    *, seg, n_cast_chunks):
    s_id = pl.program_id(1)
    _cast_rider_step(pl.program_id(0) * pl.num_programs(1) + s_id, n_cast_chunks,
                     [(wc1_ref, oc1_ref), (wc2_ref, oc2_ref)])
    halo = NA_HALO_ROWS * GRID_W
    m, nk = NA_GROUP_ROWS * GRID_W, NA_KEY_ROWS * GRID_W
    n_groups = seg // m
    kcat[0:halo] = kp_ref[...]
    kcat[halo:halo + seg] = kc_ref[...]
    kcat[halo + seg:] = kn_ref[...]
    vcat[0:halo] = vp_ref[...]
    vcat[halo:halo + seg] = vc_ref[...]
    vcat[halo + seg:] = vn_ref[...]
    low_half = lax.broadcasted_iota(jnp.int32, (m, LANES), 1) < HEAD_DIM

    def scores(g, slot):
        ks = pl.multiple_of(ks_ref[s_id * n_groups + g], GRID_W)
        r0 = _aligned(g * m, m)
        q = q_ref[pl.ds(r0, m), :]
        zero = jnp.zeros_like(q)
        q2 = jnp.concatenate([jnp.where(low_half, q, zero), jnp.where(low_half, zero, q)], axis=0)
        kw = kcat[pl.ds(ks, nk), :]
        s_scr[slot] = lax.dot_general(q2, kw, (((1,), (1,)), ((), ())), preferred_element_type=F32)

    def softmax(g, slot):
        cfg = cfg_ref[s_id * n_groups + g]
        s = s_scr[slot] + bias_ref[cfg].reshape(2 * m, nk)
        mx = jnp.max(s, axis=-1, keepdims=True)
        e = jnp.exp(s - mx)
        e_scr[slot] = e.astype(BF16)
        r_scr[slot] = 1.0 / jnp.sum(e, axis=-1, keepdims=True)

    def values(g, slot):
        ks = pl.multiple_of(ks_ref[s_id * n_groups + g], GRID_W)
        r0 = _aligned(g * m, m)
        vw = vcat[pl.ds(ks, nk), :]
        o2 = jnp.dot(e_scr[slot], vw, preferred_element_type=F32) * r_scr[slot]
        o_ref[pl.ds(r0, m), :] = jnp.where(low_half, o2[:m], o2[m:]).astype(BF16)

    _three_stage_pipeline(n_groups, scores, softmax, values)


def _na_attn_call(proj, bias_cfg, cfg_ids, kstarts, seg, base, cast_src1, cast_src2):
    t = proj.shape[0]
    halo = NA_HALO_ROWS * GRID_W
    n_seg = t // seg
    riders = [_CastRider(*src, n_inner=n_seg, n_steps=N_HEAD_PAIRS * n_seg) for src in (cast_src1, cast_src2)]
    assert riders[0].n_chunks == riders[1].n_chunks
    per = seg // halo
    n_halo_blocks = t // halo
    ncfg = bias_cfg.shape[0]
    m, nk = NA_GROUP_ROWS * GRID_W, NA_KEY_ROWS * GRID_W
    kb0, vb0 = base + COL_KB // LANES, base + COL_VB // LANES

    def cur(col0):
        return pl.BlockSpec((seg, LANES), lambda p, s, c, k: (s, col0 + p))

    def prev(col0):
        return pl.BlockSpec((halo, LANES), lambda p, s, c, k: (jnp.maximum(s * per - 1, 0), col0 + p))

    def nxt(col0):
        return pl.BlockSpec((halo, LANES), lambda p, s, c, k: (jnp.minimum((s + 1) * per, n_halo_blocks - 1), col0 + p))

    grid_spec = pltpu.PrefetchScalarGridSpec(
        num_scalar_prefetch=2,
        grid=(N_HEAD_PAIRS, n_seg),
        in_specs=[cur(base + COL_QB // LANES), prev(kb0), cur(kb0), nxt(kb0), prev(vb0), cur(vb0), nxt(vb0),
                  pl.BlockSpec((ncfg, 2, m, nk), lambda p, s, c, k: (0, p, 0, 0)),
                  riders[0].in_spec, riders[1].in_spec],
        out_specs=[pl.BlockSpec((seg, LANES), lambda p, s, c, k: (s, p)), riders[0].out_spec, riders[1].out_spec],
        scratch_shapes=[pltpu.VMEM((seg + 2 * halo, LANES), BF16), pltpu.VMEM((seg + 2 * halo, LANES), BF16),
                        pltpu.VMEM((2, 2 * m, nk), F32), pltpu.VMEM((2, 2 * m, nk), BF16),
                        pltpu.VMEM((2, 2 * m, 1), F32)],
    )
    return pl.pallas_call(
        functools.partial(_na_attn_kernel, seg=seg, n_cast_chunks=riders[0].n_chunks),
        grid_spec=grid_spec,
        out_shape=[jax.ShapeDtypeStruct((t, W_B), BF16), riders[0].out_shape, riders[1].out_shape],
        compiler_params=_cparams(("arbitrary", "arbitrary")),
        name="na_attn",
    )(cfg_ids, kstarts, proj, proj, proj, proj, proj, proj, proj, bias_cfg, riders[0].operand, riders[1].operand)


def _outproj_kernel(aa_ref, ab_ref, ga_ref, gb_ref, xp_ref, xs_ref, mod_ref, woa_ref, wob_ref, wout_ref,
                    g2_ref, wrh_ref, wrl_ref, br_ref, x1_ref, h2_ref, lg_ref, *, n_prompt_tiles):
    i = pl.program_id(0)
    ya = jnp.dot(aa_ref[...], woa_ref[...], preferred_element_type=F32)
    yb = jnp.dot(ab_ref[...], wob_ref[...], preferred_element_type=F32)
    merged = ga_ref[...].astype(F32) * ya + gb_ref[...].astype(F32) * yb
    z = jnp.dot(merged.astype(BF16), wout_ref[...], preferred_element_type=F32)
    x = jnp.where(i < n_prompt_tiles, xp_ref[...], xs_ref[...])
    m = mod_ref[0]
    x1 = x + m[2:3] * z
    x1_ref[...] = x1
    y = x1 * lax.rsqrt(jnp.mean(x1 * x1, axis=-1, keepdims=True) + RMS_EPS) * g2_ref[...]
    h2 = y * (1.0 + m[4:5]) + m[3:4]
    h2_hi = h2.astype(BF16)
    h2_ref[...] = h2_hi
    h2_lo = (h2 - h2_hi.astype(F32)).astype(BF16)
    lg = jnp.dot(h2_hi, wrh_ref[...], preferred_element_type=F32)
    lg = lg + jnp.dot(h2_lo, wrh_ref[...], preferred_element_type=F32)
    lg = lg + jnp.dot(h2_hi, wrl_ref[...], preferred_element_type=F32)
    lg_ref[...] = lg + br_ref[...]


def _outproj_call(attn_a, attn_b, proj, xp, xs, mod_seg, woa, wob, wout, g2, wr_hi, wr_lo, br, seg):
    t = attn_a.shape[0]
    d = xp.shape[1]
    tm = min(256, seg)
    npt, nst = xp.shape[0] // tm, xs.shape[0] // tm
    xp_spec, xs_spec = _two_group_specs(tm, d, npt, nst)
    def const(shape):
        return pl.BlockSpec(shape, lambda i: (0,) * len(shape), pipeline_mode=pl.Buffered(1))

    return pl.pallas_call(
        functools.partial(_outproj_kernel, n_prompt_tiles=npt),
        grid=(npt + nst,),
        in_specs=[pl.BlockSpec((tm, Q_A), lambda i: (i, 0)),
                  pl.BlockSpec((tm, W_B), lambda i: (i, 0)),
                  pl.BlockSpec((tm, d), lambda i: (i, 0)),
                  pl.BlockSpec((tm, d), lambda i: (i, 1)),
                  xp_spec, xs_spec,
                  pl.BlockSpec((1, 6, d), lambda i: (i * tm // seg, 0, 0)),
                  const((Q_A, d)), const((W_B, d)), const((d, d)), const((1, d)),
                  const((d, LANES)), const((d, LANES)), const((1, LANES))],
        out_specs=[pl.BlockSpec((tm, d), lambda i: (i, 0)),
                   pl.BlockSpec((tm, d), lambda i: (i, 0)),
                   pl.BlockSpec((tm, LANES), lambda i: (i, 0))],
        out_shape=[jax.ShapeDtypeStruct((t, d), F32),
                   jax.ShapeDtypeStruct((t, d), BF16),
                   jax.ShapeDtypeStruct((t, LANES), F32)],
        compiler_params=_cparams(("arbitrary",)),
        name="out_proj",
    )(attn_a, attn_b, proj, proj, xp, xs, mod_seg, woa, wob, wout, g2.reshape(1, d), wr_hi, wr_lo, br)


def _moe_kernel(te_ref, nv_ref, x_ref, wg_ref, wl_ref, bg_ref, bl_ref, wd_ref, bd_ref, *rest, nf, tile0):
    o_ref, acc_ref = rest[-2:]
    t = pl.program_id(0) + tile0
    f = pl.program_id(1)
    live = t < nv_ref[0]

    @pl.when((pl.program_id(0) == 0) & (f == 0))
    def _():
        acc_ref[...] = jnp.zeros_like(acc_ref)

    @pl.when(live)
    def _():
        x = x_ref[...]
        gate = jnp.dot(x, wg_ref[...], preferred_element_type=F32) + bg_ref[0]
        lin = jnp.dot(x, wl_ref[...], preferred_element_type=F32) + bl_ref[0]
        gate = jnp.minimum(gate, SWIGLU_LIMIT)
        lin = jnp.clip(lin, -SWIGLU_LIMIT, SWIGLU_LIMIT)
        act = gate * _sigmoid(SWIGLU_ALPHA * gate) * (lin + 1.0)
        part = jnp.dot(act.astype(BF16), wd_ref[...], preferred_element_type=F32)
        total = part + jnp.where(f == 0, bd_ref[0], acc_ref[...])
        acc_ref[...] = total
        o_ref[...] = total.astype(BF16)

    @pl.when(jnp.logical_not(live) & (f == 0))
    def _():
        o_ref[...] = jnp.zeros_like(o_ref)


def _moe_call(xin, tile_e, n_valid, w_gate, w_lin, b_gu, w_down, b_down, tm, tile0, n_rows_total, yb_prev):
    n_rows, d = xin.shape
    n_e, two_ff = b_gu.shape
    d_ff = two_ff // 2
    tf = min(512, d_ff)
    nf = d_ff // tf
    n_tiles = n_rows // tm

    def fidx(t, f, nv):
        return jnp.where(t + tile0 < nv[0], f, nf - 1)

    def expert(t, te):
        return te[t + tile0]

    in_specs = [pl.BlockSpec((tm, d), lambda t, f, te, nv: (t, 0)),
                pl.BlockSpec((d, tf), lambda t, f, te, nv: (expert(t, te), fidx(t, f, nv))),
                pl.BlockSpec((d, tf), lambda t, f, te, nv: (expert(t, te), fidx(t, f, nv))),
                pl.BlockSpec((1, 1, tf), lambda t, f, te, nv: (expert(t, te), 0, fidx(t, f, nv))),
                pl.BlockSpec((1, 1, tf), lambda t, f, te, nv: (expert(t, te), 0, fidx(t, f, nv) + nf)),
                pl.BlockSpec((tf, d), lambda t, f, te, nv: (expert(t, te) * nf + fidx(t, f, nv), 0)),
                pl.BlockSpec((1, 1, d), lambda t, f, te, nv: (expert(t, te), 0, 0))]
    operands = [tile_e, n_valid, xin, w_gate, w_lin, b_gu.reshape(n_e, 1, two_ff), b_gu.reshape(n_e, 1, two_ff),
                w_down, b_down.reshape(n_e, 1, d)]
    aliases = {}
    if yb_prev is not None:
        in_specs.append(pl.BlockSpec(memory_space=pl.ANY))
        aliases = {len(operands): 0}
        operands.append(yb_prev)
    grid_spec = pltpu.PrefetchScalarGridSpec(
        num_scalar_prefetch=2,
        grid=(n_tiles, nf),
        in_specs=in_specs,
        out_specs=pl.BlockSpec((tm, d), lambda t, f, te, nv: (t + tile0, 0)),
        scratch_shapes=[pltpu.VMEM((tm, d), F32)],
    )
    return pl.pallas_call(
        functools.partial(_moe_kernel, nf=nf, tile0=tile0),
        grid_spec=grid_spec,
        out_shape=jax.ShapeDtypeStruct((n_rows_total, d), BF16),
        input_output_aliases=aliases,
        compiler_params=_cparams(("arbitrary", "arbitrary")),
        name="moe_experts",
    )(*operands)


def _combine_kernel(x1_ref, ys_ref, w_ref, mod_ref, o_ref):
    w = w_ref[...]
    acc = w[:, 0:1] * ys_ref[0].astype(F32)
    for k in range(1, TOP_K):
        acc = acc + w[:, k:k + 1] * ys_ref[k].astype(F32)
    o_ref[...] = x1_ref[...] + mod_ref[0][5:6] * acc


def _combine_call(x1, ysel, top_w, mod_seg, row0, n_rows, seg):
    d = x1.shape[1]
    tm = min(512, seg)
    off = row0 // tm
    return pl.pallas_call(
        _combine_kernel,
        grid=(n_rows // tm,),
        in_specs=[pl.BlockSpec((tm, d), lambda i: (i + off, 0)),
                  pl.BlockSpec((TOP_K, tm, d), lambda i: (0, i + off, 0)),
                  pl.BlockSpec((tm, TOP_K), lambda i: (i + off, 0)),
                  pl.BlockSpec((1, 6, d), lambda i: ((i + off) * tm // seg, 0, 0))],
        out_specs=pl.BlockSpec((tm, d), lambda i: (i, 0)),
        out_shape=jax.ShapeDtypeStruct((n_rows, d), F32),
        compiler_params=_cparams(("arbitrary",)),
        name="moe_combine",
    )(x1, ysel, top_w, mod_seg)


def _pack_w_in(w_in, g_q_a, g_k_a, g_q_b, g_k_b):
    o1 = Q_A
    o2 = o1 + KV_A
    o3 = o2 + KV_A
    o4 = o3 + W_B
    o5 = o4 + W_B
    o6 = o5 + W_B
    d = w_in.shape[0]
    wqa, wka, wva = w_in[:, :o1], w_in[:, o1:o2], w_in[:, o2:o3]
    wqb, wkb, wvb, wg = w_in[:, o3:o4], w_in[:, o4:o5], w_in[:, o5:o6], w_in[:, o6:]

    def dup(w):
        w4 = w.reshape(d, N_KV_A, 1, HEAD_DIM)
        return jnp.broadcast_to(w4, (d, N_KV_A, 2, HEAD_DIM)).reshape(d, N_KV_A * LANES)

    w = jnp.concatenate([wg, wqa, wqb, wkb, dup(wka), dup(wva), wvb], axis=1).astype(BF16)
    q_scale = HEAD_DIM ** -0.5
    gvec = jnp.concatenate([jnp.ones((2 * d,), F32),
                            jnp.tile(g_q_a * q_scale, N_HEADS_A), jnp.tile(g_q_b * q_scale, N_HEADS_B),
                            jnp.tile(g_k_b, N_HEADS_B), jnp.tile(g_k_a, 2 * N_KV_A),
                            jnp.ones((COL_END - COL_VA,), F32)]).reshape(1, -1).astype(F32)
    return w, gvec


def kernel(x_prompt, x_sample, c_prompt, c_sample, w_ada, b_ada, g_norm1, w_in, g_q_a, g_k_a, g_q_b, g_k_b, sink_a, t5_bias, na_bias, w_o_a, w_o_b, w_out, g_norm2, w_router, b_router, w_gu, b_gu, w_down, b_down):
    bp, sp, d = x_prompt.shape
    bs, ss, _ = x_sample.shape
    n_e = w_router.shape[-1]
    tp, ts = bp * sp, bs * ss
    t = tp + ts
    seg = math.gcd(math.gcd(sp, ss), 2048)
    assert w_ada.shape[0] == 1 and sp % seg == 0 and ss % seg == 0 and seg % (NA_HALO_ROWS * GRID_W) == 0
    assert 2 * d == w_in.shape[-1] - (Q_A + 2 * KV_A + 3 * W_B)
    base = 2 * d // LANES
    xp = x_prompt.reshape(tp, d)
    xs = x_sample.reshape(ts, d)

    n_c = bp + bs
    c_pad = jnp.zeros((-(-n_c // 16) * 16, d), F32).at[:n_c].set(jnp.concatenate([c_prompt, c_sample], axis=0))
    mod = _ada_call(c_pad, w_ada[0], b_ada[0])[:n_c].reshape(n_c, 6, d)
    seg_owner = np.concatenate([np.repeat(np.arange(bp), sp // seg), bp + np.repeat(np.arange(bs), ss // seg)])
    mod_seg = mod[seg_owner]

    h1 = _norm_mod_call(xp, xs, mod_seg, g_norm1[0], seg)

    w_slab, gvec = _pack_w_in(w_in[0], g_q_a[0], g_k_a[0], g_q_b[0], g_k_b[0])
    grp = np.arange(MXU_DIM) // HEAD_DIM
    ones_bd = jnp.asarray(grp[:, None] == grp[None, :], BF16)
    proj = _inproj_call(h1, w_slab, gvec, ones_bd, tm=min(1024, seg))

    qs = min(2048, seg)
    sb_tok = np.arange(t // qs) * qs
    seq_start = np.where(sb_tok < tp, sb_tok // sp * sp, tp + (sb_tok - tp) // ss * ss)
    seq_len = np.where(sb_tok < tp, sp, ss)
    flags = np.stack([sb_tok != seq_start, sb_tok + qs != seq_start + seq_len], axis=1).astype(np.int32).reshape(-1)
    n_e, d_ff = w_down.shape[1], w_down.shape[2]
    assert d_ff == d and w_gu.shape[1:] == (n_e, d, 2 * d_ff)
    w_gu2d = w_gu[0].reshape(n_e * d, 2 * d_ff)
    w_down2d = w_down[0].reshape(n_e * d_ff, d)
    attn_a, w_gate = _win_attn_call(proj, _window_bias(t5_bias), sink_a[0].astype(F32), jnp.asarray(flags), qs, base,
                                    (w_gu2d, 0, d_ff))

    cfg_ids, kstarts, cfg_blocks = _na_plan([sp] * bp + [ss] * bs, seg)
    bias_cfg = _na_bias_tables(na_bias[0], cfg_blocks)
    attn_b, w_lin, w_dn = _na_attn_call(proj, bias_cfg, jnp.asarray(cfg_ids), jnp.asarray(kstarts), seg, base,
                                        (w_gu2d, 1, d_ff), (w_down2d, 0, d))

    wr = jnp.zeros((d, LANES), F32).at[:, :n_e].set(w_router[0])
    wr_hi = wr.astype(BF16)
    wr_lo = (wr - wr_hi.astype(F32)).astype(BF16)
    br = jnp.zeros((1, LANES), F32).at[0, :n_e].set(b_router[0])
    x1, h2, logits = _outproj_call(attn_a, attn_b, proj, xp, xs, mod_seg, w_o_a[0].astype(BF16),
                                   w_o_b[0].astype(BF16), w_out[0].astype(BF16), g_norm2[0], wr_hi, wr_lo, br, seg)

    tm_e = min(1024, seg)
    top_logits, top_idx = lax.top_k(logits[:, :n_e], TOP_K)
    top_w = jax.nn.softmax(top_logits, axis=-1)
    n_assign = t * TOP_K
    e_flat = top_idx.reshape(-1)
    onehot = (e_flat[:, None] == jnp.arange(n_e, dtype=e_flat.dtype)[None, :]).astype(jnp.int32)
    csum = jnp.cumsum(onehot, axis=0)
    rank = jnp.take_along_axis(csum, e_flat[:, None], axis=1)[:, 0] - 1
    counts = csum[-1]
    padded = (counts + tm_e - 1) // tm_e * tm_e
    pend = jnp.cumsum(padded)
    dest = (pend - padded)[e_flat] + rank
    n_tiles = -(-n_assign // tm_e) + n_e
    n_rows = n_tiles * tm_e
    tile_e = jnp.minimum(jnp.searchsorted(pend, jnp.arange(n_tiles) * tm_e, side='right'), n_e - 1).astype(jnp.int32)
    n_valid = (pend[-1:] // tm_e).astype(jnp.int32)
    tok_flat = jnp.arange(n_assign, dtype=jnp.int32) // TOP_K
    _, tok_sorted = lax.sort((e_flat, tok_flat), num_keys=1, is_stable=True)
    tile_src0 = (jnp.cumsum(counts) - counts)[tile_e] + jnp.arange(n_tiles, dtype=jnp.int32) * tm_e - (pend - padded)[tile_e]
    tile_end = jnp.cumsum(counts)[tile_e]
    src = tile_src0[:, None] + jnp.arange(tm_e, dtype=jnp.int32)[None, :]
    slot_tok = jnp.where(src < tile_end[:, None], tok_sorted[jnp.minimum(src, n_assign - 1)],
                         (jnp.arange(n_rows, dtype=jnp.int32) % t).reshape(n_tiles, tm_e))

    n_chunks = math.gcd(n_tiles, MOE_CHUNKS)
    tiles_per_chunk = n_tiles // n_chunks
    yb = None
    for c in range(n_chunks):
        tiles = slice(c * tiles_per_chunk, (c + 1) * tiles_per_chunk)
        xin = h2[slot_tok[tiles].reshape(-1)]
        yb = _moe_call(xin, tile_e, n_valid, w_gate, w_lin, b_gu[0], w_dn, b_down[0], tm_e,
                       c * tiles_per_chunk, n_rows, yb)

    ysel = yb[dest.reshape(t, TOP_K).T]
    y_prompt = _combine_call(x1, ysel, top_w, mod_seg, 0, tp, seg)
    y_sample = _combine_call(x1, ysel, top_w, mod_seg, tp, ts, seg)
    return (y_prompt.reshape(bp, sp, d), y_sample.reshape(bs, ss, d))
```

```python
import functools
import math

import numpy as np
import jax
import jax.numpy as jnp
from jax import lax
from jax.experimental import pallas as pl
from jax.experimental.pallas import tpu as pltpu

F32 = jnp.float32
BF16 = jnp.bfloat16

HEAD_DIM = 64
N_HEADS_A = 16
N_KV_A = 4
GROUP_A = N_HEADS_A // N_KV_A
N_HEADS_B = 16
WINDOW = 128
ATTN_BLOCK = 128
T5_BUCKETS = 32
T5_MAX_DIST = 128
GRID_W = 64
NA_ROWS = 8
NA_COLS = 16
TOP_K = 4
SWIGLU_LIMIT = 7.0
SWIGLU_ALPHA = 1.702
RMS_EPS = 1e-6

Q_A = N_HEADS_A * HEAD_DIM
KV_A = N_KV_A * HEAD_DIM
W_B = N_HEADS_B * HEAD_DIM

LANES = 128
MXU_DIM = 256
VMEM_LIMIT_BYTES = 56 * 1024 * 1024
MASKED = -1e30
LOG2_E = math.log2(math.e)

N_HEAD_PAIRS = Q_A // LANES
COL_QA = 0
COL_QB = COL_QA + Q_A
COL_KB = COL_QB + W_B
COL_KA = COL_KB + W_B
COL_VA = COL_KA + N_KV_A * LANES
COL_VB = COL_VA + N_KV_A * LANES
COL_END = COL_VB + W_B
NA_GROUP_ROWS = 2
NA_KEY_ROWS = NA_GROUP_ROWS + NA_ROWS - 1
NA_HALO_ROWS = NA_ROWS


def _sigmoid(x):
    return 1.0 / (1.0 + jnp.exp(-x))


def _cparams(sem):
    return pltpu.CompilerParams(dimension_semantics=sem, vmem_limit_bytes=VMEM_LIMIT_BYTES)


def _ada_kernel(c_ref, w_ref, b_ref, o_ref):
    c = c_ref[...]
    a = (c * _sigmoid(c)).astype(BF16)
    o_ref[...] = jnp.dot(a, w_ref[...].astype(BF16), preferred_element_type=F32) + b_ref[...]


def _ada_call(c_pad, w_ada, b_ada):
    rows, d = c_pad.shape
    n = w_ada.shape[1]
    tn = min(n, 1024)
    return pl.pallas_call(
        _ada_kernel,
        grid=(n // tn,),
        in_specs=[pl.BlockSpec((rows, d), lambda j: (0, 0)),
                  pl.BlockSpec((d, tn), lambda j: (0, j)),
                  pl.BlockSpec((1, tn), lambda j: (0, j))],
        out_specs=pl.BlockSpec((rows, tn), lambda j: (0, j)),
        out_shape=jax.ShapeDtypeStruct((rows, n), F32),
        compiler_params=_cparams(("arbitrary",)),
        name="ada_ln",
    )(c_pad, w_ada, b_ada.reshape(1, n))


def _two_group_specs(tm, d, n_prompt_tiles, n_sample_tiles):
    xp = pl.BlockSpec((tm, d), lambda i, *_: (jnp.minimum(i, n_prompt_tiles - 1), 0))
    xs = pl.BlockSpec((tm, d), lambda i, *_: (jnp.clip(i - n_prompt_tiles, 0, n_sample_tiles - 1), 0))
    return xp, xs


def _norm_mod_kernel(xp_ref, xs_ref, mod_ref, g_ref, o_ref, *, n_prompt_tiles):
    i = pl.program_id(0)
    x = jnp.where(i < n_prompt_tiles, xp_ref[...], xs_ref[...])
    y = x * lax.rsqrt(jnp.mean(x * x, axis=-1, keepdims=True) + RMS_EPS) * g_ref[...]
    m = mod_ref[0]
    o_ref[...] = (y * (1.0 + m[1:2]) + m[0:1]).astype(BF16)


def _norm_mod_call(xp, xs, mod_seg, g, seg):
    d = xp.shape[1]
    tm = min(512, seg)
    npt, nst = xp.shape[0] // tm, xs.shape[0] // tm
    xp_spec, xs_spec = _two_group_specs(tm, d, npt, nst)
    return pl.pallas_call(
        functools.partial(_norm_mod_kernel, n_prompt_tiles=npt),
        grid=(npt + nst,),
        in_specs=[xp_spec, xs_spec,
                  pl.BlockSpec((1, 6, d), lambda i: (i * tm // seg, 0, 0)),
                  pl.BlockSpec((1, d), lambda i: (0, 0))],
        out_specs=pl.BlockSpec((tm, d), lambda i: (i, 0)),
        out_shape=jax.ShapeDtypeStruct((xp.shape[0] + xs.shape[0], d), BF16),
        compiler_params=_cparams(("arbitrary",)),
        name="norm1_mod",
    )(xp, xs, mod_seg, g.reshape(1, d))


def _group_rms(y, g, ones_ref):
    sq = (y * y).astype(BF16)
    parts = []
    for c in range(y.shape[1] // MXU_DIM):
        parts.append(jnp.dot(sq[:, c * MXU_DIM:(c + 1) * MXU_DIM], ones_ref[...], preferred_element_type=F32))
    ss = jnp.concatenate(parts, axis=1) if len(parts) > 1 else parts[0]
    return y * lax.rsqrt(ss * (1.0 / HEAD_DIM) + RMS_EPS) * g


INPROJ_ROW_CHUNKS = 2


def _inproj_kernel(h_ref, w_ref, g_ref, ones_ref, o_ref, *, tn, gate_tiles):
    j = pl.program_id(1)
    half = tn // 2
    mixed_tile = gate_tiles + COL_KA // tn
    plain_tile = gate_tiles + COL_VB // tn
    rows = h_ref.shape[0] // INPROJ_ROW_CHUNKS

    def chunked(epilogue):
        for c in range(INPROJ_ROW_CHUNKS):
            r = slice(c * rows, (c + 1) * rows)
            epilogue(jnp.dot(h_ref[r, :], w_ref[...], preferred_element_type=F32), r)

    @pl.when(j < gate_tiles)
    def _():
        def epilogue(y, r):
            o_ref[r, :] = (0.5 * jnp.tanh(0.5 * y) + 0.5).astype(BF16)
        chunked(epilogue)

    @pl.when((j >= gate_tiles) & (j < mixed_tile))
    def _():
        def epilogue(y, r):
            o_ref[r, :] = _group_rms(y, g_ref[...], ones_ref).astype(BF16)
        chunked(epilogue)

    @pl.when(j == mixed_tile)
    def _():
        def epilogue(y, r):
            o_ref[r, :half] = _group_rms(y[:, :half], g_ref[:, :half], ones_ref).astype(BF16)
            o_ref[r, half:] = y[:, half:].astype(BF16)
        chunked(epilogue)

    @pl.when(j == plain_tile)
    def _():
        def epilogue(y, r):
            o_ref[r, :] = y.astype(BF16)
        chunked(epilogue)


def _inproj_call(h, w, gvec, ones_bd, tm):
    t, d = h.shape
    pw = w.shape[1]
    tn = 1024
    assert (2 * d) % tn == 0 and COL_KA % tn == 0 and COL_VA - COL_KA == tn // 2 and COL_VB % tn == 0
    return pl.pallas_call(
        functools.partial(_inproj_kernel, tn=tn, gate_tiles=2 * d // tn),
        grid=(t // tm, pw // tn),
        in_specs=[pl.BlockSpec((tm, d), lambda i, j: (i, 0)),
                  pl.BlockSpec((d, tn), lambda i, j: (0, j)),
                  pl.BlockSpec((1, tn), lambda i, j: (0, j)),
                  pl.BlockSpec((MXU_DIM, MXU_DIM), lambda i, j: (0, 0))],
        out_specs=pl.BlockSpec((tm, tn), lambda i, j: (i, j)),
        out_shape=jax.ShapeDtypeStruct((t, pw), BF16),
        compiler_params=_cparams(("arbitrary", "arbitrary")),
        name="in_proj",
    )(h, w, gvec, ones_bd)


def _aligned(x, m):
    return x if isinstance(x, int) else pl.multiple_of(x, m)


def _three_stage_pipeline(n, stage_a, stage_b, stage_c):
    assert n >= 6 and n % 2 == 0
    pairs = n // 2

    def trip(t, run_a=True, run_b=True, run_c=True):
        for k in range(2):
            if run_c:
                stage_c(2 * (t - 2) + k, k)
        for k in range(2):
            if run_b:
                stage_b(2 * (t - 1) + k, k)
        for k in range(2):
            if run_a:
                stage_a(2 * t + k, k)

    trip(0, run_b=False, run_c=False)
    trip(1, run_c=False)

    def body(t, carry):
        trip(t)
        return carry

    lax.fori_loop(2, pairs, body, 0)
    trip(pairs, run_a=False)
    trip(pairs + 1, run_a=False, run_b=False)


CAST_ROWS = 512
MOE_CHUNKS = 4


class _CastRider:
    def __init__(self, w2d, col_block, n_cols, n_inner, n_steps):
        rows = w2d.shape[0]
        tr = min(CAST_ROWS, rows)
        self.n_chunks = rows // tr
        assert rows % tr == 0 and self.n_chunks <= n_steps
        self.operand = w2d

        def chunk(p, s, *_):
            return jnp.minimum(p * n_inner + s, self.n_chunks - 1)

        self.in_spec = pl.BlockSpec((tr, n_cols), lambda *g: (chunk(*g), col_block))
        self.out_spec = pl.BlockSpec((tr, n_cols), lambda *g: (chunk(*g), 0))
        self.out_shape = jax.ShapeDtypeStruct((rows, n_cols), BF16)


def _cast_rider_step(step, n_chunks, pairs):
    @pl.when(step < n_chunks)
    def _():
        for src, dst in pairs:
            dst[...] = src[...].astype(BF16)


def _win_attn_kernel(flags_ref, q_ref, kp_ref, kc_ref, kn_ref, vp_ref, vc_ref, vn_ref, bias_ref, sink_ref,
                     wc_ref, o_ref, oc_ref, kcat, vcat, s_scr, e_scr, r_scr, *, qs, nb, n_cast_chunks):
    p = pl.program_id(0)
    sb = pl.program_id(1)
    _cast_rider_step(p * pl.num_programs(1) + sb, n_cast_chunks, [(wc_ref, oc_ref)])
    blk = ATTN_BLOCK
    kcat[0:blk] = kp_ref[...]
    kcat[blk:blk + qs] = kc_ref[...]
    kcat[blk + qs:] = kn_ref[...]
    vcat[0:blk] = vp_ref[...]
    vcat[blk:blk + qs] = vc_ref[...]
    vcat[blk + qs:] = vn_ref[...]
    prev_ok = flags_ref[2 * sb]
    next_ok = flags_ref[2 * sb + 1]
    low_half = lax.broadcasted_iota(jnp.int32, (blk, LANES), 1) < HEAD_DIM
    row = lax.broadcasted_iota(jnp.int32, (2 * blk, 1), 0)
    sinkcol = jnp.where(row < blk, sink_ref[2 * p], sink_ref[2 * p + 1])
    kl = blk + 2 * WINDOW

    def scores(b, slot):
        r0 = _aligned(b * blk, blk)
        q = q_ref[pl.ds(r0, blk), :]
        zero = jnp.zeros_like(q)
        q2 = jnp.concatenate([jnp.where(low_half, q, zero), jnp.where(low_half, zero, q)], axis=0)
        kw = kcat[pl.ds(r0, kl), :]
        s_scr[slot] = lax.dot_general(q2, kw, (((1,), (1,)), ((), ())), preferred_element_type=F32)

    def softmax(b, slot):
        variant = jnp.where((b == 0) & (prev_ok == 0), 1, jnp.where((b == nb - 1) & (next_ok == 0), 2, 0))
        s = s_scr[slot] + bias_ref[variant].reshape(2 * blk, kl)
        m = jnp.maximum(jnp.max(s, axis=-1, keepdims=True), sinkcol)
        e = jnp.exp2(s - m)
        denom = jnp.sum(e, axis=-1, keepdims=True) + jnp.exp2(sinkcol - m)
        e_scr[slot] = e.astype(BF16)
        r_scr[slot] = 1.0 / denom

    def values(b, slot):
        r0 = _aligned(b * blk, blk)
        vw = vcat[pl.ds(r0, kl), :]
        o2 = jnp.dot(e_scr[slot], vw, preferred_element_type=F32) * r_scr[slot]
        o_ref[pl.ds(r0, blk), :] = jnp.where(low_half, o2[:blk], o2[blk:]).astype(BF16)

    _three_stage_pipeline(nb, scores, softmax, values)


def _win_attn_call(proj, bias3, sink, flags, qs, base, cast_src):
    t = proj.shape[0]
    blk = ATTN_BLOCK
    nb = qs // blk
    n_super = t // qs
    rider = _CastRider(*cast_src, n_inner=n_super, n_steps=N_HEAD_PAIRS * n_super)
    n_blk_rows = t // blk
    kl = blk + 2 * WINDOW
    ka0, va0 = base + COL_KA // LANES, base + COL_VA // LANES

    def cur(col0):
        return pl.BlockSpec((qs, LANES), lambda p, s, f: (s, col0 + p // 2))

    def prev(col0):
        return pl.BlockSpec((blk, LANES), lambda p, s, f: (jnp.maximum(s * nb - 1, 0), col0 + p // 2))

    def nxt(col0):
        return pl.BlockSpec((blk, LANES), lambda p, s, f: (jnp.minimum((s + 1) * nb, n_blk_rows - 1), col0 + p // 2))

    grid_spec = pltpu.PrefetchScalarGridSpec(
        num_scalar_prefetch=1,
        grid=(N_HEAD_PAIRS, n_super),
        in_specs=[pl.BlockSpec((qs, LANES), lambda p, s, f: (s, base + COL_QA // LANES + p)),
                  prev(ka0), cur(ka0), nxt(ka0), prev(va0), cur(va0), nxt(va0),
                  pl.BlockSpec((3, 2, blk, kl), lambda p, s, f: (0, p, 0, 0)),
                  pl.BlockSpec(memory_space=pltpu.SMEM),
                  rider.in_spec],
        out_specs=[pl.BlockSpec((qs, LANES), lambda p, s, f: (s, p)), rider.out_spec],
        scratch_shapes=[pltpu.VMEM((qs + 2 * blk, LANES), BF16), pltpu.VMEM((qs + 2 * blk, LANES), BF16),
                        pltpu.VMEM((2, 2 * blk, kl), F32), pltpu.VMEM((2, 2 * blk, kl), BF16),
                        pltpu.VMEM((2, 2 * blk, 1), F32)],
    )
    return pl.pallas_call(
        functools.partial(_win_attn_kernel, qs=qs, nb=nb, n_cast_chunks=rider.n_chunks),
        grid_spec=grid_spec,
        out_shape=[jax.ShapeDtypeStruct((t, Q_A), BF16), rider.out_shape],
        compiler_params=_cparams(("arbitrary", "arbitrary")),
        name="win_attn",
    )(flags, proj, proj, proj, proj, proj, proj, proj, bias3, sink, rider.operand)


def _t5_bucket(rel):
    nb = T5_BUCKETS // 2
    max_exact = nb // 2
    ret = jnp.where(rel > 0, nb, 0)
    n = jnp.abs(rel)
    nf = jnp.maximum(n, 1).astype(jnp.float32)
    large = max_exact + (jnp.log(nf / max_exact) / math.log(T5_MAX_DIST / max_exact) * (nb - max_exact)).astype(jnp.int32)
    large = jnp.minimum(large, nb - 1)
    return ret + jnp.where(n < max_exact, n, large)


def _toeplitz(v, n_rows, n_cols):
    p = v.shape[-1]
    assert n_cols <= p - 1
    flat = jnp.tile(v, (1,) * (v.ndim - 1) + (n_rows,))[..., :n_rows * (p - 1)]
    return flat.reshape(v.shape[:-1] + (n_rows, p - 1))[..., :n_cols]


def _window_bias(t5_bias):
    blk, kl = ATTN_BLOCK, ATTN_BLOCK + 2 * WINDOW
    period = blk + kl - 1
    m = np.arange(period)
    rel_of_m = np.where(m < kl, m, m - period) - WINDOW
    per_rel = t5_bias[_t5_bucket(jnp.asarray(rel_of_m))].astype(F32).T
    bias = _toeplitz(per_rel * LOG2_E, blk, kl)
    col = np.arange(kl)[None, :]
    band = np.abs(col - WINDOW - np.arange(blk)[:, None]) <= WINDOW
    keep = np.stack([band, band & (col >= WINDOW), band & (col < WINDOW + blk)])
    return jnp.where(keep[:, None], bias[None], MASKED)


def _na_plan(seq_lens, seg):
    qr, nkr = NA_GROUP_ROWS, NA_KEY_ROWS
    configs, cfg_ids, kstarts = {}, [], []
    tok = 0
    for s_len in seq_lens:
        rows = s_len // GRID_W
        assert rows >= nkr and rows % qr == 0
        for r in range(0, rows, qr):
            us = min(max(r - NA_ROWS // 2, 0), rows - nkr)
            rel = tuple(min(max(r + q - NA_ROWS // 2, 0), rows - NA_ROWS) - us for q in range(qr))
            key = (r - us, rel)
            cfg_ids.append(configs.setdefault(key, len(configs)))
            g_tok = tok + r * GRID_W
            seg_tok0 = (g_tok // seg) * seg
            kstarts.append(tok + us * GRID_W - seg_tok0 + NA_HALO_ROWS * GRID_W)
        tok += s_len
    blocks = [None] * len(configs)
    for (r_us, rel), c in configs.items():
        blocks[c] = [[kr - r_us - q + NA_ROWS - 1 if rel[q] <= kr < rel[q] + NA_ROWS else None
                      for kr in range(nkr)] for q in range(qr)]
    return np.asarray(cfg_ids, np.int32), np.asarray(kstarts, np.int32), blocks


def _na_bias_tables(na_bias, blocks):
    period = 2 * GRID_W - 1
    m = np.arange(period)
    dcol = np.where(m < GRID_W, m, m - period)
    per_dcol = na_bias.astype(F32)[:, :, np.clip(dcol + NA_COLS - 1, 0, 2 * NA_COLS - 2)]
    col_tab = _toeplitz(per_dcol * LOG2_E, GRID_W, GRID_W)
    q_col, k_col = np.arange(GRID_W)[:, None], np.arange(GRID_W)[None, :]
    win_start = np.clip(q_col - NA_COLS // 2, 0, GRID_W - NA_COLS)
    col_ok = (k_col >= win_start) & (k_col < win_start + NA_COLS)
    col_tab = jnp.where(col_ok, col_tab, MASKED)
    masked_blk = jnp.full(col_tab.shape[:1] + col_tab.shape[2:], MASKED, F32)
    cfgs = []
    for cfg in blocks:
        rows = [jnp.concatenate([masked_blk if dr is None else col_tab[:, dr] for dr in qrow], axis=-1) for qrow in cfg]
        cfgs.append(jnp.concatenate(rows, axis=-2))
    return jnp.stack(cfgs)


def _na_attn_kernel(cfg_ref, ks_ref, q_ref, kp_ref, kc_ref, kn_ref, vp_ref, vc_ref, vn_ref, bias_ref,
                    wc1_ref, wc2_ref, o_ref, oc1_ref, oc2_ref, kcat, vcat, s_scr, e_scr, r_scr,
                    *, seg, n_cast_chunks):
    s_id = pl.program_id(1)
    _cast_rider_step(pl.program_id(0) * pl.num_programs(1) + s_id, n_cast_chunks,
                     [(wc1_ref, oc1_ref), (wc2_ref, oc2_ref)])
    halo = NA_HALO_ROWS * GRID_W
    m, nk = NA_GROUP_ROWS * GRID_W, NA_KEY_ROWS * GRID_W
    n_groups = seg // m
    kcat[0:halo] = kp_ref[...]
    kcat[halo:halo + seg] = kc_ref[...]
    kcat[halo + seg:] = kn_ref[...]
    vcat[0:halo] = vp_ref[...]
    vcat[halo:halo + seg] = vc_ref[...]
    vcat[halo + seg:] = vn_ref[...]
    low_half = lax.broadcasted_iota(jnp.int32, (m, LANES), 1) < HEAD_DIM

    def scores(g, slot):
        ks = pl.multiple_of(ks_ref[s_id * n_groups + g], GRID_W)
        r0 = _aligned(g * m, m)
        q = q_ref[pl.ds(r0, m), :]
        zero = jnp.zeros_like(q)
        q2 = jnp.concatenate([jnp.where(low_half, q, zero), jnp.where(low_half, zero, q)], axis=0)
        kw = kcat[pl.ds(ks, nk), :]
        s_scr[slot] = lax.dot_general(q2, kw, (((1,), (1,)), ((), ())), preferred_element_type=F32)

    def softmax(g, slot):
        cfg = cfg_ref[s_id * n_groups + g]
        s = s_scr[slot] + bias_ref[cfg].reshape(2 * m, nk)
        mx = jnp.max(s, axis=-1, keepdims=True)
        e = jnp.exp2(s - mx)
        e_scr[slot] = e.astype(BF16)
        r_scr[slot] = 1.0 / jnp.sum(e, axis=-1, keepdims=True)

    def values(g, slot):
        ks = pl.multiple_of(ks_ref[s_id * n_groups + g], GRID_W)
        r0 = _aligned(g * m, m)
        vw = vcat[pl.ds(ks, nk), :]
        o2 = jnp.dot(e_scr[slot], vw, preferred_element_type=F32) * r_scr[slot]
        o_ref[pl.ds(r0, m), :] = jnp.where(low_half, o2[:m], o2[m:]).astype(BF16)

    _three_stage_pipeline(n_groups, scores, softmax, values)


def _na_attn_call(proj, bias_cfg, cfg_ids, kstarts, seg, base, cast_src1, cast_src2):
    t = proj.shape[0]
    halo = NA_HALO_ROWS * GRID_W
    n_seg = t // seg
    riders = [_CastRider(*src, n_inner=n_seg, n_steps=N_HEAD_PAIRS * n_seg) for src in (cast_src1, cast_src2)]
    assert riders[0].n_chunks == riders[1].n_chunks
    per = seg // halo
    n_halo_blocks = t // halo
    ncfg = bias_cfg.shape[0]
    m, nk = NA_GROUP_ROWS * GRID_W, NA_KEY_ROWS * GRID_W
    kb0, vb0 = base + COL_KB // LANES, base + COL_VB // LANES

    def cur(col0):
        return pl.BlockSpec((seg, LANES), lambda p, s, c, k: (s, col0 + p))

    def prev(col0):
        return pl.BlockSpec((halo, LANES), lambda p, s, c, k: (jnp.maximum(s * per - 1, 0), col0 + p))

    def nxt(col0):
        return pl.BlockSpec((halo, LANES), lambda p, s, c, k: (jnp.minimum((s + 1) * per, n_halo_blocks - 1), col0 + p))

    grid_spec = pltpu.PrefetchScalarGridSpec(
        num_scalar_prefetch=2,
        grid=(N_HEAD_PAIRS, n_seg),
        in_specs=[cur(base + COL_QB // LANES), prev(kb0), cur(kb0), nxt(kb0), prev(vb0), cur(vb0), nxt(vb0),
                  pl.BlockSpec((ncfg, 2, m, nk), lambda p, s, c, k: (0, p, 0, 0)),
                  riders[0].in_spec, riders[1].in_spec],
        out_specs=[pl.BlockSpec((seg, LANES), lambda p, s, c, k: (s, p)), riders[0].out_spec, riders[1].out_spec],
        scratch_shapes=[pltpu.VMEM((seg + 2 * halo, LANES), BF16), pltpu.VMEM((seg + 2 * halo, LANES), BF16),
                        pltpu.VMEM((2, 2 * m, nk), F32), pltpu.VMEM((2, 2 * m, nk), BF16),
                        pltpu.VMEM((2, 2 * m, 1), F32)],
    )
    return pl.pallas_call(
        functools.partial(_na_attn_kernel, seg=seg, n_cast_chunks=riders[0].n_chunks),
        grid_spec=grid_spec,
        out_shape=[jax.ShapeDtypeStruct((t, W_B), BF16), riders[0].out_shape, riders[1].out_shape],
        compiler_params=_cparams(("arbitrary", "arbitrary")),
        name="na_attn",
    )(cfg_ids, kstarts, proj, proj, proj, proj, proj, proj, proj, bias_cfg, riders[0].operand, riders[1].operand)


def _outproj_kernel(aa_ref, ab_ref, ga_ref, gb_ref, xp_ref, xs_ref, mod_ref, woa_ref, wob_ref, wout_ref,
                    g2_ref, wrh_ref, wrl_ref, br_ref, x1_ref, h2_ref, lg_ref, *, n_prompt_tiles):
    i = pl.program_id(0)
    ya = jnp.dot(aa_ref[...], woa_ref[...], preferred_element_type=F32)
    yb = jnp.dot(ab_ref[...], wob_ref[...], preferred_element_type=F32)
    merged = ga_ref[...].astype(F32) * ya + gb_ref[...].astype(F32) * yb
    z = jnp.dot(merged.astype(BF16), wout_ref[...], preferred_element_type=F32)
    x = jnp.where(i < n_prompt_tiles, xp_ref[...], xs_ref[...])
    m = mod_ref[0]
    x1 = x + m[2:3] * z
    x1_ref[...] = x1
    y = x1 * lax.rsqrt(jnp.mean(x1 * x1, axis=-1, keepdims=True) + RMS_EPS) * g2_ref[...]
    h2 = y * (1.0 + m[4:5]) + m[3:4]
    h2_hi = h2.astype(BF16)
    h2_ref[...] = h2_hi
    h2_lo = (h2 - h2_hi.astype(F32)).astype(BF16)
    lg = jnp.dot(h2_hi, wrh_ref[...], preferred_element_type=F32)
    lg = lg + jnp.dot(h2_lo, wrh_ref[...], preferred_element_type=F32)
    lg = lg + jnp.dot(h2_hi, wrl_ref[...], preferred_element_type=F32)
    lg_ref[...] = lg + br_ref[...]


def _outproj_call(attn_a, attn_b, proj, xp, xs, mod_seg, woa, wob, wout, g2, wr_hi, wr_lo, br, seg):
    t = attn_a.shape[0]
    d = xp.shape[1]
    tm = min(256, seg)
    npt, nst = xp.shape[0] // tm, xs.shape[0] // tm
    xp_spec, xs_spec = _two_group_specs(tm, d, npt, nst)
    def const(shape):
        return pl.BlockSpec(shape, lambda i: (0,) * len(shape), pipeline_mode=pl.Buffered(1))

    return pl.pallas_call(
        functools.partial(_outproj_kernel, n_prompt_tiles=npt),
        grid=(npt + nst,),
        in_specs=[pl.BlockSpec((tm, Q_A), lambda i: (i, 0)),
                  pl.BlockSpec((tm, W_B), lambda i: (i, 0)),
                  pl.BlockSpec((tm, d), lambda i: (i, 0)),
                  pl.BlockSpec((tm, d), lambda i: (i, 1)),
                  xp_spec, xs_spec,
                  pl.BlockSpec((1, 6, d), lambda i: (i * tm // seg, 0, 0)),
                  const((Q_A, d)), const((W_B, d)), const((d, d)), const((1, d)),
                  const((d, LANES)), const((d, LANES)), const((1, LANES))],
        out_specs=[pl.BlockSpec((tm, d), lambda i: (i, 0)),
                   pl.BlockSpec((tm, d), lambda i: (i, 0)),
                   pl.BlockSpec((tm, LANES), lambda i: (i, 0))],
        out_shape=[jax.ShapeDtypeStruct((t, d), F32),
                   jax.ShapeDtypeStruct((t, d), BF16),
                   jax.ShapeDtypeStruct((t, LANES), F32)],
        compiler_params=_cparams(("arbitrary",)),
        name="out_proj",
    )(attn_a, attn_b, proj, proj, xp, xs, mod_seg, woa, wob, wout, g2.reshape(1, d), wr_hi, wr_lo, br)


def _moe_kernel(te_ref, nv_ref, x_ref, wg_ref, wl_ref, bg_ref, bl_ref, wd_ref, bd_ref, *rest, nf, tile0):
    o_ref, acc_ref = rest[-2:]
    t = pl.program_id(0) + tile0
    f = pl.program_id(1)
    live = t < nv_ref[0]

    @pl.when((pl.program_id(0) == 0) & (f == 0))
    def _():
        acc_ref[...] = jnp.zeros_like(acc_ref)

    @pl.when(live)
    def _():
        x = x_ref[...]
        gate = jnp.dot(x, wg_ref[...], preferred_element_type=F32) + bg_ref[0]
        lin = jnp.dot(x, wl_ref[...], preferred_element_type=F32) + bl_ref[0]
        gate = jnp.minimum(gate, SWIGLU_LIMIT)
        lin = jnp.clip(lin, -SWIGLU_LIMIT, SWIGLU_LIMIT)
        act = gate * _sigmoid(SWIGLU_ALPHA * gate) * (lin + 1.0)
        part = jnp.dot(act.astype(BF16), wd_ref[...], preferred_element_type=F32)
        total = part + jnp.where(f == 0, bd_ref[0], acc_ref[...])
        acc_ref[...] = total
        o_ref[...] = total.astype(BF16)

    @pl.when(jnp.logical_not(live) & (f == 0))
    def _():
        o_ref[...] = jnp.zeros_like(o_ref)


def _moe_call(xin, tile_e, n_valid, w_gate, w_lin, b_gu, w_down, b_down, tm, tile0, n_rows_total, yb_prev):
    n_rows, d = xin.shape
    n_e, two_ff = b_gu.shape
    d_ff = two_ff // 2
    tf = min(512, d_ff)
    nf = d_ff // tf
    n_tiles = n_rows // tm

    def fidx(t, f, nv):
        return jnp.where(t + tile0 < nv[0], f, nf - 1)

    def expert(t, te):
        return te[t + tile0]

    in_specs = [pl.BlockSpec((tm, d), lambda t, f, te, nv: (t, 0)),
                pl.BlockSpec((d, tf), lambda t, f, te, nv: (expert(t, te), fidx(t, f, nv))),
                pl.BlockSpec((d, tf), lambda t, f, te, nv: (expert(t, te), fidx(t, f, nv))),
                pl.BlockSpec((1, 1, tf), lambda t, f, te, nv: (expert(t, te), 0, fidx(t, f, nv))),
                pl.BlockSpec((1, 1, tf), lambda t, f, te, nv: (expert(t, te), 0, fidx(t, f, nv) + nf)),
                pl.BlockSpec((tf, d), lambda t, f, te, nv: (expert(t, te) * nf + fidx(t, f, nv), 0)),
                pl.BlockSpec((1, 1, d), lambda t, f, te, nv: (expert(t, te), 0, 0))]
    operands = [tile_e, n_valid, xin, w_gate, w_lin, b_gu.reshape(n_e, 1, two_ff), b_gu.reshape(n_e, 1, two_ff),
                w_down, b_down.reshape(n_e, 1, d)]
    aliases = {}
    if yb_prev is not None:
        in_specs.append(pl.BlockSpec(memory_space=pl.ANY))
        aliases = {len(operands): 0}
        operands.append(yb_prev)
    grid_spec = pltpu.PrefetchScalarGridSpec(
        num_scalar_prefetch=2,
        grid=(n_tiles, nf),
        in_specs=in_specs,
        out_specs=pl.BlockSpec((tm, d), lambda t, f, te, nv: (t + tile0, 0)),
        scratch_shapes=[pltpu.VMEM((tm, d), F32)],
    )
    return pl.pallas_call(
        functools.partial(_moe_kernel, nf=nf, tile0=tile0),
        grid_spec=grid_spec,
        out_shape=jax.ShapeDtypeStruct((n_rows_total, d), BF16),
        input_output_aliases=aliases,
        compiler_params=_cparams(("arbitrary", "arbitrary")),
        name="moe_experts",
    )(*operands)


def _combine_kernel(x1_ref, ys_ref, w_ref, mod_ref, *rest):
    o_ref = rest[-1]
    w = w_ref[...]
    acc = w[:, 0:1] * ys_ref[0].astype(F32)
    for k in range(1, TOP_K):
        acc = acc + w[:, k:k + 1] * ys_ref[k].astype(F32)
    o_ref[...] = x1_ref[...] + mod_ref[0][5:6] * acc


def _combine_call(x1, ysel, top_w, mod_seg, row0, out_row0, n_out_rows, seg, prev):
    d = x1.shape[1]
    n_rows = ysel.shape[1]
    tm = min(512, seg)
    off, out_off = row0 // tm, out_row0 // tm
    in_specs = [pl.BlockSpec((tm, d), lambda i: (i + off, 0)),
                pl.BlockSpec((TOP_K, tm, d), lambda i: (0, i, 0)),
                pl.BlockSpec((tm, TOP_K), lambda i: (i + off, 0)),
                pl.BlockSpec((1, 6, d), lambda i: ((i + off) * tm // seg, 0, 0))]
    operands = [x1, ysel, top_w, mod_seg]
    aliases = {}
    if prev is not None:
        in_specs.append(pl.BlockSpec(memory_space=pl.ANY))
        aliases = {len(operands): 0}
        operands.append(prev)
    return pl.pallas_call(
        _combine_kernel,
        grid=(n_rows // tm,),
        in_specs=in_specs,
        out_specs=pl.BlockSpec((tm, d), lambda i: (i + out_off, 0)),
        out_shape=jax.ShapeDtypeStruct((n_out_rows, d), F32),
        input_output_aliases=aliases,
        compiler_params=_cparams(("arbitrary",)),
        name="moe_combine",
    )(*operands)


def _pack_w_in(w_in, g_q_a, g_k_a, g_q_b, g_k_b):
    o1 = Q_A
    o2 = o1 + KV_A
    o3 = o2 + KV_A
    o4 = o3 + W_B
    o5 = o4 + W_B
    o6 = o5 + W_B
    d = w_in.shape[0]
    wqa, wka, wva = w_in[:, :o1], w_in[:, o1:o2], w_in[:, o2:o3]
    wqb, wkb, wvb, wg = w_in[:, o3:o4], w_in[:, o4:o5], w_in[:, o5:o6], w_in[:, o6:]

    def dup(w):
        w4 = w.reshape(d, N_KV_A, 1, HEAD_DIM)
        return jnp.broadcast_to(w4, (d, N_KV_A, 2, HEAD_DIM)).reshape(d, N_KV_A * LANES)

    w = jnp.concatenate([wg, wqa, wqb, wkb, dup(wka), dup(wva), wvb], axis=1).astype(BF16)
    q_scale = HEAD_DIM ** -0.5 * LOG2_E
    gvec = jnp.concatenate([jnp.ones((2 * d,), F32),
                            jnp.tile(g_q_a * q_scale, N_HEADS_A), jnp.tile(g_q_b * q_scale, N_HEADS_B),
                            jnp.tile(g_k_b, N_HEADS_B), jnp.tile(g_k_a, 2 * N_KV_A),
                            jnp.ones((COL_END - COL_VA,), F32)]).reshape(1, -1).astype(F32)
    return w, gvec


def kernel(x_prompt, x_sample, c_prompt, c_sample, w_ada, b_ada, g_norm1, w_in, g_q_a, g_k_a, g_q_b, g_k_b, sink_a, t5_bias, na_bias, w_o_a, w_o_b, w_out, g_norm2, w_router, b_router, w_gu, b_gu, w_down, b_down):
    bp, sp, d = x_prompt.shape
    bs, ss, _ = x_sample.shape
    n_e = w_router.shape[-1]
    tp, ts = bp * sp, bs * ss
    t = tp + ts
    seg = math.gcd(math.gcd(sp, ss), 2048)
    assert w_ada.shape[0] == 1 and sp % seg == 0 and ss % seg == 0 and seg % (NA_HALO_ROWS * GRID_W) == 0
    assert 2 * d == w_in.shape[-1] - (Q_A + 2 * KV_A + 3 * W_B)
    base = 2 * d // LANES
    xp = x_prompt.reshape(tp, d)
    xs = x_sample.reshape(ts, d)

    n_c = bp + bs
    c_pad = jnp.zeros((-(-n_c // 16) * 16, d), F32).at[:n_c].set(jnp.concatenate([c_prompt, c_sample], axis=0))
    mod = _ada_call(c_pad, w_ada[0], b_ada[0])[:n_c].reshape(n_c, 6, d)
    seg_owner = np.concatenate([np.repeat(np.arange(bp), sp // seg), bp + np.repeat(np.arange(bs), ss // seg)])
    mod_seg = mod[seg_owner]

    h1 = _norm_mod_call(xp, xs, mod_seg, g_norm1[0], seg)

    w_slab, gvec = _pack_w_in(w_in[0], g_q_a[0], g_k_a[0], g_q_b[0], g_k_b[0])
    grp = np.arange(MXU_DIM) // HEAD_DIM
    ones_bd = jnp.asarray(grp[:, None] == grp[None, :], BF16)
    proj = _inproj_call(h1, w_slab, gvec, ones_bd, tm=min(1024, seg))

    qs = min(2048, seg)
    sb_tok = np.arange(t // qs) * qs
    seq_start = np.where(sb_tok < tp, sb_tok // sp * sp, tp + (sb_tok - tp) // ss * ss)
    seq_len = np.where(sb_tok < tp, sp, ss)
    flags = np.stack([sb_tok != seq_start, sb_tok + qs != seq_start + seq_len], axis=1).astype(np.int32).reshape(-1)
    n_e, d_ff = w_down.shape[1], w_down.shape[2]
    assert d_ff == d and w_gu.shape[1:] == (n_e, d, 2 * d_ff)
    w_gu2d = w_gu[0].reshape(n_e * d, 2 * d_ff)
    w_down2d = w_down[0].reshape(n_e * d_ff, d)
    attn_a, w_gate = _win_attn_call(proj, _window_bias(t5_bias), sink_a[0].astype(F32) * LOG2_E, jnp.asarray(flags), qs, base,
                                    (w_gu2d, 0, d_ff))

    cfg_ids, kstarts, cfg_blocks = _na_plan([sp] * bp + [ss] * bs, seg)
    bias_cfg = _na_bias_tables(na_bias[0], cfg_blocks)
    attn_b, w_lin, w_dn = _na_attn_call(proj, bias_cfg, jnp.asarray(cfg_ids), jnp.asarray(kstarts), seg, base,
                                        (w_gu2d, 1, d_ff), (w_down2d, 0, d))

    wr = jnp.zeros((d, LANES), F32).at[:, :n_e].set(w_router[0])
    wr_hi = wr.astype(BF16)
    wr_lo = (wr - wr_hi.astype(F32)).astype(BF16)
    br = jnp.zeros((1, LANES), F32).at[0, :n_e].set(b_router[0])
    x1, h2, logits = _outproj_call(attn_a, attn_b, proj, xp, xs, mod_seg, w_o_a[0].astype(BF16),
                                   w_o_b[0].astype(BF16), w_out[0].astype(BF16), g_norm2[0], wr_hi, wr_lo, br, seg)

    tm_e = min(1024, seg)
    top_logits, top_idx = lax.top_k(logits[:, :n_e], TOP_K)
    top_w = jax.nn.softmax(top_logits, axis=-1)
    n_assign = t * TOP_K
    e_flat = top_idx.reshape(-1)
    onehot = (e_flat[:, None] == jnp.arange(n_e, dtype=e_flat.dtype)[None, :]).astype(jnp.int32)
    csum = jnp.cumsum(onehot, axis=0)
    rank = jnp.take_along_axis(csum, e_flat[:, None], axis=1)[:, 0] - 1
    counts = csum[-1]
    padded = (counts + tm_e - 1) // tm_e * tm_e
    pend = jnp.cumsum(padded)
    dest = (pend - padded)[e_flat] + rank
    n_tiles = -(-n_assign // tm_e) + n_e
    n_rows = n_tiles * tm_e
    tile_e = jnp.minimum(jnp.searchsorted(pend, jnp.arange(n_tiles) * tm_e, side='right'), n_e - 1).astype(jnp.int32)
    n_valid = (pend[-1:] // tm_e).astype(jnp.int32)
    tok_flat = jnp.arange(n_assign, dtype=jnp.int32) // TOP_K
    _, tok_sorted = lax.sort((e_flat, tok_flat), num_keys=1, is_stable=True)
    tile_src0 = (jnp.cumsum(counts) - counts)[tile_e] + jnp.arange(n_tiles, dtype=jnp.int32) * tm_e - (pend - padded)[tile_e]
    tile_end = jnp.cumsum(counts)[tile_e]
    src = tile_src0[:, None] + jnp.arange(tm_e, dtype=jnp.int32)[None, :]
    slot_tok = jnp.where(src < tile_end[:, None], tok_sorted[jnp.minimum(src, n_assign - 1)],
                         (jnp.arange(n_rows, dtype=jnp.int32) % t).reshape(n_tiles, tm_e))

    n_chunks = math.gcd(n_tiles, MOE_CHUNKS)
    tiles_per_chunk = n_tiles // n_chunks
    yb = None
    for c in range(n_chunks):
        tiles = slice(c * tiles_per_chunk, (c + 1) * tiles_per_chunk)
        xin = h2[slot_tok[tiles].reshape(-1)]
        yb = _moe_call(xin, tile_e, n_valid, w_gate, w_lin, b_gu[0], w_dn, b_down[0], tm_e,
                       c * tiles_per_chunk, n_rows, yb)

    dest_kt = dest.reshape(t, TOP_K).T
    chunk = math.gcd(tp, ts)
    if t // chunk > 8:
        chunk = 0
    outs = []
    for row0, n_group in ((0, tp), (tp, ts)):
        step = chunk if chunk else n_group
        y = None
        for r in range(0, n_group, step):
            ysel = yb[dest_kt[:, row0 + r:row0 + r + step]]
            y = _combine_call(x1, ysel, top_w, mod_seg, row0 + r, r, n_group, seg, y)
        outs.append(y)
    return (outs[0].reshape(bp, sp, d), outs[1].reshape(bs, ss, d))
```

```python
import functools
import math

import numpy as np
import jax
import jax.numpy as jnp
from jax import lax
from jax.experimental import pallas as pl
from jax.experimental.pallas import tpu as pltpu

F32 = jnp.float32
BF16 = jnp.bfloat16

HEAD_DIM = 64
N_HEADS_A = 16
N_KV_A = 4
GROUP_A = N_HEADS_A // N_KV_A
N_HEADS_B = 16
WINDOW = 128
ATTN_BLOCK = 128
T5_BUCKETS = 32
T5_MAX_DIST = 128
GRID_W = 64
NA_ROWS = 8
NA_COLS = 16
TOP_K = 4
SWIGLU_LIMIT = 7.0
SWIGLU_ALPHA = 1.702
RMS_EPS = 1e-6

Q_A = N_HEADS_A * HEAD_DIM
KV_A = N_KV_A * HEAD_DIM
W_B = N_HEADS_B * HEAD_DIM

LANES = 128
MXU_DIM = 256
VMEM_LIMIT_BYTES = 56 * 1024 * 1024
MASKED = -1e30
LOG2_E = math.log2(math.e)

N_HEAD_PAIRS = Q_A // LANES
COL_QA = 0
COL_QB = COL_QA + Q_A
COL_KB = COL_QB + W_B
COL_KA = COL_KB + W_B
COL_VA = COL_KA + N_KV_A * LANES
COL_VB = COL_VA + N_KV_A * LANES
COL_END = COL_VB + W_B
NA_GROUP_ROWS = 2
NA_KEY_ROWS = NA_GROUP_ROWS + NA_ROWS - 1
NA_HALO_ROWS = NA_ROWS


def _sigmoid(x):
    return 1.0 / (1.0 + jnp.exp(-x))


def _cparams(sem):
    return pltpu.CompilerParams(dimension_semantics=sem, vmem_limit_bytes=VMEM_LIMIT_BYTES)


def _ada_kernel(c_ref, w_ref, b_ref, o_ref):
    c = c_ref[...]
    a = (c * _sigmoid(c)).astype(BF16)
    o_ref[...] = jnp.dot(a, w_ref[...].astype(BF16), preferred_element_type=F32) + b_ref[...]


def _ada_call(c_pad, w_ada, b_ada):
    rows, d = c_pad.shape
    n = w_ada.shape[1]
    tn = min(n, 1024)
    return pl.pallas_call(
        _ada_kernel,
        grid=(n // tn,),
        in_specs=[pl.BlockSpec((rows, d), lambda j: (0, 0)),
                  pl.BlockSpec((d, tn), lambda j: (0, j)),
                  pl.BlockSpec((1, tn), lambda j: (0, j))],
        out_specs=pl.BlockSpec((rows, tn), lambda j: (0, j)),
        out_shape=jax.ShapeDtypeStruct((rows, n), F32),
        compiler_params=_cparams(("arbitrary",)),
        name="ada_ln",
    )(c_pad, w_ada, b_ada.reshape(1, n))


def _two_group_specs(tm, d, n_prompt_tiles, n_sample_tiles):
    xp = pl.BlockSpec((tm, d), lambda i, *_: (jnp.minimum(i, n_prompt_tiles - 1), 0))
    xs = pl.BlockSpec((tm, d), lambda i, *_: (jnp.clip(i - n_prompt_tiles, 0, n_sample_tiles - 1), 0))
    return xp, xs


def _norm_mod_kernel(xp_ref, xs_ref, mod_ref, g_ref, o_ref, *, n_prompt_tiles):
    i = pl.program_id(0)
    x = jnp.where(i < n_prompt_tiles, xp_ref[...], xs_ref[...])
    y = x * lax.rsqrt(jnp.mean(x * x, axis=-1, keepdims=True) + RMS_EPS) * g_ref[...]
    m = mod_ref[0]
    o_ref[...] = (y * (1.0 + m[1:2]) + m[0:1]).astype(BF16)


def _norm_mod_call(xp, xs, mod_seg, g, seg):
    d = xp.shape[1]
    tm = min(512, seg)
    npt, nst = xp.shape[0] // tm, xs.shape[0] // tm
    xp_spec, xs_spec = _two_group_specs(tm, d, npt, nst)
    return pl.pallas_call(
        functools.partial(_norm_mod_kernel, n_prompt_tiles=npt),
        grid=(npt + nst,),
        in_specs=[xp_spec, xs_spec,
                  pl.BlockSpec((1, 6, d), lambda i: (i * tm // seg, 0, 0)),
                  pl.BlockSpec((1, d), lambda i: (0, 0))],
        out_specs=pl.BlockSpec((tm, d), lambda i: (i, 0)),
        out_shape=jax.ShapeDtypeStruct((xp.shape[0] + xs.shape[0], d), BF16),
        compiler_params=_cparams(("arbitrary",)),
        name="norm1_mod",
    )(xp, xs, mod_seg, g.reshape(1, d))


def _group_rms(y, g, ones_ref):
    sq = (y * y).astype(BF16)
    parts = []
    for c in range(y.shape[1] // MXU_DIM):
        parts.append(jnp.dot(sq[:, c * MXU_DIM:(c + 1) * MXU_DIM], ones_ref[...], preferred_element_type=F32))
    ss = jnp.concatenate(parts, axis=1) if len(parts) > 1 else parts[0]
    return y * lax.rsqrt(ss * (1.0 / HEAD_DIM) + RMS_EPS) * g


INPROJ_ROW_CHUNKS = 2


def _inproj_kernel(h_ref, w_ref, g_ref, ones_ref, o_ref, *, tn, gate_tiles):
    j = pl.program_id(1)
    half = tn // 2
    mixed_tile = gate_tiles + COL_KA // tn
    plain_tile = gate_tiles + COL_VB // tn
    rows = h_ref.shape[0] // INPROJ_ROW_CHUNKS

    def chunked(epilogue):
        for c in range(INPROJ_ROW_CHUNKS):
            r = slice(c * rows, (c + 1) * rows)
            epilogue(jnp.dot(h_ref[r, :], w_ref[...], preferred_element_type=F32), r)

    @pl.when(j < gate_tiles)
    def _():
        def epilogue(y, r):
            o_ref[r, :] = (0.5 * jnp.tanh(0.5 * y) + 0.5).astype(BF16)
        chunked(epilogue)

    @pl.when((j >= gate_tiles) & (j < mixed_tile))
    def _():
        def epilogue(y, r):
            o_ref[r, :] = _group_rms(y, g_ref[...], ones_ref).astype(BF16)
        chunked(epilogue)

    @pl.when(j == mixed_tile)
    def _():
        def epilogue(y, r):
            o_ref[r, :half] = _group_rms(y[:, :half], g_ref[:, :half], ones_ref).astype(BF16)
            o_ref[r, half:] = y[:, half:].astype(BF16)
        chunked(epilogue)

    @pl.when(j == plain_tile)
    def _():
        def epilogue(y, r):
            o_ref[r, :] = y.astype(BF16)
        chunked(epilogue)


def _inproj_call(h, w, gvec, ones_bd, tm):
    t, d = h.shape
    pw = w.shape[1]
    tn = 1024
    assert (2 * d) % tn == 0 and COL_KA % tn == 0 and COL_VA - COL_KA == tn // 2 and COL_VB % tn == 0
    return pl.pallas_call(
        functools.partial(_inproj_kernel, tn=tn, gate_tiles=2 * d // tn),
        grid=(t // tm, pw // tn),
        in_specs=[pl.BlockSpec((tm, d), lambda i, j: (i, 0)),
                  pl.BlockSpec((d, tn), lambda i, j: (0, j)),
                  pl.BlockSpec((1, tn), lambda i, j: (0, j)),
                  pl.BlockSpec((MXU_DIM, MXU_DIM), lambda i, j: (0, 0))],
        out_specs=pl.BlockSpec((tm, tn), lambda i, j: (i, j)),
        out_shape=jax.ShapeDtypeStruct((t, pw), BF16),
        compiler_params=_cparams(("arbitrary", "arbitrary")),
        name="in_proj",
    )(h, w, gvec, ones_bd)


def _aligned(x, m):
    return x if isinstance(x, int) else pl.multiple_of(x, m)


def _three_stage_pipeline(n, stage_a, stage_b, stage_c, width=2):
    assert n % width == 0 and n // width >= 3
    trips = n // width

    def trip(t, run_a=True, run_b=True, run_c=True):
        for k in range(width):
            if run_c:
                stage_c(width * (t - 2) + k, k)
        for k in range(width):
            if run_b:
                stage_b(width * (t - 1) + k, k)
        for k in range(width):
            if run_a:
                stage_a(width * t + k, k)

    trip(0, run_b=False, run_c=False)
    trip(1, run_c=False)

    def body(t, carry):
        trip(t)
        return carry

    lax.fori_loop(2, trips, body, 0)
    trip(trips, run_a=False)
    trip(trips + 1, run_a=False, run_b=False)


WIN_WIDTH = 4
NA_WIDTH = 4
NA_VALUE_KEYS = -(-NA_KEY_ROWS * GRID_W // LANES) * LANES
CAST_ROWS = 512
MOE_CHUNKS = 4


class _CastRider:
    def __init__(self, w2d, col_block, n_cols, n_inner, n_steps):
        rows = w2d.shape[0]
        tr = min(CAST_ROWS, rows)
        self.n_chunks = rows // tr
        assert rows % tr == 0 and self.n_chunks <= n_steps
        self.operand = w2d

        def chunk(p, s, *_):
            return jnp.minimum(p * n_inner + s, self.n_chunks - 1)

        self.in_spec = pl.BlockSpec((tr, n_cols), lambda *g: (chunk(*g), col_block))
        self.out_spec = pl.BlockSpec((tr, n_cols), lambda *g: (chunk(*g), 0))
        self.out_shape = jax.ShapeDtypeStruct((rows, n_cols), BF16)


def _cast_rider_step(step, n_chunks, pairs):
    @pl.when(step < n_chunks)
    def _():
        for src, dst in pairs:
            dst[...] = src[...].astype(BF16)


def _win_attn_kernel(flags_ref, q_ref, kp_ref, kc_ref, kn_ref, vp_ref, vc_ref, vn_ref, bias_ref, sink_ref,
                     wc_ref, o_ref, oc_ref, kcat, vt, s_scr, e_scr, m_scr, r_scr, *, qs, nb, n_cast_chunks):
    p = pl.program_id(0)
    sb = pl.program_id(1)
    _cast_rider_step(p * pl.num_programs(1) + sb, n_cast_chunks, [(wc_ref, oc_ref)])
    blk = ATTN_BLOCK
    kcat[0:blk] = kp_ref[...]
    kcat[blk:blk + qs] = kc_ref[...]
    kcat[blk + qs:] = kn_ref[...]
    vt[:, 0:blk] = vp_ref[...].T[:HEAD_DIM]
    vt[:, blk:blk + qs] = vc_ref[...].T[:HEAD_DIM]
    vt[:, blk + qs:] = vn_ref[...].T[:HEAD_DIM]
    prev_ok = flags_ref[2 * sb]
    next_ok = flags_ref[2 * sb + 1]
    low_half = lax.broadcasted_iota(jnp.int32, (blk, LANES), 1) < HEAD_DIM
    lane2 = lax.broadcasted_iota(jnp.int32, (1, 2 * blk), 1)
    sinkrow = jnp.where(lane2 < blk, sink_ref[2 * p], sink_ref[2 * p + 1])
    kl = blk + 2 * WINDOW

    def scores(b, slot):
        r0 = _aligned(b * blk, blk)
        q = q_ref[pl.ds(r0, blk), :]
        zero = jnp.zeros_like(q)
        q2 = jnp.concatenate([jnp.where(low_half, q, zero), jnp.where(low_half, zero, q)], axis=0)
        kw = kcat[pl.ds(r0, kl), :]
        variant = jnp.where((b == 0) & (prev_ok == 0), 1, jnp.where((b == nb - 1) & (next_ok == 0), 2, 0))
        s = lax.dot_general(kw, q2, (((1,), (1,)), ((), ())), preferred_element_type=F32) + bias_ref[variant, 0]
        s_scr[slot] = s
        m_scr[slot] = jnp.maximum(jnp.max(s, axis=0, keepdims=True), sinkrow)

    def softmax(b, slot):
        m = m_scr[slot]
        e = jnp.exp2(s_scr[slot] - m)
        e_scr[slot] = e.astype(BF16)
        r_scr[slot] = 1.0 / (jnp.sum(e, axis=0, keepdims=True) + jnp.exp2(sinkrow - m))

    def values(b, slot):
        r0 = _aligned(b * blk, blk)
        ot = jnp.dot(vt[:, pl.ds(r0, kl)], e_scr[slot], preferred_element_type=F32) * r_scr[slot]
        x = jnp.concatenate([ot[:, :blk], ot[:, blk:]], axis=0)
        o_ref[pl.ds(r0, blk), :] = x.T.astype(BF16)

    _three_stage_pipeline(nb, scores, softmax, values, width=min(WIN_WIDTH, nb // 3))


def _win_attn_call(proj, bias3, sink, flags, qs, base, cast_src):
    t = proj.shape[0]
    blk = ATTN_BLOCK
    nb = qs // blk
    n_super = t // qs
    rider = _CastRider(*cast_src, n_inner=n_super, n_steps=N_HEAD_PAIRS * n_super)
    n_blk_rows = t // blk
    kl = blk + 2 * WINDOW
    ka0, va0 = base + COL_KA // LANES, base + COL_VA // LANES

    def cur(col0):
        return pl.BlockSpec((qs, LANES), lambda p, s, f: (s, col0 + p // 2))

    def prev(col0):
        return pl.BlockSpec((blk, LANES), lambda p, s, f: (jnp.maximum(s * nb - 1, 0), col0 + p // 2))

    def nxt(col0):
        return pl.BlockSpec((blk, LANES), lambda p, s, f: (jnp.minimum((s + 1) * nb, n_blk_rows - 1), col0 + p // 2))

    grid_spec = pltpu.PrefetchScalarGridSpec(
        num_scalar_prefetch=1,
        grid=(N_HEAD_PAIRS, n_super),
        in_specs=[pl.BlockSpec((qs, LANES), lambda p, s, f: (s, base + COL_QA // LANES + p)),
                  prev(ka0), cur(ka0), nxt(ka0), prev(va0), cur(va0), nxt(va0),
                  pl.BlockSpec((3, 1, kl, 2 * blk), lambda p, s, f: (0, p, 0, 0)),
                  pl.BlockSpec(memory_space=pltpu.SMEM),
                  rider.in_spec],
        out_specs=[pl.BlockSpec((qs, LANES), lambda p, s, f: (s, p)), rider.out_spec],
        scratch_shapes=[pltpu.VMEM((qs + 2 * blk, LANES), BF16), pltpu.VMEM((HEAD_DIM, qs + 2 * blk), BF16),
                        pltpu.VMEM((WIN_WIDTH, kl, 2 * blk), F32), pltpu.VMEM((WIN_WIDTH, kl, 2 * blk), BF16),
                        pltpu.VMEM((WIN_WIDTH, 1, 2 * blk), F32), pltpu.VMEM((WIN_WIDTH, 1, 2 * blk), F32)],
    )
    return pl.pallas_call(
        functools.partial(_win_attn_kernel, qs=qs, nb=nb, n_cast_chunks=rider.n_chunks),
        grid_spec=grid_spec,
        out_shape=[jax.ShapeDtypeStruct((t, Q_A), BF16), rider.out_shape],
        compiler_params=_cparams(("arbitrary", "arbitrary")),
        name="win_attn",
    )(flags, proj, proj, proj, proj, proj, proj, proj, bias3, sink, rider.operand)


def _t5_bucket(rel):
    nb = T5_BUCKETS // 2
    max_exact = nb // 2
    ret = jnp.where(rel > 0, nb, 0)
    n = jnp.abs(rel)
    nf = jnp.maximum(n, 1).astype(jnp.float32)
    large = max_exact + (jnp.log(nf / max_exact) / math.log(T5_MAX_DIST / max_exact) * (nb - max_exact)).astype(jnp.int32)
    large = jnp.minimum(large, nb - 1)
    return ret + jnp.where(n < max_exact, n, large)


def _toeplitz(v, n_rows, n_cols):
    p = v.shape[-1]
    assert n_cols <= p - 1
    flat = jnp.tile(v, (1,) * (v.ndim - 1) + (n_rows,))[..., :n_rows * (p - 1)]
    return flat.reshape(v.shape[:-1] + (n_rows, p - 1))[..., :n_cols]


def _window_bias(t5_bias):
    blk, kl = ATTN_BLOCK, ATTN_BLOCK + 2 * WINDOW
    period = blk + kl - 1
    m = np.arange(period)
    rel_of_m = np.where(m < kl, m, m - period) - WINDOW
    per_rel = t5_bias[_t5_bucket(jnp.asarray(rel_of_m))].astype(F32).T
    bias = _toeplitz(per_rel * LOG2_E, blk, kl)
    col = np.arange(kl)[None, :]
    band = np.abs(col - WINDOW - np.arange(blk)[:, None]) <= WINDOW
    keep = np.stack([band, band & (col >= WINDOW), band & (col < WINDOW + blk)])
    tab = jnp.where(keep[:, None], bias[None], MASKED)
    tab = tab.reshape(3, N_HEADS_A // 2, 2, blk, kl).transpose(0, 1, 4, 2, 3)
    return tab.reshape(3, N_HEADS_A // 2, kl, 2 * blk)


def _na_plan(seq_lens, seg):
    qr, nkr = NA_GROUP_ROWS, NA_KEY_ROWS
    configs, cfg_ids, kstarts = {}, [], []
    tok = 0
    for s_len in seq_lens:
        rows = s_len // GRID_W
        assert rows >= nkr and rows % qr == 0
        for r in range(0, rows, qr):
            us = min(max(r - NA_ROWS // 2, 0), rows - NA_ROWS)
            rel = tuple(min(max(r + q - NA_ROWS // 2, 0), rows - NA_ROWS) - us for q in range(qr))
            key = (r - us, rel)
            cfg_ids.append(configs.setdefault(key, len(configs)))
            g_tok = tok + r * GRID_W
            seg_tok0 = (g_tok // seg) * seg
            kstarts.append(tok + us * GRID_W - seg_tok0 + NA_HALO_ROWS * GRID_W)
        tok += s_len
    blocks = [None] * len(configs)
    for (r_us, rel), c in configs.items():
        blocks[c] = [[kr - r_us - q + NA_ROWS - 1 if rel[q] <= kr < rel[q] + NA_ROWS else None
                      for kr in range(nkr)] for q in range(qr)]
    return np.asarray(cfg_ids, np.int32), np.asarray(kstarts, np.int32), blocks


def _na_bias_tables(na_bias, blocks):
    period = 2 * GRID_W - 1
    m = np.arange(period)
    dcol = np.where(m < GRID_W, m, m - period)
    per_dcol = na_bias.astype(F32)[:, :, np.clip(dcol + NA_COLS - 1, 0, 2 * NA_COLS - 2)]
    col_tab = _toeplitz(per_dcol * LOG2_E, GRID_W, GRID_W)
    q_col, k_col = np.arange(GRID_W)[:, None], np.arange(GRID_W)[None, :]
    win_start = np.clip(q_col - NA_COLS // 2, 0, GRID_W - NA_COLS)
    col_ok = (k_col >= win_start) & (k_col < win_start + NA_COLS)
    col_tab = jnp.where(col_ok, col_tab, MASKED)
    masked_blk = jnp.full(col_tab.shape[:1] + col_tab.shape[2:], MASKED, F32)
    cfgs = []
    for cfg in blocks:
        rows = [jnp.concatenate([masked_blk if dr is None else col_tab[:, dr] for dr in qrow], axis=-1) for qrow in cfg]
        cfgs.append(jnp.concatenate(rows, axis=-2))
    tab = jnp.stack(cfgs)
    ncfg, n_h, m, nk = tab.shape
    tab = tab.reshape(ncfg, n_h // 2, 2, m, nk).transpose(0, 1, 4, 2, 3)
    return tab.reshape(ncfg, n_h // 2, nk, 2 * m)


def _na_attn_kernel(cfg_ref, ks_ref, q_ref, kp_ref, kc_ref, kn_ref, vp_ref, vc_ref, vn_ref, bias_ref,
                    wc1_ref, wc2_ref, o_ref, oc1_ref, oc2_ref, kcat, vt, s_scr, e_scr, m_scr, r_scr,
                    *, seg, n_cast_chunks):
    s_id = pl.program_id(1)
    _cast_rider_step(pl.program_id(0) * pl.num_programs(1) + s_id, n_cast_chunks,
                     [(wc1_ref, oc1_ref), (wc2_ref, oc2_ref)])
    halo = NA_HALO_ROWS * GRID_W
    m, nk = NA_GROUP_ROWS * GRID_W, NA_KEY_ROWS * GRID_W
    n_groups = seg // m
    kcat[0:halo] = kp_ref[...]
    kcat[halo:halo + seg] = kc_ref[...]
    kcat[halo + seg:] = kn_ref[...]
    vt[:, 0:halo] = vp_ref[...].T
    vt[:, halo:halo + seg] = vc_ref[...].T
    vt[:, halo + seg:] = vn_ref[...].T
    e_scr[:, nk:, :] = jnp.zeros((e_scr.shape[0], NA_VALUE_KEYS - nk, 2 * m), BF16)
    low_half = lax.broadcasted_iota(jnp.int32, (m, LANES), 1) < HEAD_DIM
    top_rows = lax.broadcasted_iota(jnp.int32, (LANES, m), 0) < HEAD_DIM

    def scores(g, slot):
        gg = s_id * n_groups + g
        ks = pl.multiple_of(ks_ref[gg], LANES)
        r0 = _aligned(g * m, m)
        q = q_ref[pl.ds(r0, m), :]
        zero = jnp.zeros_like(q)
        q2 = jnp.concatenate([jnp.where(low_half, q, zero), jnp.where(low_half, zero, q)], axis=0)
        kw = kcat[pl.ds(ks, nk), :]
        s = lax.dot_general(kw, q2, (((1,), (1,)), ((), ())), preferred_element_type=F32) + bias_ref[cfg_ref[gg], 0]
        s_scr[slot] = s
        m_scr[slot] = jnp.max(s, axis=0, keepdims=True)

    def softmax(g, slot):
        e = jnp.exp2(s_scr[slot] - m_scr[slot])
        e_scr[slot, :nk, :] = e.astype(BF16)
        r_scr[slot] = 1.0 / jnp.sum(e, axis=0, keepdims=True)

    def values(g, slot):
        ks = pl.multiple_of(ks_ref[s_id * n_groups + g], LANES)
        r0 = _aligned(g * m, m)
        ot = jnp.dot(vt[:, pl.ds(ks, NA_VALUE_KEYS)], e_scr[slot], preferred_element_type=F32) * r_scr[slot]
        x = jnp.where(top_rows, ot[:, :m], ot[:, m:])
        o_ref[pl.ds(r0, m), :] = x.T.astype(BF16)

    _three_stage_pipeline(n_groups, scores, softmax, values, width=min(NA_WIDTH, n_groups // 3))


def _na_attn_call(proj, bias_cfg, cfg_ids, kstarts, seg, base, cast_src1, cast_src2):
    t = proj.shape[0]
    halo = NA_HALO_ROWS * GRID_W
    n_seg = t // seg
    riders = [_CastRider(*src, n_inner=n_seg, n_steps=N_HEAD_PAIRS * n_seg) for src in (cast_src1, cast_src2)]
    assert riders[0].n_chunks == riders[1].n_chunks
    per = seg // halo
    n_halo_blocks = t // halo
    ncfg = bias_cfg.shape[0]
    m, nk = NA_GROUP_ROWS * GRID_W, NA_KEY_ROWS * GRID_W
    kb0, vb0 = base + COL_KB // LANES, base + COL_VB // LANES

    def cur(col0):
        return pl.BlockSpec((seg, LANES), lambda p, s, c, k: (s, col0 + p))

    def prev(col0):
        return pl.BlockSpec((halo, LANES), lambda p, s, c, k: (jnp.maximum(s * per - 1, 0), col0 + p))

    def nxt(col0):
        return pl.BlockSpec((halo, LANES), lambda p, s, c, k: (jnp.minimum((s + 1) * per, n_halo_blocks - 1), col0 + p))

    grid_spec = pltpu.PrefetchScalarGridSpec(
        num_scalar_prefetch=2,
        grid=(N_HEAD_PAIRS, n_seg),
        in_specs=[cur(base + COL_QB // LANES), prev(kb0), cur(kb0), nxt(kb0), prev(vb0), cur(vb0), nxt(vb0),
                  pl.BlockSpec((ncfg, 1, nk, 2 * m), lambda p, s, c, k: (0, p, 0, 0)),
                  riders[0].in_spec, riders[1].in_spec],
        out_specs=[pl.BlockSpec((seg, LANES), lambda p, s, c, k: (s, p)), riders[0].out_spec, riders[1].out_spec],
        scratch_shapes=[pltpu.VMEM((seg + 2 * halo, LANES), BF16), pltpu.VMEM((LANES, seg + 2 * halo), BF16),
                        pltpu.VMEM((NA_WIDTH, nk, 2 * m), F32), pltpu.VMEM((NA_WIDTH, NA_VALUE_KEYS, 2 * m), BF16),
                        pltpu.VMEM((NA_WIDTH, 1, 2 * m), F32), pltpu.VMEM((NA_WIDTH, 1, 2 * m), F32)],
    )
    return pl.pallas_call(
        functools.partial(_na_attn_kernel, seg=seg, n_cast_chunks=riders[0].n_chunks),
        grid_spec=grid_spec,
        out_shape=[jax.ShapeDtypeStruct((t, W_B), BF16), riders[0].out_shape, riders[1].out_shape],
        compiler_params=_cparams(("arbitrary", "arbitrary")),
        name="na_attn",
    )(cfg_ids, kstarts, proj, proj, proj, proj, proj, proj, proj, bias_cfg, riders[0].operand, riders[1].operand)


def _outproj_kernel(aa_ref, ab_ref, ga_ref, gb_ref, xp_ref, xs_ref, mod_ref, woa_ref, wob_ref, wout_ref,
                    g2_ref, wrh_ref, wrl_ref, br_ref, x1_ref, h2_ref, lg_ref, *, n_prompt_tiles):
    i = pl.program_id(0)
    ya = jnp.dot(aa_ref[...], woa_ref[...], preferred_element_type=F32)
    yb = jnp.dot(ab_ref[...], wob_ref[...], preferred_element_type=F32)
    merged = ga_ref[...].astype(F32) * ya + gb_ref[...].astype(F32) * yb
    z = jnp.dot(merged.astype(BF16), wout_ref[...], preferred_element_type=F32)
    x = jnp.where(i < n_prompt_tiles, xp_ref[...], xs_ref[...])
    m = mod_ref[0]
    x1 = x + m[2:3] * z
    x1_ref[...] = x1
    y = x1 * lax.rsqrt(jnp.mean(x1 * x1, axis=-1, keepdims=True) + RMS_EPS) * g2_ref[...]
    h2 = y * (1.0 + m[4:5]) + m[3:4]
    h2_hi = h2.astype(BF16)
    h2_ref[...] = h2_hi
    h2_lo = (h2 - h2_hi.astype(F32)).astype(BF16)
    lg = jnp.dot(h2_hi, wrh_ref[...], preferred_element_type=F32)
    lg = lg + jnp.dot(h2_lo, wrh_ref[...], preferred_element_type=F32)
    lg = lg + jnp.dot(h2_hi, wrl_ref[...], preferred_element_type=F32)
    lg_ref[...] = lg + br_ref[...]


def _outproj_call(attn_a, attn_b, proj, xp, xs, mod_seg, woa, wob, wout, g2, wr_hi, wr_lo, br, seg):
    t = attn_a.shape[0]
    d = xp.shape[1]
    tm = min(256, seg)
    npt, nst = xp.shape[0] // tm, xs.shape[0] // tm
    xp_spec, xs_spec = _two_group_specs(tm, d, npt, nst)
    def const(shape):
        return pl.BlockSpec(shape, lambda i: (0,) * len(shape), pipeline_mode=pl.Buffered(1))

    return pl.pallas_call(
        functools.partial(_outproj_kernel, n_prompt_tiles=npt),
        grid=(npt + nst,),
        in_specs=[pl.BlockSpec((tm, Q_A), lambda i: (i, 0)),
                  pl.BlockSpec((tm, W_B), lambda i: (i, 0)),
                  pl.BlockSpec((tm, d), lambda i: (i, 0)),
                  pl.BlockSpec((tm, d), lambda i: (i, 1)),
                  xp_spec, xs_spec,
                  pl.BlockSpec((1, 6, d), lambda i: (i * tm // seg, 0, 0)),
                  const((Q_A, d)), const((W_B, d)), const((d, d)), const((1, d)),
                  const((d, LANES)), const((d, LANES)), const((1, LANES))],
        out_specs=[pl.BlockSpec((tm, d), lambda i: (i, 0)),
                   pl.BlockSpec((tm, d), lambda i: (i, 0)),
                   pl.BlockSpec((tm, LANES), lambda i: (i, 0))],
        out_shape=[jax.ShapeDtypeStruct((t, d), F32),
                   jax.ShapeDtypeStruct((t, d), BF16),
                   jax.ShapeDtypeStruct((t, LANES), F32)],
        compiler_params=_cparams(("arbitrary",)),
        name="out_proj",
    )(attn_a, attn_b, proj, proj, xp, xs, mod_seg, woa, wob, wout, g2.reshape(1, d), wr_hi, wr_lo, br)


def _moe_kernel(te_ref, nv_ref, x_ref, wg_ref, wl_ref, bg_ref, bl_ref, wd_ref, bd_ref, *rest, nf, tile0):
    o_ref, acc_ref = rest[-2:]
    t = pl.program_id(0) + tile0
    f = pl.program_id(1)
    live = t < nv_ref[0]

    @pl.when((pl.program_id(0) == 0) & (f == 0))
    def _():
        acc_ref[...] = jnp.zeros_like(acc_ref)

    @pl.when(live)
    def _():
        x = x_ref[...]
        gate = jnp.dot(x, wg_ref[...], preferred_element_type=F32) + bg_ref[0]
        lin = jnp.dot(x, wl_ref[...], preferred_element_type=F32) + bl_ref[0]
        gate = jnp.minimum(gate, SWIGLU_LIMIT)
        lin = jnp.clip(lin, -SWIGLU_LIMIT, SWIGLU_LIMIT)
        act = gate * _sigmoid(SWIGLU_ALPHA * gate) * (lin + 1.0)
        part = jnp.dot(act.astype(BF16), wd_ref[...], preferred_element_type=F32)
        total = part + jnp.where(f == 0, bd_ref[0], acc_ref[...])
        acc_ref[...] = total
        o_ref[...] = total.astype(BF16)

    @pl.when(jnp.logical_not(live) & (f == 0))
    def _():
        o_ref[...] = jnp.zeros_like(o_ref)


def _moe_call(xin, tile_e, n_valid, w_gate, w_lin, b_gu, w_down, b_down, tm, tile0, n_rows_total, yb_prev):
    n_rows, d = xin.shape
    n_e, two_ff = b_gu.shape
    d_ff = two_ff // 2
    tf = min(512, d_ff)
    nf = d_ff // tf
    n_tiles = n_rows // tm

    def fidx(t, f, nv):
        return jnp.where(t + tile0 < nv[0], f, nf - 1)

    def expert(t, te):
        return te[t + tile0]

    in_specs = [pl.BlockSpec((tm, d), lambda t, f, te, nv: (t, 0)),
                pl.BlockSpec((d, tf), lambda t, f, te, nv: (expert(t, te), fidx(t, f, nv))),
                pl.BlockSpec((d, tf), lambda t, f, te, nv: (expert(t, te), fidx(t, f, nv))),
                pl.BlockSpec((1, 1, tf), lambda t, f, te, nv: (expert(t, te), 0, fidx(t, f, nv))),
                pl.BlockSpec((1, 1, tf), lambda t, f, te, nv: (expert(t, te), 0, fidx(t, f, nv) + nf)),
                pl.BlockSpec((tf, d), lambda t, f, te, nv: (expert(t, te) * nf + fidx(t, f, nv), 0)),
                pl.BlockSpec((1, 1, d), lambda t, f, te, nv: (expert(t, te), 0, 0))]
    operands = [tile_e, n_valid, xin, w_gate, w_lin, b_gu.reshape(n_e, 1, two_ff), b_gu.reshape(n_e, 1, two_ff),
                w_down, b_down.reshape(n_e, 1, d)]
    aliases = {}
    if yb_prev is not None:
        in_specs.append(pl.BlockSpec(memory_space=pl.ANY))
        aliases = {len(operands): 0}
        operands.append(yb_prev)
    grid_spec = pltpu.PrefetchScalarGridSpec(
        num_scalar_prefetch=2,
        grid=(n_tiles, nf),
        in_specs=in_specs,
        out_specs=pl.BlockSpec((tm, d), lambda t, f, te, nv: (t + tile0, 0)),
        scratch_shapes=[pltpu.VMEM((tm, d), F32)],
    )
    return pl.pallas_call(
        functools.partial(_moe_kernel, nf=nf, tile0=tile0),
        grid_spec=grid_spec,
        out_shape=jax.ShapeDtypeStruct((n_rows_total, d), BF16),
        input_output_aliases=aliases,
        compiler_params=_cparams(("arbitrary", "arbitrary")),
        name="moe_experts",
    )(*operands)


def _combine_kernel(x1_ref, ys_ref, w_ref, mod_ref, *rest):
    o_ref = rest[-1]
    w = w_ref[...]
    acc = w[:, 0:1] * ys_ref[0].astype(F32)
    for k in range(1, TOP_K):
        acc = acc + w[:, k:k + 1] * ys_ref[k].astype(F32)
    o_ref[...] = x1_ref[...] + mod_ref[0][5:6] * acc


def _combine_call(x1, ysel, top_w, mod_seg, row0, out_row0, n_out_rows, seg, prev):
    d = x1.shape[1]
    n_rows = ysel.shape[1]
    tm = min(512, seg)
    off, out_off = row0 // tm, out_row0 // tm
    in_specs = [pl.BlockSpec((tm, d), lambda i: (i + off, 0)),
                pl.BlockSpec((TOP_K, tm, d), lambda i: (0, i, 0)),
                pl.BlockSpec((tm, TOP_K), lambda i: (i + off, 0)),
                pl.BlockSpec((1, 6, d), lambda i: ((i + off) * tm // seg, 0, 0))]
    operands = [x1, ysel, top_w, mod_seg]
    aliases = {}
    if prev is not None:
        in_specs.append(pl.BlockSpec(memory_space=pl.ANY))
        aliases = {len(operands): 0}
        operands.append(prev)
    return pl.pallas_call(
        _combine_kernel,
        grid=(n_rows // tm,),
        in_specs=in_specs,
        out_specs=pl.BlockSpec((tm, d), lambda i: (i + out_off, 0)),
        out_shape=jax.ShapeDtypeStruct((n_out_rows, d), F32),
        input_output_aliases=aliases,
        compiler_params=_cparams(("arbitrary",)),
        name="moe_combine",
    )(*operands)


def _pack_w_in(w_in, g_q_a, g_k_a, g_q_b, g_k_b):
    o1 = Q_A
    o2 = o1 + KV_A
    o3 = o2 + KV_A
    o4 = o3 + W_B
    o5 = o4 + W_B
    o6 = o5 + W_B
    d = w_in.shape[0]
    wqa, wka, wva = w_in[:, :o1], w_in[:, o1:o2], w_in[:, o2:o3]
    wqb, wkb, wvb, wg = w_in[:, o3:o4], w_in[:, o4:o5], w_in[:, o5:o6], w_in[:, o6:]

    def dup(w):
        w4 = w.reshape(d, N_KV_A, 1, HEAD_DIM)
        return jnp.broadcast_to(w4, (d, N_KV_A, 2, HEAD_DIM)).reshape(d, N_KV_A * LANES)

    w = jnp.concatenate([wg, wqa, wqb, wkb, dup(wka), dup(wva), wvb], axis=1).astype(BF16)
    q_scale = HEAD_DIM ** -0.5 * LOG2_E
    gvec = jnp.concatenate([jnp.ones((2 * d,), F32),
                            jnp.tile(g_q_a * q_scale, N_HEADS_A), jnp.tile(g_q_b * q_scale, N_HEADS_B),
                            jnp.tile(g_k_b, N_HEADS_B), jnp.tile(g_k_a, 2 * N_KV_A),
                            jnp.ones((COL_END - COL_VA,), F32)]).reshape(1, -1).astype(F32)
    return w, gvec


def kernel(x_prompt, x_sample, c_prompt, c_sample, w_ada, b_ada, g_norm1, w_in, g_q_a, g_k_a, g_q_b, g_k_b, sink_a, t5_bias, na_bias, w_o_a, w_o_b, w_out, g_norm2, w_router, b_router, w_gu, b_gu, w_down, b_down):
    bp, sp, d = x_prompt.shape
    bs, ss, _ = x_sample.shape
    n_e = w_router.shape[-1]
    tp, ts = bp * sp, bs * ss
    t = tp + ts
    seg = math.gcd(math.gcd(sp, ss), 2048)
    assert w_ada.shape[0] == 1 and sp % seg == 0 and ss % seg == 0 and seg % (NA_HALO_ROWS * GRID_W) == 0
    assert 2 * d == w_in.shape[-1] - (Q_A + 2 * KV_A + 3 * W_B)
    base = 2 * d // LANES
    xp = x_prompt.reshape(tp, d)
    xs = x_sample.reshape(ts, d)

    n_c = bp + bs
    c_pad = jnp.zeros((-(-n_c // 16) * 16, d), F32).at[:n_c].set(jnp.concatenate([c_prompt, c_sample], axis=0))
    mod = _ada_call(c_pad, w_ada[0], b_ada[0])[:n_c].reshape(n_c, 6, d)
    seg_owner = np.concatenate([np.repeat(np.arange(bp), sp // seg), bp + np.repeat(np.arange(bs), ss // seg)])
    mod_seg = mod[seg_owner]

    h1 = _norm_mod_call(xp, xs, mod_seg, g_norm1[0], seg)

    w_slab, gvec = _pack_w_in(w_in[0], g_q_a[0], g_k_a[0], g_q_b[0], g_k_b[0])
    grp = np.arange(MXU_DIM) // HEAD_DIM
    ones_bd = jnp.asarray(grp[:, None] == grp[None, :], BF16)
    proj = _inproj_call(h1, w_slab, gvec, ones_bd, tm=min(1024, seg))

    qs = min(2048, seg)
    sb_tok = np.arange(t // qs) * qs
    seq_start = np.where(sb_tok < tp, sb_tok // sp * sp, tp + (sb_tok - tp) // ss * ss)
    seq_len = np.where(sb_tok < tp, sp, ss)
    flags = np.stack([sb_tok != seq_start, sb_tok + qs != seq_start + seq_len], axis=1).astype(np.int32).reshape(-1)
    n_e, d_ff = w_down.shape[1], w_down.shape[2]
    assert d_ff == d and w_gu.shape[1:] == (n_e, d, 2 * d_ff)
    w_gu2d = w_gu[0].reshape(n_e * d, 2 * d_ff)
    w_down2d = w_down[0].reshape(n_e * d_ff, d)
    attn_a, w_gate = _win_attn_call(proj, _window_bias(t5_bias), sink_a[0].astype(F32) * LOG2_E, jnp.asarray(flags), qs, base,
                                    (w_gu2d, 0, d_ff))

    cfg_ids, kstarts, cfg_blocks = _na_plan([sp] * bp + [ss] * bs, seg)
    bias_cfg = _na_bias_tables(na_bias[0], cfg_blocks)
    attn_b, w_lin, w_dn = _na_attn_call(proj, bias_cfg, jnp.asarray(cfg_ids), jnp.asarray(kstarts), seg, base,
                                        (w_gu2d, 1, d_ff), (w_down2d, 0, d))

    wr = jnp.zeros((d, LANES), F32).at[:, :n_e].set(w_router[0])
    wr_hi = wr.astype(BF16)
    wr_lo = (wr - wr_hi.astype(F32)).astype(BF16)
    br = jnp.zeros((1, LANES), F32).at[0, :n_e].set(b_router[0])
    x1, h2, logits = _outproj_call(attn_a, attn_b, proj, xp, xs, mod_seg, w_o_a[0].astype(BF16),
                                   w_o_b[0].astype(BF16), w_out[0].astype(BF16), g_norm2[0], wr_hi, wr_lo, br, seg)

    tm_e = min(1024, seg)
    top_logits, top_idx = lax.top_k(logits[:, :n_e], TOP_K)
    top_w = jax.nn.softmax(top_logits, axis=-1)
    n_assign = t * TOP_K
    e_flat = top_idx.reshape(-1)
    onehot = (e_flat[:, None] == jnp.arange(n_e, dtype=e_flat.dtype)[None, :]).astype(jnp.int32)
    csum = jnp.cumsum(onehot, axis=0)
    rank = jnp.take_along_axis(csum, e_flat[:, None], axis=1)[:, 0] - 1
    counts = csum[-1]
    padded = (counts + tm_e - 1) // tm_e * tm_e
    pend = jnp.cumsum(padded)
    dest = (pend - padded)[e_flat] + rank
    n_tiles = -(-n_assign // tm_e) + n_e
    n_rows = n_tiles * tm_e
    tile_e = jnp.minimum(jnp.searchsorted(pend, jnp.arange(n_tiles) * tm_e, side='right'), n_e - 1).astype(jnp.int32)
    n_valid = (pend[-1:] // tm_e).astype(jnp.int32)
    tok_flat = jnp.arange(n_assign, dtype=jnp.int32) // TOP_K
    _, tok_sorted = lax.sort((e_flat, tok_flat), num_keys=1, is_stable=True)
    tile_src0 = (jnp.cumsum(counts) - counts)[tile_e] + jnp.arange(n_tiles, dtype=jnp.int32) * tm_e - (pend - padded)[tile_e]
    tile_end = jnp.cumsum(counts)[tile_e]
    src = tile_src0[:, None] + jnp.arange(tm_e, dtype=jnp.int32)[None, :]
    slot_tok = jnp.where(src < tile_end[:, None], tok_sorted[jnp.minimum(src, n_assign - 1)],
                         (jnp.arange(n_rows, dtype=jnp.int32) % t).reshape(n_tiles, tm_e))

    n_chunks = math.gcd(n_tiles, MOE_CHUNKS)
    tiles_per_chunk = n_tiles // n_chunks
    yb = None
    for c in range(n_chunks):
        tiles = slice(c * tiles_per_chunk, (c + 1) * tiles_per_chunk)
        xin = h2[slot_tok[tiles].reshape(-1)]
        yb = _moe_call(xin, tile_e, n_valid, w_gate, w_lin, b_gu[0], w_dn, b_down[0], tm_e,
                       c * tiles_per_chunk, n_rows, yb)

    dest_kt = dest.reshape(t, TOP_K).T
    chunk = math.gcd(tp, ts)
    if t // chunk > 8:
        chunk = 0
    outs = []
    for row0, n_group in ((0, tp), (tp, ts)):
        step = chunk if chunk else n_group
        y = None
        for r in range(0, n_group, step):
            ysel = yb[dest_kt[:, row0 + r:row0 + r + step]]
            y = _combine_call(x1, ysel, top_w, mod_seg, row0 + r, r, n_group, seg, y)
        outs.append(y)
    return (outs[0].reshape(bp, sp, d), outs[1].reshape(bs, ss, d))
```

```python
import functools
import math

import numpy as np
import jax
import jax.numpy as jnp
from jax import lax
from jax.experimental import pallas as pl
from jax.experimental.pallas import tpu as pltpu

F32 = jnp.float32
BF16 = jnp.bfloat16

HEAD_DIM = 64
N_HEADS_A = 16
N_KV_A = 4
GROUP_A = N_HEADS_A // N_KV_A
N_HEADS_B = 16
WINDOW = 128
ATTN_BLOCK = 128
T5_BUCKETS = 32
T5_MAX_DIST = 128
GRID_W = 64
NA_ROWS = 8
NA_COLS = 16
TOP_K = 4
SWIGLU_LIMIT = 7.0
SWIGLU_ALPHA = 1.702
RMS_EPS = 1e-6

Q_A = N_HEADS_A * HEAD_DIM
KV_A = N_KV_A * HEAD_DIM
W_B = N_HEADS_B * HEAD_DIM

LANES = 128
MXU_DIM = 256
VMEM_LIMIT_BYTES = 56 * 1024 * 1024
MASKED = -1e30
LOG2_E = math.log2(math.e)

N_HEAD_PAIRS = Q_A // LANES
COL_QA = 0
COL_QB = COL_QA + Q_A
COL_KB = COL_QB + W_B
COL_KA = COL_KB + W_B
COL_VA = COL_KA + N_KV_A * LANES
COL_VB = COL_VA + N_KV_A * LANES
COL_END = COL_VB + W_B
NA_GROUP_ROWS = 2
NA_KEY_ROWS = NA_GROUP_ROWS + NA_ROWS - 1
NA_HALO_ROWS = NA_ROWS


def _sigmoid(x):
    return 1.0 / (1.0 + jnp.exp(-x))


def _cparams(sem):
    return pltpu.CompilerParams(dimension_semantics=sem, vmem_limit_bytes=VMEM_LIMIT_BYTES)


def _ada_kernel(c_ref, w_ref, b_ref, o_ref):
    c = c_ref[...]
    a = (c * _sigmoid(c)).astype(BF16)
    o_ref[...] = jnp.dot(a, w_ref[...].astype(BF16), preferred_element_type=F32) + b_ref[...]


def _ada_call(c_pad, w_ada, b_ada):
    rows, d = c_pad.shape
    n = w_ada.shape[1]
    tn = min(n, 1024)
    return pl.pallas_call(
        _ada_kernel,
        grid=(n // tn,),
        in_specs=[pl.BlockSpec((rows, d), lambda j: (0, 0)),
                  pl.BlockSpec((d, tn), lambda j: (0, j)),
                  pl.BlockSpec((1, tn), lambda j: (0, j))],
        out_specs=pl.BlockSpec((rows, tn), lambda j: (0, j)),
        out_shape=jax.ShapeDtypeStruct((rows, n), F32),
        compiler_params=_cparams(("arbitrary",)),
        name="ada_ln",
    )(c_pad, w_ada, b_ada.reshape(1, n))


def _two_group_specs(tm, d, n_prompt_tiles, n_sample_tiles):
    xp = pl.BlockSpec((tm, d), lambda i, *_: (jnp.minimum(i, n_prompt_tiles - 1), 0))
    xs = pl.BlockSpec((tm, d), lambda i, *_: (jnp.clip(i - n_prompt_tiles, 0, n_sample_tiles - 1), 0))
    return xp, xs


def _norm_mod_kernel(xp_ref, xs_ref, mod_ref, g_ref, o_ref, *, n_prompt_tiles):
    i = pl.program_id(0)
    x = jnp.where(i < n_prompt_tiles, xp_ref[...], xs_ref[...])
    y = x * lax.rsqrt(jnp.mean(x * x, axis=-1, keepdims=True) + RMS_EPS) * g_ref[...]
    m = mod_ref[0]
    o_ref[...] = (y * (1.0 + m[1:2]) + m[0:1]).astype(BF16)


def _norm_mod_call(xp, xs, mod_seg, g, seg):
    d = xp.shape[1]
    tm = min(512, seg)
    npt, nst = xp.shape[0] // tm, xs.shape[0] // tm
    xp_spec, xs_spec = _two_group_specs(tm, d, npt, nst)
    return pl.pallas_call(
        functools.partial(_norm_mod_kernel, n_prompt_tiles=npt),
        grid=(npt + nst,),
        in_specs=[xp_spec, xs_spec,
                  pl.BlockSpec((1, 6, d), lambda i: (i * tm // seg, 0, 0)),
                  pl.BlockSpec((1, d), lambda i: (0, 0))],
        out_specs=pl.BlockSpec((tm, d), lambda i: (i, 0)),
        out_shape=jax.ShapeDtypeStruct((xp.shape[0] + xs.shape[0], d), BF16),
        compiler_params=_cparams(("arbitrary",)),
        name="norm1_mod",
    )(xp, xs, mod_seg, g.reshape(1, d))


def _group_rms(y, g, ones_ref):
    sq = (y * y).astype(BF16)
    parts = []
    for c in range(y.shape[1] // MXU_DIM):
        parts.append(jnp.dot(sq[:, c * MXU_DIM:(c + 1) * MXU_DIM], ones_ref[...], preferred_element_type=F32))
    ss = jnp.concatenate(parts, axis=1) if len(parts) > 1 else parts[0]
    return y * lax.rsqrt(ss * (1.0 / HEAD_DIM) + RMS_EPS) * g


INPROJ_ROW_CHUNKS = 2
INPROJ_CAST_ROWS = 256


def _inproj_kernel(h_ref, w_ref, g_ref, ones_ref, wc_ref, o_ref, oc_ref, *, tn, gate_tiles, n_cast_chunks):
    j = pl.program_id(1)
    _cast_rider_step(pl.program_id(0) * pl.num_programs(1) + j, n_cast_chunks, [(wc_ref, oc_ref)])
    half = tn // 2
    mixed_tile = gate_tiles + COL_KA // tn
    plain_tile = gate_tiles + COL_VB // tn
    rows = h_ref.shape[0] // INPROJ_ROW_CHUNKS

    def chunked(epilogue):
        for c in range(INPROJ_ROW_CHUNKS):
            r = slice(c * rows, (c + 1) * rows)
            epilogue(jnp.dot(h_ref[r, :], w_ref[...], preferred_element_type=F32), r)

    @pl.when(j < gate_tiles)
    def _():
        def epilogue(y, r):
            o_ref[r, :] = (0.5 * jnp.tanh(0.5 * y) + 0.5).astype(BF16)
        chunked(epilogue)

    @pl.when((j >= gate_tiles) & (j < mixed_tile))
    def _():
        def epilogue(y, r):
            o_ref[r, :] = _group_rms(y, g_ref[...], ones_ref).astype(BF16)
        chunked(epilogue)

    @pl.when(j == mixed_tile)
    def _():
        def epilogue(y, r):
            o_ref[r, :half] = _group_rms(y[:, :half], g_ref[:, :half], ones_ref).astype(BF16)
            o_ref[r, half:] = y[:, half:].astype(BF16)
        chunked(epilogue)

    @pl.when(j == plain_tile)
    def _():
        def epilogue(y, r):
            o_ref[r, :] = y.astype(BF16)
        chunked(epilogue)


def _inproj_call(h, w, gvec, ones_bd, tm, cast_src):
    t, d = h.shape
    pw = w.shape[1]
    tn = 1024
    assert (2 * d) % tn == 0 and COL_KA % tn == 0 and COL_VA - COL_KA == tn // 2 and COL_VB % tn == 0
    grid = (t // tm, pw // tn)
    rider = _CastRider(*cast_src, n_inner=grid[1], n_steps=grid[0] * grid[1], tr=INPROJ_CAST_ROWS)
    return pl.pallas_call(
        functools.partial(_inproj_kernel, tn=tn, gate_tiles=2 * d // tn, n_cast_chunks=rider.n_chunks),
        grid=grid,
        in_specs=[pl.BlockSpec((tm, d), lambda i, j: (i, 0)),
                  pl.BlockSpec((d, tn), lambda i, j: (0, j)),
                  pl.BlockSpec((1, tn), lambda i, j: (0, j)),
                  pl.BlockSpec((MXU_DIM, MXU_DIM), lambda i, j: (0, 0)),
                  rider.in_spec],
        out_specs=[pl.BlockSpec((tm, tn), lambda i, j: (i, j)), rider.out_spec],
        out_shape=[jax.ShapeDtypeStruct((t, pw), BF16), rider.out_shape],
        compiler_params=_cparams(("arbitrary", "arbitrary")),
        name="in_proj",
    )(h, w, gvec, ones_bd, rider.operand)


def _aligned(x, m):
    return x if isinstance(x, int) else pl.multiple_of(x, m)


def _three_stage_pipeline(n, stage_a, stage_b, stage_c, width=2):
    assert n % width == 0 and n // width >= 3
    trips = n // width

    def trip(t, run_a=True, run_b=True, run_c=True):
        for k in range(width):
            if run_c:
                stage_c(width * (t - 2) + k, k)
        for k in range(width):
            if run_b:
                stage_b(width * (t - 1) + k, k)
        for k in range(width):
            if run_a:
                stage_a(width * t + k, k)

    trip(0, run_b=False, run_c=False)
    trip(1, run_c=False)

    def body(t, carry):
        trip(t)
        return carry

    lax.fori_loop(2, trips, body, 0)
    trip(trips, run_a=False)
    trip(trips + 1, run_a=False, run_b=False)


WIN_WIDTH = 4
NA_WIDTH = 4
NA_VALUE_KEYS = -(-NA_KEY_ROWS * GRID_W // LANES) * LANES
CAST_ROWS = 512
MOE_CHUNKS = 4


class _CastRider:
    def __init__(self, w2d, col_block, n_cols, n_inner, n_steps, tr=CAST_ROWS):
        rows = w2d.shape[0]
        tr = min(tr, rows)
        self.n_chunks = rows // tr
        assert rows % tr == 0 and self.n_chunks <= n_steps
        self.operand = w2d

        def chunk(p, s, *_):
            return jnp.minimum(p * n_inner + s, self.n_chunks - 1)

        self.in_spec = pl.BlockSpec((tr, n_cols), lambda *g: (chunk(*g), col_block))
        self.out_spec = pl.BlockSpec((tr, n_cols), lambda *g: (chunk(*g), 0))
        self.out_shape = jax.ShapeDtypeStruct((rows, n_cols), BF16)


def _cast_rider_step(step, n_chunks, pairs):
    @pl.when(step < n_chunks)
    def _():
        for src, dst in pairs:
            dst[...] = src[...].astype(BF16)


def _win_attn_kernel(flags_ref, q_ref, kp_ref, kc_ref, kn_ref, vp_ref, vc_ref, vn_ref, bias_ref, sink_ref,
                     wc_ref, o_ref, oc_ref, kcat, vt, s_scr, e_scr, m_scr, r_scr, *, qs, nb, n_cast_chunks):
    p = pl.program_id(0)
    sb = pl.program_id(1)
    _cast_rider_step(p * pl.num_programs(1) + sb, n_cast_chunks, [(wc_ref, oc_ref)])
    blk = ATTN_BLOCK
    kcat[0:blk] = kp_ref[...]
    kcat[blk:blk + qs] = kc_ref[...]
    kcat[blk + qs:] = kn_ref[...]
    vt[:, 0:blk] = vp_ref[...].T[:HEAD_DIM]
    vt[:, blk:blk + qs] = vc_ref[...].T[:HEAD_DIM]
    vt[:, blk + qs:] = vn_ref[...].T[:HEAD_DIM]
    prev_ok = flags_ref[2 * sb]
    next_ok = flags_ref[2 * sb + 1]
    low_half = lax.broadcasted_iota(jnp.int32, (blk, LANES), 1) < HEAD_DIM
    lane2 = lax.broadcasted_iota(jnp.int32, (1, 2 * blk), 1)
    sinkrow = jnp.where(lane2 < blk, sink_ref[2 * p], sink_ref[2 * p + 1])
    kl = blk + 2 * WINDOW

    def scores(b, slot):
        r0 = _aligned(b * blk, blk)
        q = q_ref[pl.ds(r0, blk), :]
        zero = jnp.zeros_like(q)
        q2 = jnp.concatenate([jnp.where(low_half, q, zero), jnp.where(low_half, zero, q)], axis=0)
        kw = kcat[pl.ds(r0, kl), :]
        variant = jnp.where((b == 0) & (prev_ok == 0), 1, jnp.where((b == nb - 1) & (next_ok == 0), 2, 0))
        s = lax.dot_general(kw, q2, (((1,), (1,)), ((), ())), preferred_element_type=F32) + bias_ref[variant, 0]
        s_scr[slot] = s
        m_scr[slot] = jnp.maximum(jnp.max(s, axis=0, keepdims=True), sinkrow)

    def softmax(b, slot):
        m = m_scr[slot]
        e = jnp.exp2(s_scr[slot] - m)
        e_scr[slot] = e.astype(BF16)
        r_scr[slot] = 1.0 / (jnp.sum(e, axis=0, keepdims=True) + jnp.exp2(sinkrow - m))

    def values(b, slot):
        r0 = _aligned(b * blk, blk)
        ot = jnp.dot(vt[:, pl.ds(r0, kl)], e_scr[slot], preferred_element_type=F32) * r_scr[slot]
        x = jnp.concatenate([ot[:, :blk], ot[:, blk:]], axis=0)
        o_ref[pl.ds(r0, blk), :] = x.T.astype(BF16)

    _three_stage_pipeline(nb, scores, softmax, values, width=min(WIN_WIDTH, nb // 3))


def _win_attn_call(proj, bias3, sink, flags, qs, base, cast_src):
    t = proj.shape[0]
    blk = ATTN_BLOCK
    nb = qs // blk
    n_super = t // qs
    rider = _CastRider(*cast_src, n_inner=n_super, n_steps=N_HEAD_PAIRS * n_super)
    n_blk_rows = t // blk
    kl = blk + 2 * WINDOW
    ka0, va0 = base + COL_KA // LANES, base + COL_VA // LANES

    def cur(col0):
        return pl.BlockSpec((qs, LANES), lambda p, s, f: (s, col0 + p // 2))

    def prev(col0):
        return pl.BlockSpec((blk, LANES), lambda p, s, f: (jnp.maximum(s * nb - 1, 0), col0 + p // 2))

    def nxt(col0):
        return pl.BlockSpec((blk, LANES), lambda p, s, f: (jnp.minimum((s + 1) * nb, n_blk_rows - 1), col0 + p // 2))

    grid_spec = pltpu.PrefetchScalarGridSpec(
        num_scalar_prefetch=1,
        grid=(N_HEAD_PAIRS, n_super),
        in_specs=[pl.BlockSpec((qs, LANES), lambda p, s, f: (s, base + COL_QA // LANES + p)),
                  prev(ka0), cur(ka0), nxt(ka0), prev(va0), cur(va0), nxt(va0),
                  pl.BlockSpec((3, 1, kl, 2 * blk), lambda p, s, f: (0, p, 0, 0)),
                  pl.BlockSpec(memory_space=pltpu.SMEM),
                  rider.in_spec],
        out_specs=[pl.BlockSpec((qs, LANES), lambda p, s, f: (s, p)), rider.out_spec],
        scratch_shapes=[pltpu.VMEM((qs + 2 * blk, LANES), BF16), pltpu.VMEM((HEAD_DIM, qs + 2 * blk), BF16),
                        pltpu.VMEM((WIN_WIDTH, kl, 2 * blk), F32), pltpu.VMEM((WIN_WIDTH, kl, 2 * blk), BF16),
                        pltpu.VMEM((WIN_WIDTH, 1, 2 * blk), F32), pltpu.VMEM((WIN_WIDTH, 1, 2 * blk), F32)],
    )
    return pl.pallas_call(
        functools.partial(_win_attn_kernel, qs=qs, nb=nb, n_cast_chunks=rider.n_chunks),
        grid_spec=grid_spec,
        out_shape=[jax.ShapeDtypeStruct((t, Q_A), BF16), rider.out_shape],
        compiler_params=_cparams(("arbitrary", "arbitrary")),
        name="win_attn",
    )(flags, proj, proj, proj, proj, proj, proj, proj, bias3, sink, rider.operand)


def _t5_bucket(rel):
    nb = T5_BUCKETS // 2
    max_exact = nb // 2
    ret = jnp.where(rel > 0, nb, 0)
    n = jnp.abs(rel)
    nf = jnp.maximum(n, 1).astype(jnp.float32)
    large = max_exact + (jnp.log(nf / max_exact) / math.log(T5_MAX_DIST / max_exact) * (nb - max_exact)).astype(jnp.int32)
    large = jnp.minimum(large, nb - 1)
    return ret + jnp.where(n < max_exact, n, large)


def _toeplitz(v, n_rows, n_cols):
    p = v.shape[-1]
    assert n_cols <= p - 1
    flat = jnp.tile(v, (1,) * (v.ndim - 1) + (n_rows,))[..., :n_rows * (p - 1)]
    return flat.reshape(v.shape[:-1] + (n_rows, p - 1))[..., :n_cols]


def _window_bias(t5_bias):
    blk, kl = ATTN_BLOCK, ATTN_BLOCK + 2 * WINDOW
    period = blk + kl - 1
    m = np.arange(period)
    rel_of_m = np.where(m < kl, m, m - period) - WINDOW
    per_rel = t5_bias[_t5_bucket(jnp.asarray(rel_of_m))].astype(F32).T
    bias = _toeplitz(per_rel * LOG2_E, blk, kl)
    col = np.arange(kl)[None, :]
    band = np.abs(col - WINDOW - np.arange(blk)[:, None]) <= WINDOW
    keep = np.stack([band, band & (col >= WINDOW), band & (col < WINDOW + blk)])
    tab = jnp.where(keep[:, None], bias[None], MASKED)
    tab = tab.reshape(3, N_HEADS_A // 2, 2, blk, kl).transpose(0, 1, 4, 2, 3)
    return tab.reshape(3, N_HEADS_A // 2, kl, 2 * blk)


def _na_plan(seq_lens, seg):
    qr, nkr = NA_GROUP_ROWS, NA_KEY_ROWS
    configs, cfg_ids, kstarts = {}, [], []
    tok = 0
    for s_len in seq_lens:
        rows = s_len // GRID_W
        assert rows >= nkr and rows % qr == 0
        for r in range(0, rows, qr):
            us = min(max(r - NA_ROWS // 2, 0), rows - NA_ROWS)
            rel = tuple(min(max(r + q - NA_ROWS // 2, 0), rows - NA_ROWS) - us for q in range(qr))
            key = (r - us, rel)
            cfg_ids.append(configs.setdefault(key, len(configs)))
            g_tok = tok + r * GRID_W
            seg_tok0 = (g_tok // seg) * seg
            kstarts.append(tok + us * GRID_W - seg_tok0 + NA_HALO_ROWS * GRID_W)
        tok += s_len
    blocks = [None] * len(configs)
    for (r_us, rel), c in configs.items():
        blocks[c] = [[kr - r_us - q + NA_ROWS - 1 if rel[q] <= kr < rel[q] + NA_ROWS else None
                      for kr in range(nkr)] for q in range(qr)]
    return np.asarray(cfg_ids, np.int32), np.asarray(kstarts, np.int32), blocks


def _na_bias_tables(na_bias, blocks):
    period = 2 * GRID_W - 1
    m = np.arange(period)
    dcol = np.where(m < GRID_W, m, m - period)
    per_dcol = na_bias.astype(F32)[:, :, np.clip(dcol + NA_COLS - 1, 0, 2 * NA_COLS - 2)]
    col_tab = _toeplitz(per_dcol * LOG2_E, GRID_W, GRID_W)
    q_col, k_col = np.arange(GRID_W)[:, None], np.arange(GRID_W)[None, :]
    win_start = np.clip(q_col - NA_COLS // 2, 0, GRID_W - NA_COLS)
    col_ok = (k_col >= win_start) & (k_col < win_start + NA_COLS)
    col_tab = jnp.where(col_ok, col_tab, MASKED)
    masked_blk = jnp.full(col_tab.shape[:1] + col_tab.shape[2:], MASKED, F32)
    cfgs = []
    for cfg in blocks:
        rows = [jnp.concatenate([masked_blk if dr is None else col_tab[:, dr] for dr in qrow], axis=-1) for qrow in cfg]
        cfgs.append(jnp.concatenate(rows, axis=-2))
    tab = jnp.stack(cfgs)
    ncfg, n_h, m, nk = tab.shape
    tab = tab.reshape(ncfg, n_h // 2, 2, m, nk).transpose(0, 1, 4, 2, 3)
    return tab.reshape(ncfg, n_h // 2, nk, 2 * m)


def _na_attn_kernel(cfg_ref, ks_ref, q_ref, kp_ref, kc_ref, kn_ref, vp_ref, vc_ref, vn_ref, bias_ref,
                    wc_ref, o_ref, oc_ref, kcat, vt, s_scr, e_scr, m_scr, r_scr,
                    *, seg, n_cast_chunks):
    s_id = pl.program_id(1)
    _cast_rider_step(pl.program_id(0) * pl.num_programs(1) + s_id, n_cast_chunks, [(wc_ref, oc_ref)])
    halo = NA_HALO_ROWS * GRID_W
    m, nk = NA_GROUP_ROWS * GRID_W, NA_KEY_ROWS * GRID_W
    n_groups = seg // m
    kcat[0:halo] = kp_ref[...]
    kcat[halo:halo + seg] = kc_ref[...]
    kcat[halo + seg:] = kn_ref[...]
    vt[:, 0:halo] = vp_ref[...].T
    vt[:, halo:halo + seg] = vc_ref[...].T
    vt[:, halo + seg:] = vn_ref[...].T
    e_scr[:, nk:, :] = jnp.zeros((e_scr.shape[0], NA_VALUE_KEYS - nk, 2 * m), BF16)
    low_half = lax.broadcasted_iota(jnp.int32, (m, LANES), 1) < HEAD_DIM
    top_rows = lax.broadcasted_iota(jnp.int32, (LANES, m), 0) < HEAD_DIM

    def scores(g, slot):
        gg = s_id * n_groups + g
        ks = pl.multiple_of(ks_ref[gg], LANES)
        r0 = _aligned(g * m, m)
        q = q_ref[pl.ds(r0, m), :]
        zero = jnp.zeros_like(q)
        q2 = jnp.concatenate([jnp.where(low_half, q, zero), jnp.where(low_half, zero, q)], axis=0)
        kw = kcat[pl.ds(ks, nk), :]
        s = lax.dot_general(kw, q2, (((1,), (1,)), ((), ())), preferred_element_type=F32) + bias_ref[cfg_ref[gg], 0]
        s_scr[slot] = s
        m_scr[slot] = jnp.max(s, axis=0, keepdims=True)

    def softmax(g, slot):
        e = jnp.exp2(s_scr[slot] - m_scr[slot])
        e_scr[slot, :nk, :] = e.astype(BF16)
        r_scr[slot] = 1.0 / jnp.sum(e, axis=0, keepdims=True)

    def values(g, slot):
        ks = pl.multiple_of(ks_ref[s_id * n_groups + g], LANES)
        r0 = _aligned(g * m, m)
        ot = jnp.dot(vt[:, pl.ds(ks, NA_VALUE_KEYS)], e_scr[slot], preferred_element_type=F32) * r_scr[slot]
        x = jnp.where(top_rows, ot[:, :m], ot[:, m:])
        o_ref[pl.ds(r0, m), :] = x.T.astype(BF16)

    _three_stage_pipeline(n_groups, scores, softmax, values, width=min(NA_WIDTH, n_groups // 3))


def _na_attn_call(proj, bias_cfg, cfg_ids, kstarts, seg, base, cast_src):
    t = proj.shape[0]
    halo = NA_HALO_ROWS * GRID_W
    n_seg = t // seg
    rider = _CastRider(*cast_src, n_inner=n_seg, n_steps=N_HEAD_PAIRS * n_seg)
    per = seg // halo
    n_halo_blocks = t // halo
    ncfg = bias_cfg.shape[0]
    m, nk = NA_GROUP_ROWS * GRID_W, NA_KEY_ROWS * GRID_W
    kb0, vb0 = base + COL_KB // LANES, base + COL_VB // LANES

    def cur(col0):
        return pl.BlockSpec((seg, LANES), lambda p, s, c, k: (s, col0 + p))

    def prev(col0):
        return pl.BlockSpec((halo, LANES), lambda p, s, c, k: (jnp.maximum(s * per - 1, 0), col0 + p))

    def nxt(col0):
        return pl.BlockSpec((halo, LANES), lambda p, s, c, k: (jnp.minimum((s + 1) * per, n_halo_blocks - 1), col0 + p))

    grid_spec = pltpu.PrefetchScalarGridSpec(
        num_scalar_prefetch=2,
        grid=(N_HEAD_PAIRS, n_seg),
        in_specs=[cur(base + COL_QB // LANES), prev(kb0), cur(kb0), nxt(kb0), prev(vb0), cur(vb0), nxt(vb0),
                  pl.BlockSpec((ncfg, 1, nk, 2 * m), lambda p, s, c, k: (0, p, 0, 0)),
                  rider.in_spec],
        out_specs=[pl.BlockSpec((seg, LANES), lambda p, s, c, k: (s, p)), rider.out_spec],
        scratch_shapes=[pltpu.VMEM((seg + 2 * halo, LANES), BF16), pltpu.VMEM((LANES, seg + 2 * halo), BF16),
                        pltpu.VMEM((NA_WIDTH, nk, 2 * m), F32), pltpu.VMEM((NA_WIDTH, NA_VALUE_KEYS, 2 * m), BF16),
                        pltpu.VMEM((NA_WIDTH, 1, 2 * m), F32), pltpu.VMEM((NA_WIDTH, 1, 2 * m), F32)],
    )
    return pl.pallas_call(
        functools.partial(_na_attn_kernel, seg=seg, n_cast_chunks=rider.n_chunks),
        grid_spec=grid_spec,
        out_shape=[jax.ShapeDtypeStruct((t, W_B), BF16), rider.out_shape],
        compiler_params=_cparams(("arbitrary", "arbitrary")),
        name="na_attn",
    )(cfg_ids, kstarts, proj, proj, proj, proj, proj, proj, proj, bias_cfg, rider.operand)


def _outproj_kernel(aa_ref, ab_ref, ga_ref, gb_ref, xp_ref, xs_ref, mod_ref, woa_ref, wob_ref, wout_ref,
                    g2_ref, wrh_ref, wrl_ref, br_ref, x1_ref, h2_ref, lg_ref, *, n_prompt_tiles):
    i = pl.program_id(0)
    ya = jnp.dot(aa_ref[...], woa_ref[...], preferred_element_type=F32)
    yb = jnp.dot(ab_ref[...], wob_ref[...], preferred_element_type=F32)
    merged = ga_ref[...].astype(F32) * ya + gb_ref[...].astype(F32) * yb
    z = jnp.dot(merged.astype(BF16), wout_ref[...], preferred_element_type=F32)
    x = jnp.where(i < n_prompt_tiles, xp_ref[...], xs_ref[...])
    m = mod_ref[0]
    x1 = x + m[2:3] * z
    x1_ref[...] = x1
    y = x1 * lax.rsqrt(jnp.mean(x1 * x1, axis=-1, keepdims=True) + RMS_EPS) * g2_ref[...]
    h2 = y * (1.0 + m[4:5]) + m[3:4]
    h2_hi = h2.astype(BF16)
    h2_ref[...] = h2_hi
    h2_lo = (h2 - h2_hi.astype(F32)).astype(BF16)
    lg = jnp.dot(h2_hi, wrh_ref[...], preferred_element_type=F32)
    lg = lg + jnp.dot(h2_lo, wrh_ref[...], preferred_element_type=F32)
    lg = lg + jnp.dot(h2_hi, wrl_ref[...], preferred_element_type=F32)
    lg_ref[...] = lg + br_ref[...]


def _outproj_call(attn_a, attn_b, proj, xp, xs, mod_seg, woa, wob, wout, g2, wr_hi, wr_lo, br, seg):
    t = attn_a.shape[0]
    d = xp.shape[1]
    tm = min(256, seg)
    npt, nst = xp.shape[0] // tm, xs.shape[0] // tm
    xp_spec, xs_spec = _two_group_specs(tm, d, npt, nst)
    def const(shape):
        return pl.BlockSpec(shape, lambda i: (0,) * len(shape), pipeline_mode=pl.Buffered(1))

    return pl.pallas_call(
        functools.partial(_outproj_kernel, n_prompt_tiles=npt),
        grid=(npt + nst,),
        in_specs=[pl.BlockSpec((tm, Q_A), lambda i: (i, 0)),
                  pl.BlockSpec((tm, W_B), lambda i: (i, 0)),
                  pl.BlockSpec((tm, d), lambda i: (i, 0)),
                  pl.BlockSpec((tm, d), lambda i: (i, 1)),
                  xp_spec, xs_spec,
                  pl.BlockSpec((1, 6, d), lambda i: (i * tm // seg, 0, 0)),
                  const((Q_A, d)), const((W_B, d)), const((d, d)), const((1, d)),
                  const((d, LANES)), const((d, LANES)), const((1, LANES))],
        out_specs=[pl.BlockSpec((tm, d), lambda i: (i, 0)),
                   pl.BlockSpec((tm, d), lambda i: (i, 0)),
                   pl.BlockSpec((tm, LANES), lambda i: (i, 0))],
        out_shape=[jax.ShapeDtypeStruct((t, d), F32),
                   jax.ShapeDtypeStruct((t, d), BF16),
                   jax.ShapeDtypeStruct((t, LANES), F32)],
        compiler_params=_cparams(("arbitrary",)),
        name="out_proj",
    )(attn_a, attn_b, proj, proj, xp, xs, mod_seg, woa, wob, wout, g2.reshape(1, d), wr_hi, wr_lo, br)


def _moe_kernel(te_ref, nv_ref, x_ref, wg_ref, wl_ref, bg_ref, bl_ref, wd_ref, bd_ref, *rest, nf, tile0):
    o_ref, acc_ref = rest[-2:]
    t = pl.program_id(0) + tile0
    f = pl.program_id(1)
    live = t < nv_ref[0]

    @pl.when((pl.program_id(0) == 0) & (f == 0))
    def _():
        acc_ref[...] = jnp.zeros_like(acc_ref)

    @pl.when(live)
    def _():
        x = x_ref[...]
        gate = jnp.dot(x, wg_ref[...], preferred_element_type=F32) + bg_ref[0]
        lin = jnp.dot(x, wl_ref[...], preferred_element_type=F32) + bl_ref[0]
        gate = jnp.minimum(gate, SWIGLU_LIMIT)
        lin = jnp.clip(lin, -SWIGLU_LIMIT, SWIGLU_LIMIT)
        act = gate * _sigmoid(SWIGLU_ALPHA * gate) * (lin + 1.0)
        part = jnp.dot(act.astype(BF16), wd_ref[...], preferred_element_type=F32)
        total = part + jnp.where(f == 0, bd_ref[0], acc_ref[...])
        acc_ref[...] = total
        o_ref[...] = total.astype(BF16)

    @pl.when(jnp.logical_not(live) & (f == 0))
    def _():
        o_ref[...] = jnp.zeros_like(o_ref)


def _moe_call(xin, tile_e, n_valid, w_gate, w_lin, b_gu, w_down, b_down, tm, tile0, n_rows_total, yb_prev):
    n_rows, d = xin.shape
    n_e, two_ff = b_gu.shape
    d_ff = two_ff // 2
    tf = min(512, d_ff)
    nf = d_ff // tf
    n_tiles = n_rows // tm

    def fidx(t, f, nv):
        return jnp.where(t + tile0 < nv[0], f, nf - 1)

    def expert(t, te):
        return te[t + tile0]

    in_specs = [pl.BlockSpec((tm, d), lambda t, f, te, nv: (t, 0)),
                pl.BlockSpec((d, tf), lambda t, f, te, nv: (expert(t, te), fidx(t, f, nv))),
                pl.BlockSpec((d, tf), lambda t, f, te, nv: (expert(t, te), fidx(t, f, nv))),
                pl.BlockSpec((1, 1, tf), lambda t, f, te, nv: (expert(t, te), 0, fidx(t, f, nv))),
                pl.BlockSpec((1, 1, tf), lambda t, f, te, nv: (expert(t, te), 0, fidx(t, f, nv) + nf)),
                pl.BlockSpec((tf, d), lambda t, f, te, nv: (expert(t, te) * nf + fidx(t, f, nv), 0)),
                pl.BlockSpec((1, 1, d), lambda t, f, te, nv: (expert(t, te), 0, 0))]
    operands = [tile_e, n_valid, xin, w_gate, w_lin, b_gu.reshape(n_e, 1, two_ff), b_gu.reshape(n_e, 1, two_ff),
                w_down, b_down.reshape(n_e, 1, d)]
    aliases = {}
    if yb_prev is not None:
        in_specs.append(pl.BlockSpec(memory_space=pl.ANY))
        aliases = {len(operands): 0}
        operands.append(yb_prev)
    grid_spec = pltpu.PrefetchScalarGridSpec(
        num_scalar_prefetch=2,
        grid=(n_tiles, nf),
        in_specs=in_specs,
        out_specs=pl.BlockSpec((tm, d), lambda t, f, te, nv: (t + tile0, 0)),
        scratch_shapes=[pltpu.VMEM((tm, d), F32)],
    )
    return pl.pallas_call(
        functools.partial(_moe_kernel, nf=nf, tile0=tile0),
        grid_spec=grid_spec,
        out_shape=jax.ShapeDtypeStruct((n_rows_total, d), BF16),
        input_output_aliases=aliases,
        compiler_params=_cparams(("arbitrary", "arbitrary")),
        name="moe_experts",
    )(*operands)


def _combine_kernel(x1_ref, ys_ref, w_ref, mod_ref, *rest):
    o_ref = rest[-1]
    w = w_ref[...]
    acc = w[:, 0:1] * ys_ref[0].astype(F32)
    for k in range(1, TOP_K):
        acc = acc + w[:, k:k + 1] * ys_ref[k].astype(F32)
    o_ref[...] = x1_ref[...] + mod_ref[0][5:6] * acc


def _combine_call(x1, ysel, top_w, mod_seg, row0, out_row0, n_out_rows, seg, prev):
    d = x1.shape[1]
    n_rows = ysel.shape[1]
    tm = min(512, seg)
    off, out_off = row0 // tm, out_row0 // tm
    in_specs = [pl.BlockSpec((tm, d), lambda i: (i + off, 0)),
                pl.BlockSpec((TOP_K, tm, d), lambda i: (0, i, 0)),
                pl.BlockSpec((tm, TOP_K), lambda i: (i + off, 0)),
                pl.BlockSpec((1, 6, d), lambda i: ((i + off) * tm // seg, 0, 0))]
    operands = [x1, ysel, top_w, mod_seg]
    aliases = {}
    if prev is not None:
        in_specs.append(pl.BlockSpec(memory_space=pl.ANY))
        aliases = {len(operands): 0}
        operands.append(prev)
    return pl.pallas_call(
        _combine_kernel,
        grid=(n_rows // tm,),
        in_specs=in_specs,
        out_specs=pl.BlockSpec((tm, d), lambda i: (i + out_off, 0)),
        out_shape=jax.ShapeDtypeStruct((n_out_rows, d), F32),
        input_output_aliases=aliases,
        compiler_params=_cparams(("arbitrary",)),
        name="moe_combine",
    )(*operands)


def _pack_w_in(w_in, g_q_a, g_k_a, g_q_b, g_k_b):
    o1 = Q_A
    o2 = o1 + KV_A
    o3 = o2 + KV_A
    o4 = o3 + W_B
    o5 = o4 + W_B
    o6 = o5 + W_B
    d = w_in.shape[0]
    wqa, wka, wva = w_in[:, :o1], w_in[:, o1:o2], w_in[:, o2:o3]
    wqb, wkb, wvb, wg = w_in[:, o3:o4], w_in[:, o4:o5], w_in[:, o5:o6], w_in[:, o6:]

    def dup(w):
        w4 = w.reshape(d, N_KV_A, 1, HEAD_DIM)
        return jnp.broadcast_to(w4, (d, N_KV_A, 2, HEAD_DIM)).reshape(d, N_KV_A * LANES)

    w = jnp.concatenate([wg, wqa, wqb, wkb, dup(wka), dup(wva), wvb], axis=1).astype(BF16)
    q_scale = HEAD_DIM ** -0.5 * LOG2_E
    gvec = jnp.concatenate([jnp.ones((2 * d,), F32),
                            jnp.tile(g_q_a * q_scale, N_HEADS_A), jnp.tile(g_q_b * q_scale, N_HEADS_B),
                            jnp.tile(g_k_b, N_HEADS_B), jnp.tile(g_k_a, 2 * N_KV_A),
                            jnp.ones((COL_END - COL_VA,), F32)]).reshape(1, -1).astype(F32)
    return w, gvec


def kernel(x_prompt, x_sample, c_prompt, c_sample, w_ada, b_ada, g_norm1, w_in, g_q_a, g_k_a, g_q_b, g_k_b, sink_a, t5_bias, na_bias, w_o_a, w_o_b, w_out, g_norm2, w_router, b_router, w_gu, b_gu, w_down, b_down):
    bp, sp, d = x_prompt.shape
    bs, ss, _ = x_sample.shape
    n_e = w_router.shape[-1]
    tp, ts = bp * sp, bs * ss
    t = tp + ts
    seg = math.gcd(math.gcd(sp, ss), 2048)
    assert w_ada.shape[0] == 1 and sp % seg == 0 and ss % seg == 0 and seg % (NA_HALO_ROWS * GRID_W) == 0
    assert 2 * d == w_in.shape[-1] - (Q_A + 2 * KV_A + 3 * W_B)
    base = 2 * d // LANES
    xp = x_prompt.reshape(tp, d)
    xs = x_sample.reshape(ts, d)

    n_c = bp + bs
    c_pad = jnp.zeros((-(-n_c // 16) * 16, d), F32).at[:n_c].set(jnp.concatenate([c_prompt, c_sample], axis=0))
    mod = _ada_call(c_pad, w_ada[0], b_ada[0])[:n_c].reshape(n_c, 6, d)
    seg_owner = np.concatenate([np.repeat(np.arange(bp), sp // seg), bp + np.repeat(np.arange(bs), ss // seg)])
    mod_seg = mod[seg_owner]

    h1 = _norm_mod_call(xp, xs, mod_seg, g_norm1[0], seg)

    w_slab, gvec = _pack_w_in(w_in[0], g_q_a[0], g_k_a[0], g_q_b[0], g_k_b[0])
    grp = np.arange(MXU_DIM) // HEAD_DIM
    ones_bd = jnp.asarray(grp[:, None] == grp[None, :], BF16)
    n_e, d_ff = w_down.shape[1], w_down.shape[2]
    assert d_ff == d and w_gu.shape[1:] == (n_e, d, 2 * d_ff)
    w_gu2d = w_gu[0].reshape(n_e * d, 2 * d_ff)
    w_down2d = w_down[0].reshape(n_e * d_ff, d)
    proj, w_dn = _inproj_call(h1, w_slab, gvec, ones_bd, min(1024, seg), (w_down2d, 0, d))

    qs = min(2048, seg)
    sb_tok = np.arange(t // qs) * qs
    seq_start = np.where(sb_tok < tp, sb_tok // sp * sp, tp + (sb_tok - tp) // ss * ss)
    seq_len = np.where(sb_tok < tp, sp, ss)
    flags = np.stack([sb_tok != seq_start, sb_tok + qs != seq_start + seq_len], axis=1).astype(np.int32).reshape(-1)
    attn_a, w_gate = _win_attn_call(proj, _window_bias(t5_bias), sink_a[0].astype(F32) * LOG2_E, jnp.asarray(flags), qs, base,
                                    (w_gu2d, 0, d_ff))

    cfg_ids, kstarts, cfg_blocks = _na_plan([sp] * bp + [ss] * bs, seg)
    bias_cfg = _na_bias_tables(na_bias[0], cfg_blocks)
    attn_b, w_lin = _na_attn_call(proj, bias_cfg, jnp.asarray(cfg_ids), jnp.asarray(kstarts), seg, base,
                                  (w_gu2d, 1, d_ff))

    wr = jnp.zeros((d, LANES), F32).at[:, :n_e].set(w_router[0])
    wr_hi = wr.astype(BF16)
    wr_lo = (wr - wr_hi.astype(F32)).astype(BF16)
    br = jnp.zeros((1, LANES), F32).at[0, :n_e].set(b_router[0])
    x1, h2, logits = _outproj_call(attn_a, attn_b, proj, xp, xs, mod_seg, w_o_a[0].astype(BF16),
                                   w_o_b[0].astype(BF16), w_out[0].astype(BF16), g_norm2[0], wr_hi, wr_lo, br, seg)

    tm_e = min(1024, seg)
    top_logits, top_idx = lax.top_k(logits[:, :n_e], TOP_K)
    top_w = jax.nn.softmax(top_logits, axis=-1)
    n_assign = t * TOP_K
    e_flat = top_idx.reshape(-1)
    onehot = (e_flat[:, None] == jnp.arange(n_e, dtype=e_flat.dtype)[None, :]).astype(jnp.int32)
    csum = jnp.cumsum(onehot, axis=0)
    rank = jnp.take_along_axis(csum, e_flat[:, None], axis=1)[:, 0] - 1
    counts = csum[-1]
    padded = (counts + tm_e - 1) // tm_e * tm_e
    pend = jnp.cumsum(padded)
    dest = (pend - padded)[e_flat] + rank
    n_tiles = -(-n_assign // tm_e) + n_e
    n_rows = n_tiles * tm_e
    tile_e = jnp.minimum(jnp.searchsorted(pend, jnp.arange(n_tiles) * tm_e, side='right'), n_e - 1).astype(jnp.int32)
    n_valid = (pend[-1:] // tm_e).astype(jnp.int32)
    tok_flat = jnp.arange(n_assign, dtype=jnp.int32) // TOP_K
    _, tok_sorted = lax.sort((e_flat, tok_flat), num_keys=1, is_stable=True)
    tile_src0 = (jnp.cumsum(counts) - counts)[tile_e] + jnp.arange(n_tiles, dtype=jnp.int32) * tm_e - (pend - padded)[tile_e]
    tile_end = jnp.cumsum(counts)[tile_e]
    src = tile_src0[:, None] + jnp.arange(tm_e, dtype=jnp.int32)[None, :]
    slot_tok = jnp.where(src < tile_end[:, None], tok_sorted[jnp.minimum(src, n_assign - 1)],
                         (jnp.arange(n_rows, dtype=jnp.int32) % t).reshape(n_tiles, tm_e))

    n_chunks = math.gcd(n_tiles, MOE_CHUNKS)
    tiles_per_chunk = n_tiles // n_chunks
    yb = None
    for c in range(n_chunks):
        tiles = slice(c * tiles_per_chunk, (c + 1) * tiles_per_chunk)
        xin = h2[slot_tok[tiles].reshape(-1)]
        yb = _moe_call(xin, tile_e, n_valid, w_gate, w_lin, b_gu[0], w_dn, b_down[0], tm_e,
                       c * tiles_per_chunk, n_rows, yb)

    dest_kt = dest.reshape(t, TOP_K).T
    chunk = math.gcd(tp, ts)
    if t // chunk > 8:
        chunk = 0
    outs = []
    for row0, n_group in ((0, tp), (tp, ts)):
        step = chunk if chunk else n_group
        y = None
        for r in range(0, n_group, step):
            ysel = yb[dest_kt[:, row0 + r:row0 + r + step]]
            y = _combine_call(x1, ysel, top_w, mod_seg, row0 + r, r, n_group, seg, y)
        outs.append(y)
    return (outs[0].reshape(bp, sp, d), outs[1].reshape(bs, ss, d))
```

```python
import functools
import math

import numpy as np
import jax
import jax.numpy as jnp
from jax import lax
from jax.experimental import pallas as pl
from jax.experimental.pallas import tpu as pltpu

F32 = jnp.float32
BF16 = jnp.bfloat16

HEAD_DIM = 64
N_HEADS_A = 16
N_KV_A = 4
GROUP_A = N_HEADS_A // N_KV_A
N_HEADS_B = 16
WINDOW = 128
ATTN_BLOCK = 128
T5_BUCKETS = 32
T5_MAX_DIST = 128
GRID_W = 64
NA_ROWS = 8
NA_COLS = 16
TOP_K = 4
SWIGLU_LIMIT = 7.0
SWIGLU_ALPHA = 1.702
RMS_EPS = 1e-6

Q_A = N_HEADS_A * HEAD_DIM
KV_A = N_KV_A * HEAD_DIM
W_B = N_HEADS_B * HEAD_DIM

LANES = 128
MXU_DIM = 256
VMEM_LIMIT_BYTES = 56 * 1024 * 1024
MASKED = -1e30
LOG2_E = math.log2(math.e)

N_HEAD_PAIRS = Q_A // LANES
COL_QA = 0
COL_QB = COL_QA + Q_A
COL_KB = COL_QB + W_B
COL_KA = COL_KB + W_B
COL_VA = COL_KA + N_KV_A * LANES
COL_VB = COL_VA + N_KV_A * LANES
COL_END = COL_VB + W_B
NA_GROUP_ROWS = 2
NA_KEY_ROWS = NA_GROUP_ROWS + NA_ROWS - 1
NA_HALO_ROWS = NA_ROWS


def _sigmoid(x):
    return 1.0 / (1.0 + jnp.exp(-x))


def _cparams(sem):
    return pltpu.CompilerParams(dimension_semantics=sem, vmem_limit_bytes=VMEM_LIMIT_BYTES)


def _ada_kernel(c_ref, w_ref, b_ref, o_ref):
    c = c_ref[...]
    a = (c * _sigmoid(c)).astype(BF16)
    o_ref[...] = jnp.dot(a, w_ref[...].astype(BF16), preferred_element_type=F32) + b_ref[...]


def _ada_call(c_pad, w_ada, b_ada):
    rows, d = c_pad.shape
    n = w_ada.shape[1]
    tn = min(n, 1024)
    return pl.pallas_call(
        _ada_kernel,
        grid=(n // tn,),
        in_specs=[pl.BlockSpec((rows, d), lambda j: (0, 0)),
                  pl.BlockSpec((d, tn), lambda j: (0, j)),
                  pl.BlockSpec((1, tn), lambda j: (0, j))],
        out_specs=pl.BlockSpec((rows, tn), lambda j: (0, j)),
        out_shape=jax.ShapeDtypeStruct((rows, n), F32),
        compiler_params=_cparams(("arbitrary",)),
        name="ada_ln",
    )(c_pad, w_ada, b_ada.reshape(1, n))


def _two_group_specs(tm, d, n_prompt_tiles, n_sample_tiles):
    xp = pl.BlockSpec((tm, d), lambda i, *_: (jnp.minimum(i, n_prompt_tiles - 1), 0))
    xs = pl.BlockSpec((tm, d), lambda i, *_: (jnp.clip(i - n_prompt_tiles, 0, n_sample_tiles - 1), 0))
    return xp, xs


def _norm_mod_kernel(xp_ref, xs_ref, mod_ref, g_ref, o_ref, *, n_prompt_tiles):
    i = pl.program_id(0)
    x = jnp.where(i < n_prompt_tiles, xp_ref[...], xs_ref[...])
    y = x * lax.rsqrt(jnp.mean(x * x, axis=-1, keepdims=True) + RMS_EPS) * g_ref[...]
    m = mod_ref[0]
    o_ref[...] = (y * (1.0 + m[1:2]) + m[0:1]).astype(BF16)


def _norm_mod_call(xp, xs, mod_seg, g, seg):
    d = xp.shape[1]
    tm = min(512, seg)
    npt, nst = xp.shape[0] // tm, xs.shape[0] // tm
    xp_spec, xs_spec = _two_group_specs(tm, d, npt, nst)
    return pl.pallas_call(
        functools.partial(_norm_mod_kernel, n_prompt_tiles=npt),
        grid=(npt + nst,),
        in_specs=[xp_spec, xs_spec,
                  pl.BlockSpec((1, 6, d), lambda i: (i * tm // seg, 0, 0)),
                  pl.BlockSpec((1, d), lambda i: (0, 0))],
        out_specs=pl.BlockSpec((tm, d), lambda i: (i, 0)),
        out_shape=jax.ShapeDtypeStruct((xp.shape[0] + xs.shape[0], d), BF16),
        compiler_params=_cparams(("arbitrary",)),
        name="norm1_mod",
    )(xp, xs, mod_seg, g.reshape(1, d))


def _group_rms(y, g, ones_ref):
    sq = (y * y).astype(BF16)
    parts = []
    for c in range(y.shape[1] // MXU_DIM):
        parts.append(jnp.dot(sq[:, c * MXU_DIM:(c + 1) * MXU_DIM], ones_ref[...], preferred_element_type=F32))
    ss = jnp.concatenate(parts, axis=1) if len(parts) > 1 else parts[0]
    return y * lax.rsqrt(ss * (1.0 / HEAD_DIM) + RMS_EPS) * g


INPROJ_ROW_CHUNKS = 2
INPROJ_CAST_ROWS = 256


def _inproj_kernel(h_ref, w_ref, g_ref, ones_ref, wc_ref, o_ref, oc_ref, *, tn, gate_tiles, n_cast_chunks):
    j = pl.program_id(1)
    _cast_rider_step(pl.program_id(0) * pl.num_programs(1) + j, n_cast_chunks, [(wc_ref, oc_ref)])
    half = tn // 2
    mixed_tile = gate_tiles + COL_KA // tn
    plain_tile = gate_tiles + COL_VB // tn
    rows = h_ref.shape[0] // INPROJ_ROW_CHUNKS

    def chunked(epilogue):
        for c in range(INPROJ_ROW_CHUNKS):
            r = slice(c * rows, (c + 1) * rows)
            epilogue(jnp.dot(h_ref[r, :], w_ref[...], preferred_element_type=F32), r)

    @pl.when(j < gate_tiles)
    def _():
        def epilogue(y, r):
            o_ref[r, :] = (0.5 * jnp.tanh(0.5 * y) + 0.5).astype(BF16)
        chunked(epilogue)

    @pl.when((j >= gate_tiles) & (j < mixed_tile))
    def _():
        def epilogue(y, r):
            o_ref[r, :] = _group_rms(y, g_ref[...], ones_ref).astype(BF16)
        chunked(epilogue)

    @pl.when(j == mixed_tile)
    def _():
        def epilogue(y, r):
            o_ref[r, :half] = _group_rms(y[:, :half], g_ref[:, :half], ones_ref).astype(BF16)
            o_ref[r, half:] = y[:, half:].astype(BF16)
        chunked(epilogue)

    @pl.when(j == plain_tile)
    def _():
        def epilogue(y, r):
            o_ref[r, :] = y.astype(BF16)
        chunked(epilogue)


def _inproj_call(h, w, gvec, ones_bd, tm, cast_src):
    t, d = h.shape
    pw = w.shape[1]
    tn = 1024
    assert (2 * d) % tn == 0 and COL_KA % tn == 0 and COL_VA - COL_KA == tn // 2 and COL_VB % tn == 0
    grid = (t // tm, pw // tn)
    rider = _CastRider(*cast_src, n_inner=grid[1], n_steps=grid[0] * grid[1], tr=INPROJ_CAST_ROWS)
    return pl.pallas_call(
        functools.partial(_inproj_kernel, tn=tn, gate_tiles=2 * d // tn, n_cast_chunks=rider.n_chunks),
        grid=grid,
        in_specs=[pl.BlockSpec((tm, d), lambda i, j: (i, 0)),
                  pl.BlockSpec((d, tn), lambda i, j: (0, j)),
                  pl.BlockSpec((1, tn), lambda i, j: (0, j)),
                  pl.BlockSpec((MXU_DIM, MXU_DIM), lambda i, j: (0, 0)),
                  rider.in_spec],
        out_specs=[pl.BlockSpec((tm, tn), lambda i, j: (i, j)), rider.out_spec],
        out_shape=[jax.ShapeDtypeStruct((t, pw), BF16), rider.out_shape],
        compiler_params=_cparams(("arbitrary", "arbitrary")),
        name="in_proj",
    )(h, w, gvec, ones_bd, rider.operand)


def _aligned(x, m):
    return x if isinstance(x, int) else pl.multiple_of(x, m)


def _three_stage_pipeline(n, stage_a, stage_b, stage_c, width=2):
    assert n % width == 0 and n // width >= 3
    trips = n // width

    def trip(t, run_a=True, run_b=True, run_c=True):
        for k in range(width):
            if run_c:
                stage_c(width * (t - 2) + k, k)
        for k in range(width):
            if run_b:
                stage_b(width * (t - 1) + k, k)
        for k in range(width):
            if run_a:
                stage_a(width * t + k, k)

    trip(0, run_b=False, run_c=False)
    trip(1, run_c=False)

    def body(t, carry):
        trip(t)
        return carry

    lax.fori_loop(2, trips, body, 0)
    trip(trips, run_a=False)
    trip(trips + 1, run_a=False, run_b=False)


WIN_WIDTH = 4
NA_WIDTH = 4
NA_VALUE_KEYS = -(-NA_KEY_ROWS * GRID_W // LANES) * LANES
CAST_ROWS = 512
MOE_CHUNKS = 4


class _CastRider:
    def __init__(self, w2d, col_block, n_cols, n_inner, n_steps, tr=CAST_ROWS):
        rows = w2d.shape[0]
        tr = min(tr, rows)
        self.n_chunks = rows // tr
        assert rows % tr == 0 and self.n_chunks <= n_steps
        self.operand = w2d

        def chunk(p, s, *_):
            return jnp.minimum(p * n_inner + s, self.n_chunks - 1)

        self.in_spec = pl.BlockSpec((tr, n_cols), lambda *g: (chunk(*g), col_block))
        self.out_spec = pl.BlockSpec((tr, n_cols), lambda *g: (chunk(*g), 0))
        self.out_shape = jax.ShapeDtypeStruct((rows, n_cols), BF16)


def _cast_rider_step(step, n_chunks, pairs):
    @pl.when(step < n_chunks)
    def _():
        for src, dst in pairs:
            dst[...] = src[...].astype(BF16)


def _win_attn_kernel(flags_ref, q_ref, kp_ref, kc_ref, kn_ref, vp_ref, vc_ref, vn_ref, bias_ref, sink_ref,
                     wc_ref, o_ref, oc_ref, kcat, vt, s_scr, e_scr, m_scr, r_scr, *, qs, nb, n_cast_chunks):
    p = pl.program_id(0)
    sb = pl.program_id(1)
    _cast_rider_step(p * pl.num_programs(1) + sb, n_cast_chunks, [(wc_ref, oc_ref)])
    blk = ATTN_BLOCK
    kcat[0:blk] = kp_ref[...]
    kcat[blk:blk + qs] = kc_ref[...]
    kcat[blk + qs:] = kn_ref[...]
    vt[:, 0:blk] = vp_ref[...].T[:HEAD_DIM]
    vt[:, blk:blk + qs] = vc_ref[...].T[:HEAD_DIM]
    vt[:, blk + qs:] = vn_ref[...].T[:HEAD_DIM]
    prev_ok = flags_ref[2 * sb]
    next_ok = flags_ref[2 * sb + 1]
    low_half = lax.broadcasted_iota(jnp.int32, (blk, LANES), 1) < HEAD_DIM
    lane2 = lax.broadcasted_iota(jnp.int32, (1, 2 * blk), 1)
    sinkrow = jnp.where(lane2 < blk, sink_ref[2 * p], sink_ref[2 * p + 1])
    kl = blk + 2 * WINDOW

    def scores(b, slot):
        r0 = _aligned(b * blk, blk)
        q = q_ref[pl.ds(r0, blk), :]
        zero = jnp.zeros_like(q)
        q2 = jnp.concatenate([jnp.where(low_half, q, zero), jnp.where(low_half, zero, q)], axis=0)
        kw = kcat[pl.ds(r0, kl), :]
        variant = jnp.where((b == 0) & (prev_ok == 0), 1, jnp.where((b == nb - 1) & (next_ok == 0), 2, 0))
        s = lax.dot_general(kw, q2, (((1,), (1,)), ((), ())), preferred_element_type=F32) + bias_ref[variant, 0]
        s_scr[slot] = s
        m_scr[slot] = jnp.maximum(jnp.max(s, axis=0, keepdims=True), sinkrow)

    def softmax(b, slot):
        m = m_scr[slot]
        e = jnp.exp2(s_scr[slot] - m)
        e_scr[slot] = e.astype(BF16)
        r_scr[slot] = 1.0 / (jnp.sum(e, axis=0, keepdims=True) + jnp.exp2(sinkrow - m))

    def values(b, slot):
        r0 = _aligned(b * blk, blk)
        ot = jnp.dot(vt[:, pl.ds(r0, kl)], e_scr[slot], preferred_element_type=F32) * r_scr[slot]
        x = jnp.concatenate([ot[:, :blk], ot[:, blk:]], axis=0)
        o_ref[pl.ds(r0, blk), :] = x.T.astype(BF16)

    _three_stage_pipeline(nb, scores, softmax, values, width=min(WIN_WIDTH, nb // 3))


def _win_attn_call(proj, bias3, sink, flags, qs, base, cast_src):
    t = proj.shape[0]
    blk = ATTN_BLOCK
    nb = qs // blk
    n_super = t // qs
    rider = _CastRider(*cast_src, n_inner=n_super, n_steps=N_HEAD_PAIRS * n_super)
    n_blk_rows = t // blk
    kl = blk + 2 * WINDOW
    ka0, va0 = base + COL_KA // LANES, base + COL_VA // LANES

    def cur(col0):
        return pl.BlockSpec((qs, LANES), lambda p, s, f: (s, col0 + p // 2))

    def prev(col0):
        return pl.BlockSpec((blk, LANES), lambda p, s, f: (jnp.maximum(s * nb - 1, 0), col0 + p // 2))

    def nxt(col0):
        return pl.BlockSpec((blk, LANES), lambda p, s, f: (jnp.minimum((s + 1) * nb, n_blk_rows - 1), col0 + p // 2))

    grid_spec = pltpu.PrefetchScalarGridSpec(
        num_scalar_prefetch=1,
        grid=(N_HEAD_PAIRS, n_super),
        in_specs=[pl.BlockSpec((qs, LANES), lambda p, s, f: (s, base + COL_QA // LANES + p)),
                  prev(ka0), cur(ka0), nxt(ka0), prev(va0), cur(va0), nxt(va0),
                  pl.BlockSpec((3, 1, kl, 2 * blk), lambda p, s, f: (0, p, 0, 0)),
                  pl.BlockSpec(memory_space=pltpu.SMEM),
                  rider.in_spec],
        out_specs=[pl.BlockSpec((qs, LANES), lambda p, s, f: (s, p)), rider.out_spec],
        scratch_shapes=[pltpu.VMEM((qs + 2 * blk, LANES), BF16), pltpu.VMEM((HEAD_DIM, qs + 2 * blk), BF16),
                        pltpu.VMEM((WIN_WIDTH, kl, 2 * blk), F32), pltpu.VMEM((WIN_WIDTH, kl, 2 * blk), BF16),
                        pltpu.VMEM((WIN_WIDTH, 1, 2 * blk), F32), pltpu.VMEM((WIN_WIDTH, 1, 2 * blk), F32)],
    )
    return pl.pallas_call(
        functools.partial(_win_attn_kernel, qs=qs, nb=nb, n_cast_chunks=rider.n_chunks),
        grid_spec=grid_spec,
        out_shape=[jax.ShapeDtypeStruct((t, Q_A), BF16), rider.out_shape],
        compiler_params=_cparams(("arbitrary", "arbitrary")),
        name="win_attn",
    )(flags, proj, proj, proj, proj, proj, proj, proj, bias3, sink, rider.operand)


def _t5_bucket(rel):
    nb = T5_BUCKETS // 2
    max_exact = nb // 2
    ret = jnp.where(rel > 0, nb, 0)
    n = jnp.abs(rel)
    nf = jnp.maximum(n, 1).astype(jnp.float32)
    large = max_exact + (jnp.log(nf / max_exact) / math.log(T5_MAX_DIST / max_exact) * (nb - max_exact)).astype(jnp.int32)
    large = jnp.minimum(large, nb - 1)
    return ret + jnp.where(n < max_exact, n, large)


def _toeplitz(v, n_rows, n_cols):
    p = v.shape[-1]
    assert n_cols <= p - 1
    flat = jnp.tile(v, (1,) * (v.ndim - 1) + (n_rows,))[..., :n_rows * (p - 1)]
    return flat.reshape(v.shape[:-1] + (n_rows, p - 1))[..., :n_cols]


def _window_bias(t5_bias):
    blk, kl = ATTN_BLOCK, ATTN_BLOCK + 2 * WINDOW
    period = blk + kl - 1
    m = np.arange(period)
    rel_of_m = np.where(m < kl, m, m - period) - WINDOW
    per_rel = t5_bias[_t5_bucket(jnp.asarray(rel_of_m))].astype(F32).T
    bias = _toeplitz(per_rel * LOG2_E, blk, kl)
    col = np.arange(kl)[None, :]
    band = np.abs(col - WINDOW - np.arange(blk)[:, None]) <= WINDOW
    keep = np.stack([band, band & (col >= WINDOW), band & (col < WINDOW + blk)])
    tab = jnp.where(keep[:, None], bias[None], MASKED)
    tab = tab.reshape(3, N_HEADS_A // 2, 2, blk, kl).transpose(0, 1, 4, 2, 3)
    return tab.reshape(3, N_HEADS_A // 2, kl, 2 * blk)


def _na_plan(seq_lens, seg):
    qr, nkr = NA_GROUP_ROWS, NA_KEY_ROWS
    configs, cfg_ids, kstarts = {}, [], []
    tok = 0
    for s_len in seq_lens:
        rows = s_len // GRID_W
        assert rows >= nkr and rows % qr == 0
        for r in range(0, rows, qr):
            us = min(max(r - NA_ROWS // 2, 0), rows - NA_ROWS)
            rel = tuple(min(max(r + q - NA_ROWS // 2, 0), rows - NA_ROWS) - us for q in range(qr))
            key = (r - us, rel)
            cfg_ids.append(configs.setdefault(key, len(configs)))
            g_tok = tok + r * GRID_W
            seg_tok0 = (g_tok // seg) * seg
            kstarts.append(tok + us * GRID_W - seg_tok0 + NA_HALO_ROWS * GRID_W)
        tok += s_len
    blocks = [None] * len(configs)
    for (r_us, rel), c in configs.items():
        blocks[c] = [[kr - r_us - q + NA_ROWS - 1 if rel[q] <= kr < rel[q] + NA_ROWS else None
                      for kr in range(nkr)] for q in range(qr)]
    return np.asarray(cfg_ids, np.int32), np.asarray(kstarts, np.int32), blocks


def _na_bias_tables(na_bias, blocks):
    period = 2 * GRID_W - 1
    m = np.arange(period)
    dcol = np.where(m < GRID_W, m, m - period)
    per_dcol = na_bias.astype(F32)[:, :, np.clip(dcol + NA_COLS - 1, 0, 2 * NA_COLS - 2)]
    col_tab = _toeplitz(per_dcol * LOG2_E, GRID_W, GRID_W)
    q_col, k_col = np.arange(GRID_W)[:, None], np.arange(GRID_W)[None, :]
    win_start = np.clip(q_col - NA_COLS // 2, 0, GRID_W - NA_COLS)
    col_ok = (k_col >= win_start) & (k_col < win_start + NA_COLS)
    col_tab = jnp.where(col_ok, col_tab, MASKED)
    masked_blk = jnp.full(col_tab.shape[:1] + col_tab.shape[2:], MASKED, F32)
    cfgs = []
    for cfg in blocks:
        rows = [jnp.concatenate([masked_blk if dr is None else col_tab[:, dr] for dr in qrow], axis=-1) for qrow in cfg]
        cfgs.append(jnp.concatenate(rows, axis=-2))
    tab = jnp.stack(cfgs)
    ncfg, n_h, m, nk = tab.shape
    tab = tab.reshape(ncfg, n_h // 2, 2, m, nk).transpose(0, 1, 4, 2, 3)
    return tab.reshape(ncfg, n_h // 2, nk, 2 * m)


def _na_attn_kernel(cfg_ref, ks_ref, q_ref, kp_ref, kc_ref, kn_ref, vp_ref, vc_ref, vn_ref, bias_ref,
                    wc_ref, o_ref, oc_ref, kcat, vt, s_scr, e_scr, m_scr, r_scr,
                    *, seg, n_cast_chunks):
    s_id = pl.program_id(1)
    _cast_rider_step(pl.program_id(0) * pl.num_programs(1) + s_id, n_cast_chunks, [(wc_ref, oc_ref)])
    halo = NA_HALO_ROWS * GRID_W
    m, nk = NA_GROUP_ROWS * GRID_W, NA_KEY_ROWS * GRID_W
    n_groups = seg // m
    kcat[0:halo] = kp_ref[...]
    kcat[halo:halo + seg] = kc_ref[...]
    kcat[halo + seg:] = kn_ref[...]
    vt[:, 0:halo] = vp_ref[...].T
    vt[:, halo:halo + seg] = vc_ref[...].T
    vt[:, halo + seg:] = vn_ref[...].T
    e_scr[:, nk:, :] = jnp.zeros((e_scr.shape[0], NA_VALUE_KEYS - nk, 2 * m), BF16)
    low_half = lax.broadcasted_iota(jnp.int32, (m, LANES), 1) < HEAD_DIM
    top_rows = lax.broadcasted_iota(jnp.int32, (LANES, m), 0) < HEAD_DIM

    def scores(g, slot):
        gg = s_id * n_groups + g
        ks = pl.multiple_of(ks_ref[gg], LANES)
        r0 = _aligned(g * m, m)
        q = q_ref[pl.ds(r0, m), :]
        zero = jnp.zeros_like(q)
        q2 = jnp.concatenate([jnp.where(low_half, q, zero), jnp.where(low_half, zero, q)], axis=0)
        kw = kcat[pl.ds(ks, nk), :]
        s = lax.dot_general(kw, q2, (((1,), (1,)), ((), ())), preferred_element_type=F32) + bias_ref[cfg_ref[gg], 0]
        s_scr[slot] = s
        m_scr[slot] = jnp.max(s, axis=0, keepdims=True)

    def softmax(g, slot):
        e = jnp.exp2(s_scr[slot] - m_scr[slot])
        e_scr[slot, :nk, :] = e.astype(BF16)
        r_scr[slot] = 1.0 / jnp.sum(e, axis=0, keepdims=True)

    def values(g, slot):
        ks = pl.multiple_of(ks_ref[s_id * n_groups + g], LANES)
        r0 = _aligned(g * m, m)
        ot = jnp.dot(vt[:, pl.ds(ks, NA_VALUE_KEYS)], e_scr[slot], preferred_element_type=F32) * r_scr[slot]
        x = jnp.where(top_rows, ot[:, :m], ot[:, m:])
        o_ref[pl.ds(r0, m), :] = x.T.astype(BF16)

    _three_stage_pipeline(n_groups, scores, softmax, values, width=min(NA_WIDTH, n_groups // 3))


def _na_attn_call(proj, bias_cfg, cfg_ids, kstarts, seg, base, cast_src):
    t = proj.shape[0]
    halo = NA_HALO_ROWS * GRID_W
    n_seg = t // seg
    rider = _CastRider(*cast_src, n_inner=n_seg, n_steps=N_HEAD_PAIRS * n_seg)
    per = seg // halo
    n_halo_blocks = t // halo
    ncfg = bias_cfg.shape[0]
    m, nk = NA_GROUP_ROWS * GRID_W, NA_KEY_ROWS * GRID_W
    kb0, vb0 = base + COL_KB // LANES, base + COL_VB // LANES

    def cur(col0):
        return pl.BlockSpec((seg, LANES), lambda p, s, c, k: (s, col0 + p))

    def prev(col0):
        return pl.BlockSpec((halo, LANES), lambda p, s, c, k: (jnp.maximum(s * per - 1, 0), col0 + p))

    def nxt(col0):
        return pl.BlockSpec((halo, LANES), lambda p, s, c, k: (jnp.minimum((s + 1) * per, n_halo_blocks - 1), col0 + p))

    grid_spec = pltpu.PrefetchScalarGridSpec(
        num_scalar_prefetch=2,
        grid=(N_HEAD_PAIRS, n_seg),
        in_specs=[cur(base + COL_QB // LANES), prev(kb0), cur(kb0), nxt(kb0), prev(vb0), cur(vb0), nxt(vb0),
                  pl.BlockSpec((ncfg, 1, nk, 2 * m), lambda p, s, c, k: (0, p, 0, 0)),
                  rider.in_spec],
        out_specs=[pl.BlockSpec((seg, LANES), lambda p, s, c, k: (s, p)), rider.out_spec],
        scratch_shapes=[pltpu.VMEM((seg + 2 * halo, LANES), BF16), pltpu.VMEM((LANES, seg + 2 * halo), BF16),
                        pltpu.VMEM((NA_WIDTH, nk, 2 * m), F32), pltpu.VMEM((NA_WIDTH, NA_VALUE_KEYS, 2 * m), BF16),
                        pltpu.VMEM((NA_WIDTH, 1, 2 * m), F32), pltpu.VMEM((NA_WIDTH, 1, 2 * m), F32)],
    )
    return pl.pallas_call(
        functools.partial(_na_attn_kernel, seg=seg, n_cast_chunks=rider.n_chunks),
        grid_spec=grid_spec,
        out_shape=[jax.ShapeDtypeStruct((t, W_B), BF16), rider.out_shape],
        compiler_params=_cparams(("arbitrary", "arbitrary")),
        name="na_attn",
    )(cfg_ids, kstarts, proj, proj, proj, proj, proj, proj, proj, bias_cfg, rider.operand)


def _outproj_kernel(aa_ref, ab_ref, ga_ref, gb_ref, xp_ref, xs_ref, mod_ref, woa_ref, wob_ref, wout_ref,
                    g2_ref, wrh_ref, wrl_ref, br_ref, x1_ref, h2_ref, lg_ref, *, n_prompt_tiles):
    i = pl.program_id(0)
    ya = jnp.dot(aa_ref[...], woa_ref[...], preferred_element_type=F32)
    yb = jnp.dot(ab_ref[...], wob_ref[...], preferred_element_type=F32)
    merged = ga_ref[...].astype(F32) * ya + gb_ref[...].astype(F32) * yb
    z = jnp.dot(merged.astype(BF16), wout_ref[...], preferred_element_type=F32)
    x = jnp.where(i < n_prompt_tiles, xp_ref[...], xs_ref[...])
    m = mod_ref[0]
    x1 = x + m[2:3] * z
    x1_ref[...] = x1
    y = x1 * lax.rsqrt(jnp.mean(x1 * x1, axis=-1, keepdims=True) + RMS_EPS) * g2_ref[...]
    h2 = y * (1.0 + m[4:5]) + m[3:4]
    h2_hi = h2.astype(BF16)
    h2_ref[...] = h2_hi
    h2_lo = (h2 - h2_hi.astype(F32)).astype(BF16)
    lg = jnp.dot(h2_hi, wrh_ref[...], preferred_element_type=F32)
    lg = lg + jnp.dot(h2_lo, wrh_ref[...], preferred_element_type=F32)
    lg = lg + jnp.dot(h2_hi, wrl_ref[...], preferred_element_type=F32)
    lg_ref[...] = lg + br_ref[...]


def _outproj_call(attn_a, attn_b, proj, xp, xs, mod_seg, woa, wob, wout, g2, wr_hi, wr_lo, br, seg):
    t = attn_a.shape[0]
    d = xp.shape[1]
    tm = min(256, seg)
    npt, nst = xp.shape[0] // tm, xs.shape[0] // tm
    xp_spec, xs_spec = _two_group_specs(tm, d, npt, nst)
    def const(shape):
        return pl.BlockSpec(shape, lambda i: (0,) * len(shape), pipeline_mode=pl.Buffered(1))

    return pl.pallas_call(
        functools.partial(_outproj_kernel, n_prompt_tiles=npt),
        grid=(npt + nst,),
        in_specs=[pl.BlockSpec((tm, Q_A), lambda i: (i, 0)),
                  pl.BlockSpec((tm, W_B), lambda i: (i, 0)),
                  pl.BlockSpec((tm, d), lambda i: (i, 0)),
                  pl.BlockSpec((tm, d), lambda i: (i, 1)),
                  xp_spec, xs_spec,
                  pl.BlockSpec((1, 6, d), lambda i: (i * tm // seg, 0, 0)),
                  const((Q_A, d)), const((W_B, d)), const((d, d)), const((1, d)),
                  const((d, LANES)), const((d, LANES)), const((1, LANES))],
        out_specs=[pl.BlockSpec((tm, d), lambda i: (i, 0)),
                   pl.BlockSpec((tm, d), lambda i: (i, 0)),
                   pl.BlockSpec((tm, LANES), lambda i: (i, 0))],
        out_shape=[jax.ShapeDtypeStruct((t, d), F32),
                   jax.ShapeDtypeStruct((t, d), BF16),
                   jax.ShapeDtypeStruct((t, LANES), F32)],
        compiler_params=_cparams(("arbitrary",)),
        name="out_proj",
    )(attn_a, attn_b, proj, proj, xp, xs, mod_seg, woa, wob, wout, g2.reshape(1, d), wr_hi, wr_lo, br)


def _moe_kernel(te_ref, nv_ref, x_ref, wg_ref, wl_ref, bg_ref, bl_ref, wd_ref, bd_ref, *rest, nf, tile0):
    o_ref, acc_ref = rest[-2:]
    t = pl.program_id(0) + tile0
    f = pl.program_id(1)
    live = t < nv_ref[0]

    @pl.when((pl.program_id(0) == 0) & (f == 0))
    def _():
        acc_ref[...] = jnp.zeros_like(acc_ref)

    @pl.when(live)
    def _():
        x = x_ref[...]
        gate = jnp.dot(x, wg_ref[...], preferred_element_type=F32) + bg_ref[0]
        lin = jnp.dot(x, wl_ref[...], preferred_element_type=F32) + bl_ref[0]
        gate = jnp.minimum(gate, SWIGLU_LIMIT)
        lin = jnp.clip(lin, -SWIGLU_LIMIT, SWIGLU_LIMIT)
        act = gate * _sigmoid(SWIGLU_ALPHA * gate) * (lin + 1.0)
        part = jnp.dot(act.astype(BF16), wd_ref[...], preferred_element_type=F32)
        total = part + jnp.where(f == 0, bd_ref[0], acc_ref[...])
        acc_ref[...] = total
        o_ref[...] = total.astype(BF16)

    @pl.when(jnp.logical_not(live) & (f == 0))
    def _():
        o_ref[...] = jnp.zeros_like(o_ref)


def _moe_call(xin, tile_e, n_valid, w_gate, w_lin, b_gu, w_down, b_down, tm, tile0, n_rows_total, yb_prev):
    n_rows, d = xin.shape
    n_e, two_ff = b_gu.shape
    d_ff = two_ff // 2
    tf = min(512, d_ff)
    nf = d_ff // tf
    n_tiles = n_rows // tm

    def fidx(t, f, nv):
        return jnp.where(t + tile0 < nv[0], f, nf - 1)

    def expert(t, te):
        return te[t + tile0]

    in_specs = [pl.BlockSpec((tm, d), lambda t, f, te, nv: (t, 0)),
                pl.BlockSpec((d, tf), lambda t, f, te, nv: (expert(t, te), fidx(t, f, nv))),
                pl.BlockSpec((d, tf), lambda t, f, te, nv: (expert(t, te), fidx(t, f, nv))),
                pl.BlockSpec((1, 1, tf), lambda t, f, te, nv: (expert(t, te), 0, fidx(t, f, nv))),
                pl.BlockSpec((1, 1, tf), lambda t, f, te, nv: (expert(t, te), 0, fidx(t, f, nv) + nf)),
                pl.BlockSpec((tf, d), lambda t, f, te, nv: (expert(t, te) * nf + fidx(t, f, nv), 0)),
                pl.BlockSpec((1, 1, d), lambda t, f, te, nv: (expert(t, te), 0, 0))]
    operands = [tile_e, n_valid, xin, w_gate, w_lin, b_gu.reshape(n_e, 1, two_ff), b_gu.reshape(n_e, 1, two_ff),
                w_down, b_down.reshape(n_e, 1, d)]
    aliases = {}
    if yb_prev is not None:
        in_specs.append(pl.BlockSpec(memory_space=pl.ANY))
        aliases = {len(operands): 0}
        operands.append(yb_prev)
    grid_spec = pltpu.PrefetchScalarGridSpec(
        num_scalar_prefetch=2,
        grid=(n_tiles, nf),
        in_specs=in_specs,
        out_specs=pl.BlockSpec((tm, d), lambda t, f, te, nv: (t + tile0, 0)),
        scratch_shapes=[pltpu.VMEM((tm, d), F32)],
    )
    return pl.pallas_call(
        functools.partial(_moe_kernel, nf=nf, tile0=tile0),
        grid_spec=grid_spec,
        out_shape=jax.ShapeDtypeStruct((n_rows_total, d), BF16),
        input_output_aliases=aliases,
        compiler_params=_cparams(("arbitrary", "arbitrary")),
        name="moe_experts",
    )(*operands)


def _combine_kernel(x1_ref, ys_ref, w_ref, mod_ref, *rest):
    o_ref = rest[-1]
    w = w_ref[...]
    acc = w[:, 0:1] * ys_ref[0].astype(F32)
    for k in range(1, TOP_K):
        acc = acc + w[:, k:k + 1] * ys_ref[k].astype(F32)
    o_ref[...] = x1_ref[...] + mod_ref[0][5:6] * acc


def _combine_call(x1, ysel, top_w, mod_seg, row0, out_row0, n_out_rows, seg, prev):
    d = x1.shape[1]
    n_rows = ysel.shape[1]
    tm = min(512, seg)
    off, out_off = row0 // tm, out_row0 // tm
    in_specs = [pl.BlockSpec((tm, d), lambda i: (i + off, 0)),
                pl.BlockSpec((TOP_K, tm, d), lambda i: (0, i, 0)),
                pl.BlockSpec((tm, TOP_K), lambda i: (i + off, 0)),
                pl.BlockSpec((1, 6, d), lambda i: ((i + off) * tm // seg, 0, 0))]
    operands = [x1, ysel, top_w, mod_seg]
    aliases = {}
    if prev is not None:
        in_specs.append(pl.BlockSpec(memory_space=pl.ANY))
        aliases = {len(operands): 0}
        operands.append(prev)
    return pl.pallas_call(
        _combine_kernel,
        grid=(n_rows // tm,),
        in_specs=in_specs,
        out_specs=pl.BlockSpec((tm, d), lambda i: (i + out_off, 0)),
        out_shape=jax.ShapeDtypeStruct((n_out_rows, d), F32),
        input_output_aliases=aliases,
        compiler_params=_cparams(("arbitrary",)),
        name="moe_combine",
    )(*operands)


def _pack_w_in(w_in, g_q_a, g_k_a, g_q_b, g_k_b):
    o1 = Q_A
    o2 = o1 + KV_A
    o3 = o2 + KV_A
    o4 = o3 + W_B
    o5 = o4 + W_B
    o6 = o5 + W_B
    d = w_in.shape[0]
    wqa, wka, wva = w_in[:, :o1], w_in[:, o1:o2], w_in[:, o2:o3]
    wqb, wkb, wvb, wg = w_in[:, o3:o4], w_in[:, o4:o5], w_in[:, o5:o6], w_in[:, o6:]

    def dup(w):
        w4 = w.reshape(d, N_KV_A, 1, HEAD_DIM)
        return jnp.broadcast_to(w4, (d, N_KV_A, 2, HEAD_DIM)).reshape(d, N_KV_A * LANES)

    w = jnp.concatenate([wg, wqa, wqb, wkb, dup(wka), dup(wva), wvb], axis=1).astype(BF16)
    q_scale = HEAD_DIM ** -0.5 * LOG2_E
    gvec = jnp.concatenate([jnp.ones((2 * d,), F32),
                            jnp.tile(g_q_a * q_scale, N_HEADS_A), jnp.tile(g_q_b * q_scale, N_HEADS_B),
                            jnp.tile(g_k_b, N_HEADS_B), jnp.tile(g_k_a, 2 * N_KV_A),
                            jnp.ones((COL_END - COL_VA,), F32)]).reshape(1, -1).astype(F32)
    return w, gvec


def kernel(x_prompt, x_sample, c_prompt, c_sample, w_ada, b_ada, g_norm1, w_in, g_q_a, g_k_a, g_q_b, g_k_b, sink_a, t5_bias, na_bias, w_o_a, w_o_b, w_out, g_norm2, w_router, b_router, w_gu, b_gu, w_down, b_down):
    bp, sp, d = x_prompt.shape
    bs, ss, _ = x_sample.shape
    n_e = w_router.shape[-1]
    tp, ts = bp * sp, bs * ss
    t = tp + ts
    seg = math.gcd(math.gcd(sp, ss), 2048)
    assert w_ada.shape[0] == 1 and sp % seg == 0 and ss % seg == 0 and seg % (NA_HALO_ROWS * GRID_W) == 0
    assert 2 * d == w_in.shape[-1] - (Q_A + 2 * KV_A + 3 * W_B)
    base = 2 * d // LANES
    xp = x_prompt.reshape(tp, d)
    xs = x_sample.reshape(ts, d)

    n_c = bp + bs
    c_pad = jnp.zeros((-(-n_c // 16) * 16, d), F32).at[:n_c].set(jnp.concatenate([c_prompt, c_sample], axis=0))
    mod = _ada_call(c_pad, w_ada[0], b_ada[0])[:n_c].reshape(n_c, 6, d)
    seg_owner = np.concatenate([np.repeat(np.arange(bp), sp // seg), bp + np.repeat(np.arange(bs), ss // seg)])
    mod_seg = mod[seg_owner]

    h1 = _norm_mod_call(xp, xs, mod_seg, g_norm1[0], seg)

    w_slab, gvec = _pack_w_in(w_in[0], g_q_a[0], g_k_a[0], g_q_b[0], g_k_b[0])
    grp = np.arange(MXU_DIM) // HEAD_DIM
    ones_bd = jnp.asarray(grp[:, None] == grp[None, :], BF16)
    n_e, d_ff = w_down.shape[1], w_down.shape[2]
    assert d_ff == d and w_gu.shape[1:] == (n_e, d, 2 * d_ff)
    w_gu2d = w_gu[0].reshape(n_e * d, 2 * d_ff)
    w_down2d = w_down[0].reshape(n_e * d_ff, d)
    proj, w_dn = _inproj_call(h1, w_slab, gvec, ones_bd, min(1024, seg), (w_down2d, 0, d))

    qs = min(2048, seg)
    sb_tok = np.arange(t // qs) * qs
    seq_start = np.where(sb_tok < tp, sb_tok // sp * sp, tp + (sb_tok - tp) // ss * ss)
    seq_len = np.where(sb_tok < tp, sp, ss)
    flags = np.stack([sb_tok != seq_start, sb_tok + qs != seq_start + seq_len], axis=1).astype(np.int32).reshape(-1)
    attn_a, w_gate = _win_attn_call(proj, _window_bias(t5_bias), sink_a[0].astype(F32) * LOG2_E, jnp.asarray(flags), qs, base,
                                    (w_gu2d, 0, d_ff))

    cfg_ids, kstarts, cfg_blocks = _na_plan([sp] * bp + [ss] * bs, seg)
    bias_cfg = _na_bias_tables(na_bias[0], cfg_blocks)
    attn_b, w_lin = _na_attn_call(proj, bias_cfg, jnp.asarray(cfg_ids), jnp.asarray(kstarts), seg, base,
                                  (w_gu2d, 1, d_ff))

    wr = jnp.zeros((d, LANES), F32).at[:, :n_e].set(w_router[0])
    wr_hi = wr.astype(BF16)
    wr_lo = (wr - wr_hi.astype(F32)).astype(BF16)
    br = jnp.zeros((1, LANES), F32).at[0, :n_e].set(b_router[0])
    x1, h2, logits = _outproj_call(attn_a, attn_b, proj, xp, xs, mod_seg, w_o_a[0].astype(BF16),
                                   w_o_b[0].astype(BF16), w_out[0].astype(BF16), g_norm2[0], wr_hi, wr_lo, br, seg)

    tm_e = min(1024, seg)
    lt = logits[:, :n_e].T
    e_iota = jnp.arange(n_e, dtype=jnp.int32)[:, None]
    top_vals, top_idx = [], []
    for _ in range(TOP_K):
        idx = jnp.argmax(lt, axis=0).astype(jnp.int32)
        top_vals.append(jnp.max(lt, axis=0))
        top_idx.append(idx)
        lt = jnp.where(e_iota == idx[None, :], -jnp.inf, lt)
    top_w = jax.nn.softmax(jnp.stack(top_vals), axis=0).T
    n_assign = t * TOP_K
    e_flat = jnp.stack(top_idx).reshape(-1)
    onehot = e_flat[:, None] == jnp.arange(n_e, dtype=jnp.int32)[None, :]
    pb = math.gcd(n_assign, 512)
    oh_blocks = onehot.astype(BF16).reshape(n_assign // pb, pb, n_e)
    tril = jnp.asarray(np.tril(np.ones((pb, pb), np.float32)), BF16)
    within = jnp.einsum('ij,bjk->bik', tril, oh_blocks, preferred_element_type=F32)
    totals = within[:, -1, :]
    before = jnp.cumsum(totals, axis=0) - totals
    counts = jnp.sum(totals, axis=0).astype(jnp.int32)
    padded = (counts + tm_e - 1) // tm_e * tm_e
    pend = jnp.cumsum(padded)
    slot_of = within + (before + (pend - padded).astype(F32) - 1.0)[:, None, :]
    dest = jnp.sum(jnp.where(oh_blocks > 0, slot_of, 0.0), axis=-1).astype(jnp.int32).reshape(-1)
    n_tiles = -(-n_assign // tm_e) + n_e
    n_rows = n_tiles * tm_e
    assert n_rows < 2 ** 24
    tile_e = jnp.minimum(jnp.searchsorted(pend, jnp.arange(n_tiles) * tm_e, side='right'), n_e - 1).astype(jnp.int32)
    n_valid = (pend[-1:] // tm_e).astype(jnp.int32)
    tok_flat = jnp.arange(n_assign, dtype=jnp.int32) % t
    _, tok_sorted = lax.sort((e_flat, tok_flat), num_keys=1, is_stable=True)
    tile_src0 = (jnp.cumsum(counts) - counts)[tile_e] + jnp.arange(n_tiles, dtype=jnp.int32) * tm_e - (pend - padded)[tile_e]
    tile_end = jnp.cumsum(counts)[tile_e]
    src = tile_src0[:, None] + jnp.arange(tm_e, dtype=jnp.int32)[None, :]
    slot_tok = jnp.where(src < tile_end[:, None], tok_sorted[jnp.minimum(src, n_assign - 1)],
                         (jnp.arange(n_rows, dtype=jnp.int32) % t).reshape(n_tiles, tm_e))

    n_chunks = math.gcd(n_tiles, MOE_CHUNKS)
    tiles_per_chunk = n_tiles // n_chunks
    yb = None
    for c in range(n_chunks):
        tiles = slice(c * tiles_per_chunk, (c + 1) * tiles_per_chunk)
        xin = h2[slot_tok[tiles].reshape(-1)]
        yb = _moe_call(xin, tile_e, n_valid, w_gate, w_lin, b_gu[0], w_dn, b_down[0], tm_e,
                       c * tiles_per_chunk, n_rows, yb)

    dest_kt = dest.reshape(TOP_K, t)
    chunk = math.gcd(tp, ts)
    if t // chunk > 8:
        chunk = 0
    outs = []
    for row0, n_group in ((0, tp), (tp, ts)):
        step = chunk if chunk else n_group
        y = None
        for r in range(0, n_group, step):
            ysel = yb[dest_kt[:, row0 + r:row0 + r + step]]
            y = _combine_call(x1, ysel, top_w, mod_seg, row0 + r, r, n_group, seg, y)
        outs.append(y)
    return (outs[0].reshape(bp, sp, d), outs[1].reshape(bs, ss, d))
```

```python
import functools
import math

import numpy as np
import jax
import jax.numpy as jnp
from jax import lax
from jax.experimental import pallas as pl
from jax.experimental.pallas import tpu as pltpu

F32 = jnp.float32
BF16 = jnp.bfloat16

HEAD_DIM = 64
N_HEADS_A = 16
N_KV_A = 4
GROUP_A = N_HEADS_A // N_KV_A
N_HEADS_B = 16
WINDOW = 128
ATTN_BLOCK = 128
T5_BUCKETS = 32
T5_MAX_DIST = 128
GRID_W = 64
NA_ROWS = 8
NA_COLS = 16
TOP_K = 4
SWIGLU_LIMIT = 7.0
SWIGLU_ALPHA = 1.702
RMS_EPS = 1e-6

Q_A = N_HEADS_A * HEAD_DIM
KV_A = N_KV_A * HEAD_DIM
W_B = N_HEADS_B * HEAD_DIM

LANES = 128
MXU_DIM = 256
VMEM_LIMIT_BYTES = 56 * 1024 * 1024
MASKED = -1e30
LOG2_E = math.log2(math.e)

N_HEAD_PAIRS = Q_A // LANES
COL_QA = 0
COL_QB = COL_QA + Q_A
COL_KB = COL_QB + W_B
COL_KA = COL_KB + W_B
COL_VA = COL_KA + N_KV_A * LANES
COL_VB = COL_VA + N_KV_A * LANES
COL_END = COL_VB + W_B
NA_GROUP_ROWS = 2
NA_KEY_ROWS = NA_GROUP_ROWS + NA_ROWS - 1
NA_HALO_ROWS = NA_ROWS


def _sigmoid(x):
    return 1.0 / (1.0 + jnp.exp(-x))


def _cparams(sem):
    return pltpu.CompilerParams(dimension_semantics=sem, vmem_limit_bytes=VMEM_LIMIT_BYTES)


def _ada_kernel(c_ref, w_ref, b_ref, o_ref):
    c = c_ref[...]
    a = (c * _sigmoid(c)).astype(BF16)
    o_ref[...] = jnp.dot(a, w_ref[...].astype(BF16), preferred_element_type=F32) + b_ref[...]


def _ada_call(c_pad, w_ada, b_ada):
    rows, d = c_pad.shape
    n = w_ada.shape[1]
    tn = min(n, 1024)
    return pl.pallas_call(
        _ada_kernel,
        grid=(n // tn,),
        in_specs=[pl.BlockSpec((rows, d), lambda j: (0, 0)),
                  pl.BlockSpec((d, tn), lambda j: (0, j)),
                  pl.BlockSpec((1, tn), lambda j: (0, j))],
        out_specs=pl.BlockSpec((rows, tn), lambda j: (0, j)),
        out_shape=jax.ShapeDtypeStruct((rows, n), F32),
        compiler_params=_cparams(("arbitrary",)),
        name="ada_ln",
    )(c_pad, w_ada, b_ada.reshape(1, n))


def _two_group_specs(tm, d, n_prompt_tiles, n_sample_tiles):
    xp = pl.BlockSpec((tm, d), lambda i, *_: (jnp.minimum(i, n_prompt_tiles - 1), 0))
    xs = pl.BlockSpec((tm, d), lambda i, *_: (jnp.clip(i - n_prompt_tiles, 0, n_sample_tiles - 1), 0))
    return xp, xs


def _norm_mod_kernel(xp_ref, xs_ref, mod_ref, g_ref, o_ref, *, n_prompt_tiles):
    i = pl.program_id(0)
    x = jnp.where(i < n_prompt_tiles, xp_ref[...], xs_ref[...])
    y = x * lax.rsqrt(jnp.mean(x * x, axis=-1, keepdims=True) + RMS_EPS) * g_ref[...]
    m = mod_ref[0]
    o_ref[...] = (y * (1.0 + m[1:2]) + m[0:1]).astype(BF16)


def _norm_mod_call(xp, xs, mod_seg, g, seg):
    d = xp.shape[1]
    tm = min(512, seg)
    npt, nst = xp.shape[0] // tm, xs.shape[0] // tm
    xp_spec, xs_spec = _two_group_specs(tm, d, npt, nst)
    return pl.pallas_call(
        functools.partial(_norm_mod_kernel, n_prompt_tiles=npt),
        grid=(npt + nst,),
        in_specs=[xp_spec, xs_spec,
                  pl.BlockSpec((1, 6, d), lambda i: (i * tm // seg, 0, 0)),
                  pl.BlockSpec((1, d), lambda i: (0, 0))],
        out_specs=pl.BlockSpec((tm, d), lambda i: (i, 0)),
        out_shape=jax.ShapeDtypeStruct((xp.shape[0] + xs.shape[0], d), BF16),
        compiler_params=_cparams(("arbitrary",)),
        name="norm1_mod",
    )(xp, xs, mod_seg, g.reshape(1, d))


def _group_rms(y, g, ones_ref):
    sq = (y * y).astype(BF16)
    parts = []
    for c in range(y.shape[1] // MXU_DIM):
        parts.append(jnp.dot(sq[:, c * MXU_DIM:(c + 1) * MXU_DIM], ones_ref[...], preferred_element_type=F32))
    ss = jnp.concatenate(parts, axis=1) if len(parts) > 1 else parts[0]
    return y * lax.rsqrt(ss * (1.0 / HEAD_DIM) + RMS_EPS) * g


INPROJ_ROW_CHUNKS = 2
INPROJ_CAST_ROWS = 256


def _inproj_kernel(h_ref, w_ref, g_ref, ones_ref, wc_ref, o_ref, oc_ref, *, tn, gate_tiles, n_cast_chunks):
    j = pl.program_id(1)
    _cast_rider_step(pl.program_id(0) * pl.num_programs(1) + j, n_cast_chunks, [(wc_ref, oc_ref)])
    half = tn // 2
    mixed_tile = gate_tiles + COL_KA // tn
    plain_tile = gate_tiles + COL_VB // tn
    rows = h_ref.shape[0] // INPROJ_ROW_CHUNKS

    def chunked(epilogue):
        for c in range(INPROJ_ROW_CHUNKS):
            r = slice(c * rows, (c + 1) * rows)
            epilogue(jnp.dot(h_ref[r, :], w_ref[...], preferred_element_type=F32), r)

    @pl.when(j < gate_tiles)
    def _():
        def epilogue(y, r):
            o_ref[r, :] = (0.5 * jnp.tanh(0.5 * y) + 0.5).astype(BF16)
        chunked(epilogue)

    @pl.when((j >= gate_tiles) & (j < mixed_tile))
    def _():
        def epilogue(y, r):
            o_ref[r, :] = _group_rms(y, g_ref[...], ones_ref).astype(BF16)
        chunked(epilogue)

    @pl.when(j == mixed_tile)
    def _():
        def epilogue(y, r):
            o_ref[r, :half] = _group_rms(y[:, :half], g_ref[:, :half], ones_ref).astype(BF16)
            o_ref[r, half:] = y[:, half:].astype(BF16)
        chunked(epilogue)

    @pl.when(j == plain_tile)
    def _():
        def epilogue(y, r):
            o_ref[r, :] = y.astype(BF16)
        chunked(epilogue)


def _inproj_call(h, w, gvec, ones_bd, tm, cast_src):
    t, d = h.shape
    pw = w.shape[1]
    tn = 1024
    assert (2 * d) % tn == 0 and COL_KA % tn == 0 and COL_VA - COL_KA == tn // 2 and COL_VB % tn == 0
    grid = (t // tm, pw // tn)
    rider = _CastRider(*cast_src, n_inner=grid[1], n_steps=grid[0] * grid[1], tr=INPROJ_CAST_ROWS)
    return pl.pallas_call(
        functools.partial(_inproj_kernel, tn=tn, gate_tiles=2 * d // tn, n_cast_chunks=rider.n_chunks),
        grid=grid,
        in_specs=[pl.BlockSpec((tm, d), lambda i, j: (i, 0)),
                  pl.BlockSpec((d, tn), lambda i, j: (0, j)),
                  pl.BlockSpec((1, tn), lambda i, j: (0, j)),
                  pl.BlockSpec((MXU_DIM, MXU_DIM), lambda i, j: (0, 0)),
                  rider.in_spec],
        out_specs=[pl.BlockSpec((tm, tn), lambda i, j: (i, j)), rider.out_spec],
        out_shape=[jax.ShapeDtypeStruct((t, pw), BF16), rider.out_shape],
        compiler_params=_cparams(("arbitrary", "arbitrary")),
        name="in_proj",
    )(h, w, gvec, ones_bd, rider.operand)


def _aligned(x, m):
    return x if isinstance(x, int) else pl.multiple_of(x, m)


def _three_stage_pipeline(n, stage_a, stage_b, stage_c, width=2):
    assert n % width == 0 and n // width >= 3
    trips = n // width

    def trip(t, run_a=True, run_b=True, run_c=True):
        for k in range(width):
            if run_c:
                stage_c(width * (t - 2) + k, k)
        for k in range(width):
            if run_b:
                stage_b(width * (t - 1) + k, k)
        for k in range(width):
            if run_a:
                stage_a(width * t + k, k)

    trip(0, run_b=False, run_c=False)
    trip(1, run_c=False)

    def body(t, carry):
        trip(t)
        return carry

    lax.fori_loop(2, trips, body, 0)
    trip(trips, run_a=False)
    trip(trips + 1, run_a=False, run_b=False)


WIN_WIDTH = 4
NA_WIDTH = 4
NA_VALUE_KEYS = -(-NA_KEY_ROWS * GRID_W // LANES) * LANES
CAST_ROWS = 512
MOE_CHUNKS = 8


class _CastRider:
    def __init__(self, w2d, col_block, n_cols, n_inner, n_steps, tr=CAST_ROWS):
        rows = w2d.shape[0]
        tr = min(tr, rows)
        self.n_chunks = rows // tr
        assert rows % tr == 0 and self.n_chunks <= n_steps
        self.operand = w2d

        def chunk(p, s, *_):
            return jnp.minimum(p * n_inner + s, self.n_chunks - 1)

        self.in_spec = pl.BlockSpec((tr, n_cols), lambda *g: (chunk(*g), col_block))
        self.out_spec = pl.BlockSpec((tr, n_cols), lambda *g: (chunk(*g), 0))
        self.out_shape = jax.ShapeDtypeStruct((rows, n_cols), BF16)


def _cast_rider_step(step, n_chunks, pairs):
    @pl.when(step < n_chunks)
    def _():
        for src, dst in pairs:
            dst[...] = src[...].astype(BF16)


def _win_attn_kernel(flags_ref, q_ref, kp_ref, kc_ref, kn_ref, vp_ref, vc_ref, vn_ref, bias_ref, sink_ref,
                     wc_ref, o_ref, oc_ref, kcat, vt, s_scr, e_scr, m_scr, r_scr, *, qs, nb, n_cast_chunks):
    p = pl.program_id(0)
    sb = pl.program_id(1)
    _cast_rider_step(p * pl.num_programs(1) + sb, n_cast_chunks, [(wc_ref, oc_ref)])
    blk = ATTN_BLOCK
    kcat[0:blk] = kp_ref[...]
    kcat[blk:blk + qs] = kc_ref[...]
    kcat[blk + qs:] = kn_ref[...]
    vt[:, 0:blk] = vp_ref[...].T[:HEAD_DIM]
    vt[:, blk:blk + qs] = vc_ref[...].T[:HEAD_DIM]
    vt[:, blk + qs:] = vn_ref[...].T[:HEAD_DIM]
    prev_ok = flags_ref[2 * sb]
    next_ok = flags_ref[2 * sb + 1]
    low_half = lax.broadcasted_iota(jnp.int32, (blk, LANES), 1) < HEAD_DIM
    lane2 = lax.broadcasted_iota(jnp.int32, (1, 2 * blk), 1)
    sinkrow = jnp.where(lane2 < blk, sink_ref[2 * p], sink_ref[2 * p + 1])
    kl = blk + 2 * WINDOW

    def scores(b, slot):
        r0 = _aligned(b * blk, blk)
        q = q_ref[pl.ds(r0, blk), :]
        zero = jnp.zeros_like(q)
        q2 = jnp.concatenate([jnp.where(low_half, q, zero), jnp.where(low_half, zero, q)], axis=0)
        kw = kcat[pl.ds(r0, kl), :]
        variant = jnp.where((b == 0) & (prev_ok == 0), 1, jnp.where((b == nb - 1) & (next_ok == 0), 2, 0))
        s = lax.dot_general(kw, q2, (((1,), (1,)), ((), ())), preferred_element_type=F32) + bias_ref[variant, 0]
        s_scr[slot] = s
        m_scr[slot] = jnp.maximum(jnp.max(s, axis=0, keepdims=True), sinkrow)

    def softmax(b, slot):
        m = m_scr[slot]
        e = jnp.exp2(s_scr[slot] - m)
        e_scr[slot] = e.astype(BF16)
        r_scr[slot] = 1.0 / (jnp.sum(e, axis=0, keepdims=True) + jnp.exp2(sinkrow - m))

    def values(b, slot):
        r0 = _aligned(b * blk, blk)
        ot = jnp.dot(vt[:, pl.ds(r0, kl)], e_scr[slot], preferred_element_type=F32) * r_scr[slot]
        x = jnp.concatenate([ot[:, :blk], ot[:, blk:]], axis=0)
        o_ref[pl.ds(r0, blk), :] = x.T.astype(BF16)

    _three_stage_pipeline(nb, scores, softmax, values, width=min(WIN_WIDTH, nb // 3))


def _win_attn_call(proj, bias3, sink, flags, qs, base, cast_src):
    t = proj.shape[0]
    blk = ATTN_BLOCK
    nb = qs // blk
    n_super = t // qs
    rider = _CastRider(*cast_src, n_inner=n_super, n_steps=N_HEAD_PAIRS * n_super)
    n_blk_rows = t // blk
    kl = blk + 2 * WINDOW
    ka0, va0 = base + COL_KA // LANES, base + COL_VA // LANES

    def cur(col0):
        return pl.BlockSpec((qs, LANES), lambda p, s, f: (s, col0 + p // 2))

    def prev(col0):
        return pl.BlockSpec((blk, LANES), lambda p, s, f: (jnp.maximum(s * nb - 1, 0), col0 + p // 2))

    def nxt(col0):
        return pl.BlockSpec((blk, LANES), lambda p, s, f: (jnp.minimum((s + 1) * nb, n_blk_rows - 1), col0 + p // 2))

    grid_spec = pltpu.PrefetchScalarGridSpec(
        num_scalar_prefetch=1,
        grid=(N_HEAD_PAIRS, n_super),
        in_specs=[pl.BlockSpec((qs, LANES), lambda p, s, f: (s, base + COL_QA // LANES + p)),
                  prev(ka0), cur(ka0), nxt(ka0), prev(va0), cur(va0), nxt(va0),
                  pl.BlockSpec((3, 1, kl, 2 * blk), lambda p, s, f: (0, p, 0, 0)),
                  pl.BlockSpec(memory_space=pltpu.SMEM),
                  rider.in_spec],
        out_specs=[pl.BlockSpec((qs, LANES), lambda p, s, f: (s, p)), rider.out_spec],
        scratch_shapes=[pltpu.VMEM((qs + 2 * blk, LANES), BF16), pltpu.VMEM((HEAD_DIM, qs + 2 * blk), BF16),
                        pltpu.VMEM((WIN_WIDTH, kl, 2 * blk), F32), pltpu.VMEM((WIN_WIDTH, kl, 2 * blk), BF16),
                        pltpu.VMEM((WIN_WIDTH, 1, 2 * blk), F32), pltpu.VMEM((WIN_WIDTH, 1, 2 * blk), F32)],
    )
    return pl.pallas_call(
        functools.partial(_win_attn_kernel, qs=qs, nb=nb, n_cast_chunks=rider.n_chunks),
        grid_spec=grid_spec,
        out_shape=[jax.ShapeDtypeStruct((t, Q_A), BF16), rider.out_shape],
        compiler_params=_cparams(("arbitrary", "arbitrary")),
        name="win_attn",
    )(flags, proj, proj, proj, proj, proj, proj, proj, bias3, sink, rider.operand)


def _t5_bucket(rel):
    nb = T5_BUCKETS // 2
    max_exact = nb // 2
    ret = jnp.where(rel > 0, nb, 0)
    n = jnp.abs(rel)
    nf = jnp.maximum(n, 1).astype(jnp.float32)
    large = max_exact + (jnp.log(nf / max_exact) / math.log(T5_MAX_DIST / max_exact) * (nb - max_exact)).astype(jnp.int32)
    large = jnp.minimum(large, nb - 1)
    return ret + jnp.where(n < max_exact, n, large)


def _toeplitz(v, n_rows, n_cols):
    p = v.shape[-1]
    assert n_cols <= p - 1
    flat = jnp.tile(v, (1,) * (v.ndim - 1) + (n_rows,))[..., :n_rows * (p - 1)]
    return flat.reshape(v.shape[:-1] + (n_rows, p - 1))[..., :n_cols]


def _window_bias(t5_bias):
    blk, kl = ATTN_BLOCK, ATTN_BLOCK + 2 * WINDOW
    period = blk + kl - 1
    m = np.arange(period)
    rel_of_m = np.where(m < kl, m, m - period) - WINDOW
    per_rel = t5_bias[_t5_bucket(jnp.asarray(rel_of_m))].astype(F32).T
    bias = _toeplitz(per_rel * LOG2_E, blk, kl)
    col = np.arange(kl)[None, :]
    band = np.abs(col - WINDOW - np.arange(blk)[:, None]) <= WINDOW
    keep = np.stack([band, band & (col >= WINDOW), band & (col < WINDOW + blk)])
    tab = jnp.where(keep[:, None], bias[None], MASKED)
    tab = tab.reshape(3, N_HEADS_A // 2, 2, blk, kl).transpose(0, 1, 4, 2, 3)
    return tab.reshape(3, N_HEADS_A // 2, kl, 2 * blk)


def _na_plan(seq_lens, seg):
    qr, nkr = NA_GROUP_ROWS, NA_KEY_ROWS
    configs, cfg_ids, kstarts = {}, [], []
    tok = 0
    for s_len in seq_lens:
        rows = s_len // GRID_W
        assert rows >= nkr and rows % qr == 0
        for r in range(0, rows, qr):
            us = min(max(r - NA_ROWS // 2, 0), rows - NA_ROWS)
            rel = tuple(min(max(r + q - NA_ROWS // 2, 0), rows - NA_ROWS) - us for q in range(qr))
            key = (r - us, rel)
            cfg_ids.append(configs.setdefault(key, len(configs)))
            g_tok = tok + r * GRID_W
            seg_tok0 = (g_tok // seg) * seg
            kstarts.append(tok + us * GRID_W - seg_tok0 + NA_HALO_ROWS * GRID_W)
        tok += s_len
    blocks = [None] * len(configs)
    for (r_us, rel), c in configs.items():
        blocks[c] = [[kr - r_us - q + NA_ROWS - 1 if rel[q] <= kr < rel[q] + NA_ROWS else None
                      for kr in range(nkr)] for q in range(qr)]
    return np.asarray(cfg_ids, np.int32), np.asarray(kstarts, np.int32), blocks


def _na_bias_tables(na_bias, blocks):
    period = 2 * GRID_W - 1
    m = np.arange(period)
    dcol = np.where(m < GRID_W, m, m - period)
    per_dcol = na_bias.astype(F32)[:, :, np.clip(dcol + NA_COLS - 1, 0, 2 * NA_COLS - 2)]
    col_tab = _toeplitz(per_dcol * LOG2_E, GRID_W, GRID_W)
    q_col, k_col = np.arange(GRID_W)[:, None], np.arange(GRID_W)[None, :]
    win_start = np.clip(q_col - NA_COLS // 2, 0, GRID_W - NA_COLS)
    col_ok = (k_col >= win_start) & (k_col < win_start + NA_COLS)
    col_tab = jnp.where(col_ok, col_tab, MASKED)
    masked_blk = jnp.full(col_tab.shape[:1] + col_tab.shape[2:], MASKED, F32)
    cfgs = []
    for cfg in blocks:
        rows = [jnp.concatenate([masked_blk if dr is None else col_tab[:, dr] for dr in qrow], axis=-1) for qrow in cfg]
        cfgs.append(jnp.concatenate(rows, axis=-2))
    tab = jnp.stack(cfgs)
    ncfg, n_h, m, nk = tab.shape
    tab = tab.reshape(ncfg, n_h // 2, 2, m, nk).transpose(0, 1, 4, 2, 3)
    return tab.reshape(ncfg, n_h // 2, nk, 2 * m)


def _na_attn_kernel(cfg_ref, ks_ref, q_ref, kp_ref, kc_ref, kn_ref, vp_ref, vc_ref, vn_ref, bias_ref,
                    wc_ref, o_ref, oc_ref, kcat, vt, s_scr, e_scr, m_scr, r_scr,
                    *, seg, n_cast_chunks):
    s_id = pl.program_id(1)
    _cast_rider_step(pl.program_id(0) * pl.num_programs(1) + s_id, n_cast_chunks, [(wc_ref, oc_ref)])
    halo = NA_HALO_ROWS * GRID_W
    m, nk = NA_GROUP_ROWS * GRID_W, NA_KEY_ROWS * GRID_W
    n_groups = seg // m
    kcat[0:halo] = kp_ref[...]
    kcat[halo:halo + seg] = kc_ref[...]
    kcat[halo + seg:] = kn_ref[...]
    vt[:, 0:halo] = vp_ref[...].T
    vt[:, halo:halo + seg] = vc_ref[...].T
    vt[:, halo + seg:] = vn_ref[...].T
    e_scr[:, nk:, :] = jnp.zeros((e_scr.shape[0], NA_VALUE_KEYS - nk, 2 * m), BF16)
    low_half = lax.broadcasted_iota(jnp.int32, (m, LANES), 1) < HEAD_DIM
    top_rows = lax.broadcasted_iota(jnp.int32, (LANES, m), 0) < HEAD_DIM

    def scores(g, slot):
        gg = s_id * n_groups + g
        ks = pl.multiple_of(ks_ref[gg], LANES)
        r0 = _aligned(g * m, m)
        q = q_ref[pl.ds(r0, m), :]
        zero = jnp.zeros_like(q)
        q2 = jnp.concatenate([jnp.where(low_half, q, zero), jnp.where(low_half, zero, q)], axis=0)
        kw = kcat[pl.ds(ks, nk), :]
        s = lax.dot_general(kw, q2, (((1,), (1,)), ((), ())), preferred_element_type=F32) + bias_ref[cfg_ref[gg], 0]
        s_scr[slot] = s
        m_scr[slot] = jnp.max(s, axis=0, keepdims=True)

    def softmax(g, slot):
        e = jnp.exp2(s_scr[slot] - m_scr[slot])
        e_scr[slot, :nk, :] = e.astype(BF16)
        r_scr[slot] = 1.0 / jnp.sum(e, axis=0, keepdims=True)

    def values(g, slot):
        ks = pl.multiple_of(ks_ref[s_id * n_groups + g], LANES)
        r0 = _aligned(g * m, m)
        ot = jnp.dot(vt[:, pl.ds(ks, NA_VALUE_KEYS)], e_scr[slot], preferred_element_type=F32) * r_scr[slot]
        x = jnp.where(top_rows, ot[:, :m], ot[:, m:])
        o_ref[pl.ds(r0, m), :] = x.T.astype(BF16)

    _three_stage_pipeline(n_groups, scores, softmax, values, width=min(NA_WIDTH, n_groups // 3))


def _na_attn_call(proj, bias_cfg, cfg_ids, kstarts, seg, base, cast_src):
    t = proj.shape[0]
    halo = NA_HALO_ROWS * GRID_W
    n_seg = t // seg
    rider = _CastRider(*cast_src, n_inner=n_seg, n_steps=N_HEAD_PAIRS * n_seg)
    per = seg // halo
    n_halo_blocks = t // halo
    ncfg = bias_cfg.shape[0]
    m, nk = NA_GROUP_ROWS * GRID_W, NA_KEY_ROWS * GRID_W
    kb0, vb0 = base + COL_KB // LANES, base + COL_VB // LANES

    def cur(col0):
        return pl.BlockSpec((seg, LANES), lambda p, s, c, k: (s, col0 + p))

    def prev(col0):
        return pl.BlockSpec((halo, LANES), lambda p, s, c, k: (jnp.maximum(s * per - 1, 0), col0 + p))

    def nxt(col0):
        return pl.BlockSpec((halo, LANES), lambda p, s, c, k: (jnp.minimum((s + 1) * per, n_halo_blocks - 1), col0 + p))

    grid_spec = pltpu.PrefetchScalarGridSpec(
        num_scalar_prefetch=2,
        grid=(N_HEAD_PAIRS, n_seg),
        in_specs=[cur(base + COL_QB // LANES), prev(kb0), cur(kb0), nxt(kb0), prev(vb0), cur(vb0), nxt(vb0),
                  pl.BlockSpec((ncfg, 1, nk, 2 * m), lambda p, s, c, k: (0, p, 0, 0)),
                  rider.in_spec],
        out_specs=[pl.BlockSpec((seg, LANES), lambda p, s, c, k: (s, p)), rider.out_spec],
        scratch_shapes=[pltpu.VMEM((seg + 2 * halo, LANES), BF16), pltpu.VMEM((LANES, seg + 2 * halo), BF16),
                        pltpu.VMEM((NA_WIDTH, nk, 2 * m), F32), pltpu.VMEM((NA_WIDTH, NA_VALUE_KEYS, 2 * m), BF16),
                        pltpu.VMEM((NA_WIDTH, 1, 2 * m), F32), pltpu.VMEM((NA_WIDTH, 1, 2 * m), F32)],
    )
    return pl.pallas_call(
        functools.partial(_na_attn_kernel, seg=seg, n_cast_chunks=rider.n_chunks),
        grid_spec=grid_spec,
        out_shape=[jax.ShapeDtypeStruct((t, W_B), BF16), rider.out_shape],
        compiler_params=_cparams(("arbitrary", "arbitrary")),
        name="na_attn",
    )(cfg_ids, kstarts, proj, proj, proj, proj, proj, proj, proj, bias_cfg, rider.operand)


def _outproj_kernel(aa_ref, ab_ref, ga_ref, gb_ref, xp_ref, xs_ref, mod_ref, woa_ref, wob_ref, wout_ref,
                    g2_ref, wrh_ref, wrl_ref, br_ref, x1_ref, h2_ref, lg_ref, *, n_prompt_tiles):
    i = pl.program_id(0)
    ya = jnp.dot(aa_ref[...], woa_ref[...], preferred_element_type=F32)
    yb = jnp.dot(ab_ref[...], wob_ref[...], preferred_element_type=F32)
    merged = ga_ref[...].astype(F32) * ya + gb_ref[...].astype(F32) * yb
    z = jnp.dot(merged.astype(BF16), wout_ref[...], preferred_element_type=F32)
    x = jnp.where(i < n_prompt_tiles, xp_ref[...], xs_ref[...])
    m = mod_ref[0]
    x1 = x + m[2:3] * z
    x1_ref[...] = x1
    y = x1 * lax.rsqrt(jnp.mean(x1 * x1, axis=-1, keepdims=True) + RMS_EPS) * g2_ref[...]
    h2 = y * (1.0 + m[4:5]) + m[3:4]
    h2_hi = h2.astype(BF16)
    h2_ref[...] = h2_hi
    h2_lo = (h2 - h2_hi.astype(F32)).astype(BF16)
    lg = jnp.dot(h2_hi, wrh_ref[...], preferred_element_type=F32)
    lg = lg + jnp.dot(h2_lo, wrh_ref[...], preferred_element_type=F32)
    lg = lg + jnp.dot(h2_hi, wrl_ref[...], preferred_element_type=F32)
    lg_ref[...] = lg + br_ref[...]


def _outproj_call(attn_a, attn_b, proj, xp, xs, mod_seg, woa, wob, wout, g2, wr_hi, wr_lo, br, seg):
    t = attn_a.shape[0]
    d = xp.shape[1]
    tm = min(256, seg)
    npt, nst = xp.shape[0] // tm, xs.shape[0] // tm
    xp_spec, xs_spec = _two_group_specs(tm, d, npt, nst)
    def const(shape):
        return pl.BlockSpec(shape, lambda i: (0,) * len(shape), pipeline_mode=pl.Buffered(1))

    return pl.pallas_call(
        functools.partial(_outproj_kernel, n_prompt_tiles=npt),
        grid=(npt + nst,),
        in_specs=[pl.BlockSpec((tm, Q_A), lambda i: (i, 0)),
                  pl.BlockSpec((tm, W_B), lambda i: (i, 0)),
                  pl.BlockSpec((tm, d), lambda i: (i, 0)),
                  pl.BlockSpec((tm, d), lambda i: (i, 1)),
                  xp_spec, xs_spec,
                  pl.BlockSpec((1, 6, d), lambda i: (i * tm // seg, 0, 0)),
                  const((Q_A, d)), const((W_B, d)), const((d, d)), const((1, d)),
                  const((d, LANES)), const((d, LANES)), const((1, LANES))],
        out_specs=[pl.BlockSpec((tm, d), lambda i: (i, 0)),
                   pl.BlockSpec((tm, d), lambda i: (i, 0)),
                   pl.BlockSpec((tm, LANES), lambda i: (i, 0))],
        out_shape=[jax.ShapeDtypeStruct((t, d), F32),
                   jax.ShapeDtypeStruct((t, d), BF16),
                   jax.ShapeDtypeStruct((t, LANES), F32)],
        compiler_params=_cparams(("arbitrary",)),
        name="out_proj",
    )(attn_a, attn_b, proj, proj, xp, xs, mod_seg, woa, wob, wout, g2.reshape(1, d), wr_hi, wr_lo, br)


def _moe_kernel(te_ref, nv_ref, x_ref, wg_ref, wl_ref, bg_ref, bl_ref, wd_ref, bd_ref, *rest, nf, tile0):
    o_ref, acc_ref = rest[-2:]
    t = pl.program_id(0) + tile0
    f = pl.program_id(1)
    live = t < nv_ref[0]

    @pl.when((pl.program_id(0) == 0) & (f == 0))
    def _():
        acc_ref[...] = jnp.zeros_like(acc_ref)

    @pl.when(live)
    def _():
        x = x_ref[...]
        gate = jnp.dot(x, wg_ref[...], preferred_element_type=F32) + bg_ref[0]
        lin = jnp.dot(x, wl_ref[...], preferred_element_type=F32) + bl_ref[0]
        gate = jnp.minimum(gate, SWIGLU_LIMIT)
        lin = jnp.clip(lin, -SWIGLU_LIMIT, SWIGLU_LIMIT)
        act = gate * _sigmoid(SWIGLU_ALPHA * gate) * (lin + 1.0)
        part = jnp.dot(act.astype(BF16), wd_ref[...], preferred_element_type=F32)
        total = part + jnp.where(f == 0, bd_ref[0], acc_ref[...])
        acc_ref[...] = total
        o_ref[...] = total.astype(BF16)

    @pl.when(jnp.logical_not(live) & (f == 0))
    def _():
        o_ref[...] = jnp.zeros_like(o_ref)


def _moe_call(xin, tile_e, n_valid, w_gate, w_lin, b_gu, w_down, b_down, tm, tile0, n_rows_total, yb_prev):
    n_rows, d = xin.shape
    n_e, two_ff = b_gu.shape
    d_ff = two_ff // 2
    tf = min(512, d_ff)
    nf = d_ff // tf
    n_tiles = n_rows // tm

    def fidx(t, f, nv):
        return jnp.where(t + tile0 < nv[0], f, nf - 1)

    def expert(t, te):
        return te[t + tile0]

    in_specs = [pl.BlockSpec((tm, d), lambda t, f, te, nv: (t, 0)),
                pl.BlockSpec((d, tf), lambda t, f, te, nv: (expert(t, te), fidx(t, f, nv))),
                pl.BlockSpec((d, tf), lambda t, f, te, nv: (expert(t, te), fidx(t, f, nv))),
                pl.BlockSpec((1, 1, tf), lambda t, f, te, nv: (expert(t, te), 0, fidx(t, f, nv))),
                pl.BlockSpec((1, 1, tf), lambda t, f, te, nv: (expert(t, te), 0, fidx(t, f, nv) + nf)),
                pl.BlockSpec((tf, d), lambda t, f, te, nv: (expert(t, te) * nf + fidx(t, f, nv), 0)),
                pl.BlockSpec((1, 1, d), lambda t, f, te, nv: (expert(t, te), 0, 0))]
    operands = [tile_e, n_valid, xin, w_gate, w_lin, b_gu.reshape(n_e, 1, two_ff), b_gu.reshape(n_e, 1, two_ff),
                w_down, b_down.reshape(n_e, 1, d)]
    aliases = {}
    if yb_prev is not None:
        in_specs.append(pl.BlockSpec(memory_space=pl.ANY))
        aliases = {len(operands): 0}
        operands.append(yb_prev)
    grid_spec = pltpu.PrefetchScalarGridSpec(
        num_scalar_prefetch=2,
        grid=(n_tiles, nf),
        in_specs=in_specs,
        out_specs=pl.BlockSpec((tm, d), lambda t, f, te, nv: (t + tile0, 0)),
        scratch_shapes=[pltpu.VMEM((tm, d), F32)],
    )
    return pl.pallas_call(
        functools.partial(_moe_kernel, nf=nf, tile0=tile0),
        grid_spec=grid_spec,
        out_shape=jax.ShapeDtypeStruct((n_rows_total, d), BF16),
        input_output_aliases=aliases,
        compiler_params=_cparams(("arbitrary", "arbitrary")),
        name="moe_experts",
    )(*operands)


def _combine_kernel(x1_ref, ys_ref, w_ref, mod_ref, *rest):
    o_ref = rest[-1]
    w = w_ref[...]
    acc = w[:, 0:1] * ys_ref[0].astype(F32)
    for k in range(1, TOP_K):
        acc = acc + w[:, k:k + 1] * ys_ref[k].astype(F32)
    o_ref[...] = x1_ref[...] + mod_ref[0][5:6] * acc


def _combine_call(x1, ysel, top_w, mod_seg, row0, out_row0, n_out_rows, seg, prev):
    d = x1.shape[1]
    n_rows = ysel.shape[1]
    tm = min(512, seg)
    off, out_off = row0 // tm, out_row0 // tm
    in_specs = [pl.BlockSpec((tm, d), lambda i: (i + off, 0)),
                pl.BlockSpec((TOP_K, tm, d), lambda i: (0, i, 0)),
                pl.BlockSpec((tm, TOP_K), lambda i: (i + off, 0)),
                pl.BlockSpec((1, 6, d), lambda i: ((i + off) * tm // seg, 0, 0))]
    operands = [x1, ysel, top_w, mod_seg]
    aliases = {}
    if prev is not None:
        in_specs.append(pl.BlockSpec(memory_space=pl.ANY))
        aliases = {len(operands): 0}
        operands.append(prev)
    return pl.pallas_call(
        _combine_kernel,
        grid=(n_rows // tm,),
        in_specs=in_specs,
        out_specs=pl.BlockSpec((tm, d), lambda i: (i + out_off, 0)),
        out_shape=jax.ShapeDtypeStruct((n_out_rows, d), F32),
        input_output_aliases=aliases,
        compiler_params=_cparams(("arbitrary",)),
        name="moe_combine",
    )(*operands)


def _pack_w_in(w_in, g_q_a, g_k_a, g_q_b, g_k_b):
    o1 = Q_A
    o2 = o1 + KV_A
    o3 = o2 + KV_A
    o4 = o3 + W_B
    o5 = o4 + W_B
    o6 = o5 + W_B
    d = w_in.shape[0]
    wqa, wka, wva = w_in[:, :o1], w_in[:, o1:o2], w_in[:, o2:o3]
    wqb, wkb, wvb, wg = w_in[:, o3:o4], w_in[:, o4:o5], w_in[:, o5:o6], w_in[:, o6:]

    def dup(w):
        w4 = w.reshape(d, N_KV_A, 1, HEAD_DIM)
        return jnp.broadcast_to(w4, (d, N_KV_A, 2, HEAD_DIM)).reshape(d, N_KV_A * LANES)

    w = jnp.concatenate([wg, wqa, wqb, wkb, dup(wka), dup(wva), wvb], axis=1).astype(BF16)
    q_scale = HEAD_DIM ** -0.5 * LOG2_E
    gvec = jnp.concatenate([jnp.ones((2 * d,), F32),
                            jnp.tile(g_q_a * q_scale, N_HEADS_A), jnp.tile(g_q_b * q_scale, N_HEADS_B),
                            jnp.tile(g_k_b, N_HEADS_B), jnp.tile(g_k_a, 2 * N_KV_A),
                            jnp.ones((COL_END - COL_VA,), F32)]).reshape(1, -1).astype(F32)
    return w, gvec


def kernel(x_prompt, x_sample, c_prompt, c_sample, w_ada, b_ada, g_norm1, w_in, g_q_a, g_k_a, g_q_b, g_k_b, sink_a, t5_bias, na_bias, w_o_a, w_o_b, w_out, g_norm2, w_router, b_router, w_gu, b_gu, w_down, b_down):
    bp, sp, d = x_prompt.shape
    bs, ss, _ = x_sample.shape
    n_e = w_router.shape[-1]
    tp, ts = bp * sp, bs * ss
    t = tp + ts
    seg = math.gcd(math.gcd(sp, ss), 2048)
    assert w_ada.shape[0] == 1 and sp % seg == 0 and ss % seg == 0 and seg % (NA_HALO_ROWS * GRID_W) == 0
    assert 2 * d == w_in.shape[-1] - (Q_A + 2 * KV_A + 3 * W_B)
    base = 2 * d // LANES
    xp = x_prompt.reshape(tp, d)
    xs = x_sample.reshape(ts, d)

    n_c = bp + bs
    c_pad = jnp.zeros((-(-n_c // 16) * 16, d), F32).at[:n_c].set(jnp.concatenate([c_prompt, c_sample], axis=0))
    mod = _ada_call(c_pad, w_ada[0], b_ada[0])[:n_c].reshape(n_c, 6, d)
    seg_owner = np.concatenate([np.repeat(np.arange(bp), sp // seg), bp + np.repeat(np.arange(bs), ss // seg)])
    mod_seg = mod[seg_owner]

    h1 = _norm_mod_call(xp, xs, mod_seg, g_norm1[0], seg)

    w_slab, gvec = _pack_w_in(w_in[0], g_q_a[0], g_k_a[0], g_q_b[0], g_k_b[0])
    grp = np.arange(MXU_DIM) // HEAD_DIM
    ones_bd = jnp.asarray(grp[:, None] == grp[None, :], BF16)
    n_e, d_ff = w_down.shape[1], w_down.shape[2]
    assert d_ff == d and w_gu.shape[1:] == (n_e, d, 2 * d_ff)
    w_gu2d = w_gu[0].reshape(n_e * d, 2 * d_ff)
    w_down2d = w_down[0].reshape(n_e * d_ff, d)
    proj, w_dn = _inproj_call(h1, w_slab, gvec, ones_bd, min(1024, seg), (w_down2d, 0, d))

    qs = min(2048, seg)
    sb_tok = np.arange(t // qs) * qs
    seq_start = np.where(sb_tok < tp, sb_tok // sp * sp, tp + (sb_tok - tp) // ss * ss)
    seq_len = np.where(sb_tok < tp, sp, ss)
    flags = np.stack([sb_tok != seq_start, sb_tok + qs != seq_start + seq_len], axis=1).astype(np.int32).reshape(-1)
    attn_a, w_gate = _win_attn_call(proj, _window_bias(t5_bias), sink_a[0].astype(F32) * LOG2_E, jnp.asarray(flags), qs, base,
                                    (w_gu2d, 0, d_ff))

    cfg_ids, kstarts, cfg_blocks = _na_plan([sp] * bp + [ss] * bs, seg)
    bias_cfg = _na_bias_tables(na_bias[0], cfg_blocks)
    attn_b, w_lin = _na_attn_call(proj, bias_cfg, jnp.asarray(cfg_ids), jnp.asarray(kstarts), seg, base,
                                  (w_gu2d, 1, d_ff))

    wr = jnp.zeros((d, LANES), F32).at[:, :n_e].set(w_router[0])
    wr_hi = wr.astype(BF16)
    wr_lo = (wr - wr_hi.astype(F32)).astype(BF16)
    br = jnp.zeros((1, LANES), F32).at[0, :n_e].set(b_router[0])
    x1, h2, logits = _outproj_call(attn_a, attn_b, proj, xp, xs, mod_seg, w_o_a[0].astype(BF16),
                                   w_o_b[0].astype(BF16), w_out[0].astype(BF16), g_norm2[0], wr_hi, wr_lo, br, seg)

    tm_e = min(1024, seg)
    lt = logits[:, :n_e].T
    e_iota = jnp.arange(n_e, dtype=jnp.int32)[:, None]
    top_vals, top_idx = [], []
    for _ in range(TOP_K):
        idx = jnp.argmax(lt, axis=0).astype(jnp.int32)
        top_vals.append(jnp.max(lt, axis=0))
        top_idx.append(idx)
        lt = jnp.where(e_iota == idx[None, :], -jnp.inf, lt)
    top_w = jax.nn.softmax(jnp.stack(top_vals), axis=0).T
    n_assign = t * TOP_K
    e_flat = jnp.stack(top_idx).reshape(-1)
    onehot = e_flat[:, None] == jnp.arange(n_e, dtype=jnp.int32)[None, :]
    pb = math.gcd(n_assign, 512)
    oh_blocks = onehot.astype(BF16).reshape(n_assign // pb, pb, n_e)
    tril = jnp.asarray(np.tril(np.ones((pb, pb), np.float32)), BF16)
    within = jnp.einsum('ij,bjk->bik', tril, oh_blocks, preferred_element_type=F32)
    totals = within[:, -1, :]
    before = jnp.cumsum(totals, axis=0) - totals
    counts = jnp.sum(totals, axis=0).astype(jnp.int32)
    padded = (counts + tm_e - 1) // tm_e * tm_e
    pend = jnp.cumsum(padded)
    slot_of = within + (before + (pend - padded).astype(F32) - 1.0)[:, None, :]
    dest = jnp.sum(jnp.where(oh_blocks > 0, slot_of, 0.0), axis=-1).astype(jnp.int32).reshape(-1)
    n_tiles = -(-n_assign // tm_e) + n_e
    n_rows = n_tiles * tm_e
    assert n_rows < 2 ** 24
    tile_row0 = jnp.arange(n_tiles, dtype=jnp.int32) * tm_e
    tile_e = jnp.minimum(jnp.sum(pend[None, :] <= tile_row0[:, None], axis=1), n_e - 1).astype(jnp.int32)
    n_valid = (pend[-1:] // tm_e).astype(jnp.int32)
    assert n_e * n_assign < 2 ** 31
    packed = jnp.sort(e_flat * n_assign + jnp.arange(n_assign, dtype=jnp.int32))
    tok_sorted = (packed % n_assign) % t
    tile_src0 = (jnp.cumsum(counts) - counts)[tile_e] + jnp.arange(n_tiles, dtype=jnp.int32) * tm_e - (pend - padded)[tile_e]
    tile_end = jnp.cumsum(counts)[tile_e]
    src = tile_src0[:, None] + jnp.arange(tm_e, dtype=jnp.int32)[None, :]
    slot_tok = jnp.where(src < tile_end[:, None], tok_sorted[jnp.minimum(src, n_assign - 1)],
                         (jnp.arange(n_rows, dtype=jnp.int32) % t).reshape(n_tiles, tm_e))

    n_chunks = math.gcd(n_tiles, MOE_CHUNKS)
    tiles_per_chunk = n_tiles // n_chunks
    yb = None
    after = jnp.zeros((1,), jnp.int32)
    for c in range(n_chunks):
        tiles = slice(c * tiles_per_chunk, (c + 1) * tiles_per_chunk)
        xin = h2[slot_tok[tiles].reshape(-1) + after]
        if yb is not None:
            done_row = yb[(c - 1) * tiles_per_chunk * tm_e, :1]
            after = lax.bitcast_convert_type(done_row, jnp.uint16).astype(jnp.int32) >> 16
        yb = _moe_call(xin, tile_e, n_valid, w_gate, w_lin, b_gu[0], w_dn, b_down[0], tm_e,
                       c * tiles_per_chunk, n_rows, yb)

    dest_kt = dest.reshape(TOP_K, t)
    chunk = math.gcd(tp, ts)
    if t // chunk > 8:
        chunk = 0
    outs = []
    for row0, n_group in ((0, tp), (tp, ts)):
        step = chunk if chunk else n_group
        y = None
        for r in range(0, n_group, step):
            ysel = yb[dest_kt[:, row0 + r:row0 + r + step]]
            y = _combine_call(x1, ysel, top_w, mod_seg, row0 + r, r, n_group, seg, y)
        outs.append(y)
    return (outs[0].reshape(bp, sp, d), outs[1].reshape(bs, ss, d))
```
